```python
import math
import jax, jax.numpy as jnp
from jax import lax
import numpy as np

D_MODEL = 1024
BATCH = 1
SEQ = 16384
DEPTH = 2
DEC_BATCH = 8
DEC_SEQ = 32
PAST_LEN = 4096

CHUNK = 64
Q_BLOCK = 128
CS_BLOCK = 128
H_A = 4
HD_A = 64
H_B = 4
HD_B = 128
N_EXPERTS = 16
N_GROUPS = 4
EXPERTS_PER_GROUP = N_EXPERTS // N_GROUPS
TOP_K = 2
D_EXPERT = 256
ROPE_THETA = 10000.0
EPS = 1e-6
NEG_INF = -1e30

W_A_QK = H_A * 2 * HD_A
W_A_V = H_A * 2 * HD_A
W_B = H_B * HD_B
SPLIT_SIZES = (W_A_QK, W_A_QK, W_A_V, W_B, W_B, W_B, D_MODEL, D_MODEL)
IN_COLS = sum(SPLIT_SIZES)

kernel_name = "hybrid_diffattn_stickbreak_grouped_moe_stream_step"


def rmsnorm(x, g):
    xf = x.astype(jnp.float32)
    r = lax.rsqrt(jnp.mean(xf * xf, axis=-1, keepdims=True) + EPS)
    return (xf * r).astype(x.dtype) * g


def rope(x, pos):
    d = x.shape[-1]
    half = d // 2
    inv = ROPE_THETA ** (-jnp.arange(half, dtype=jnp.float32) / half)
    ang = pos.astype(jnp.float32)[:, None] * inv[None, :]
    cos = jnp.cos(ang)[:, None, :].astype(x.dtype)
    sin = jnp.sin(ang)[:, None, :].astype(x.dtype)
    x1, x2 = x[..., :half], x[..., half:]
    return jnp.concatenate([x1 * cos - x2 * sin, x2 * cos + x1 * sin], axis=-1)


def rev_cumsum(x):
    n = x.shape[-1]
    pad = (-n) % CS_BLOCK
    xp = jnp.pad(x, [(0, 0)] * (x.ndim - 1) + [(0, pad)])
    nb = xp.shape[-1] // CS_BLOCK
    xb = xp.reshape(x.shape[:-1] + (nb, CS_BLOCK))
    i = jnp.arange(CS_BLOCK)
    upper = (i[:, None] >= i[None, :]).astype(x.dtype)
    within = jnp.einsum('...bj,js->...bs', xb, upper)
    bi = jnp.arange(nb)
    later_m = (bi[:, None] > bi[None, :]).astype(x.dtype)
    later = jnp.einsum('...c,cb->...b', jnp.sum(xb, axis=-1), later_m)
    return (within + later[..., None]).reshape(xp.shape)[..., :n]


def diff_attn_block(q, k, v, q_pos, k_pos, lam):
    s = jnp.einsum('bqhmd,bkhmd->bmhqk', q, k).astype(jnp.float32) * (HD_A ** -0.5)
    mask = (k_pos[None, :] // CHUNK) <= (q_pos[:, None] // CHUNK)
    p = jax.nn.softmax(jnp.where(mask, s, NEG_INF), axis=-1)
    a = (p[:, 0] - lam * p[:, 1]).astype(v.dtype)
    return jnp.einsum('bhqk,bkhe->bqhe', a, v)


def stick_block(q, k, v, q_pos, k_pos):
    z = jnp.einsum('bqhd,bkhd->bhqk', q, k).astype(jnp.float32) * (HD_B ** -0.5)
    mask = k_pos[None, :] < q_pos[:, None]
    lneg = jnp.where(mask, jax.nn.log_sigmoid(-z), 0.0)
    a = jnp.exp(jnp.where(mask, z + rev_cumsum(lneg), NEG_INF)).astype(v.dtype)
    return jnp.einsum('bhqk,bkhd->bqhd', a, v)


def attend(fn, q, keys, q_pos, k_pos, prompt):
    T = q.shape[1]
    if (not prompt) or T <= Q_BLOCK:
        return fn(q, keys, q_pos, k_pos)
    outs = []
    for i in range(T // Q_BLOCK):
        s, e = i * Q_BLOCK, (i + 1) * Q_BLOCK
        outs.append(fn(q[:, s:e], tuple(k[:, :e] for k in keys), q_pos[s:e], k_pos[:e]))
    return jnp.concatenate(outs, axis=1)


def mixer(h, pos, l, past, w_in, w_proj_a, w_proj_b, w_out, a_lambda, a_subln):
    B, T, _ = h.shape
    proj = h @ w_in[l]
    idx = [int(i) for i in np.cumsum(SPLIT_SIZES)[:-1]]
    qa, ka, va, qb, kb, vb, ga, gb = jnp.split(proj, idx, axis=-1)
    qa = rope(qa.reshape(B, T, 2 * H_A, HD_A), pos).reshape(B, T, H_A, 2, HD_A)
    ka = rope(ka.reshape(B, T, 2 * H_A, HD_A), pos).reshape(B, T, H_A, 2 * HD_A)
    va = va.reshape(B, T, H_A, 2 * HD_A)
    qb = qb.reshape(B, T, H_B, HD_B)
    kb = kb.reshape(B, T, H_B, HD_B)
    vb = vb.reshape(B, T, H_B, HD_B)
    prompt = past is None
    if prompt:
        ka_all, va_all, kb_all, vb_all, k_pos = ka, va, kb, vb, pos
    else:
        pak, pav, pbk, pbv = past
        ka_all = jnp.concatenate([pak, ka], axis=1)
        va_all = jnp.concatenate([pav, va], axis=1)
        kb_all = jnp.concatenate([pbk, kb], axis=1)
        vb_all = jnp.concatenate([pbv, vb], axis=1)
        k_pos = jnp.concatenate([jnp.arange(pak.shape[1], dtype=jnp.int32), pos])
    lam_init = 0.8 - 0.6 * math.exp(-0.3 * l)
    lw = a_lambda[l].astype(jnp.float32)
    lam = jnp.exp(jnp.sum(lw[0] * lw[1])) - jnp.exp(jnp.sum(lw[2] * lw[3])) + lam_init
    ka5 = ka_all.reshape(B, ka_all.shape[1], H_A, 2, HD_A)
    oa = attend(lambda q, ks, qp, kp: diff_attn_block(q, ks[0], ks[1], qp, kp, lam),
                qa, (ka5, va_all), pos, k_pos, prompt)
    oa = rmsnorm(oa, a_subln[l]) * (1.0 - lam_init)
    ob = attend(lambda q, ks, qp, kp: stick_block(q, ks[0], ks[1], qp, kp),
                qb, (kb_all, vb_all), pos, k_pos, prompt)
    y = (jax.nn.sigmoid(ga) * (oa.reshape(B, T, W_A_V) @ w_proj_a[l])
         + jax.nn.sigmoid(gb) * (ob.reshape(B, T, W_B) @ w_proj_b[l]))
    return y @ w_out[l], (ka, va, kb, vb)


def moe(h, l, w_router, b_router, w_e_gate, w_e_up, w_e_down):
    B, T, D = h.shape
    xt = h.reshape(B * T, D)
    aff = jax.nn.sigmoid((xt @ w_router).astype(jnp.float32))
    sel = (aff + b_router.astype(jnp.float32)).reshape(-1, N_GROUPS, EXPERTS_PER_GROUP)
    grp_score = jnp.sum(lax.top_k(sel, 2)[0], axis=-1)
    g_idx = jnp.argmax(grp_score, axis=-1)
    in_grp = jnp.take_along_axis(sel, g_idx[:, None, None], axis=1)[:, 0]
    _, loc = lax.top_k(in_grp, TOP_K)
    e_idx = g_idx[:, None] * EXPERTS_PER_GROUP + loc
    w = jnp.take_along_axis(aff, e_idx, axis=-1)
    w = w / jnp.sum(w, axis=-1, keepdims=True)
    gates = jnp.sum(jax.nn.one_hot(e_idx, N_EXPERTS, dtype=jnp.float32) * w[..., None], axis=1)
    gates = gates.astype(h.dtype)
    out = jnp.zeros_like(xt)
    for e in range(N_EXPERTS):
        he = jax.nn.silu(xt @ w_e_gate[l, e]) * (xt @ w_e_up[l, e])
        out = out + gates[:, e:e + 1] * (he @ w_e_down[l, e])
    return out.reshape(B, T, D)


def trunk(x, c, pos, past_all, w_in, w_proj_a, w_proj_b, w_out, a_lambda, a_subln,
          w_ada, b_ada, norm_mix, norm_ffn, norm_final, w_router, b_router,
          w_e_gate, w_e_up, w_e_down):
    ak, av, bk, bv = [], [], [], []
    for l in range(DEPTH):
        mod = jax.nn.silu(c) @ w_ada[l] + b_ada[l]
        sh1, sc1, g1, sh2, sc2, g2 = [m[:, None, :] for m in jnp.split(mod, 6, axis=-1)]
        h = rmsnorm(x, norm_mix[l]) * (1.0 + sc1) + sh1
        past = None if past_all is None else tuple(p[l] for p in past_all)
        y, (ka, va, kb, vb) = mixer(h, pos, l, past, w_in, w_proj_a, w_proj_b, w_out,
                                    a_lambda, a_subln)
        x = x + g1 * y
        h = rmsnorm(x, norm_ffn[l]) * (1.0 + sc2) + sh2
        x = x + g2 * moe(h, l, w_router, b_router, w_e_gate, w_e_up, w_e_down)
        ak.append(ka); av.append(va); bk.append(kb); bv.append(vb)
    return (rmsnorm(x, norm_final), jnp.stack(ak), jnp.stack(av), jnp.stack(bk), jnp.stack(bv))


def setup_inputs(seed: int = 0) -> dict:
    key = jax.random.key(seed)
    ks = jax.random.split(key, 24)

    def nrm(k, shape, scale):
        return jax.random.normal(k, shape, jnp.float32) * scale

    D = D_MODEL
    return {
        "x_prompt": nrm(ks[0], (BATCH, SEQ, D), 1.0),
        "x_sample": nrm(ks[1], (DEC_BATCH, DEC_SEQ, D), 1.0),
        "cache_a_k": nrm(ks[2], (DEPTH, DEC_BATCH, PAST_LEN, H_A, 2 * HD_A), 1.0),
        "cache_a_v": nrm(ks[3], (DEPTH, DEC_BATCH, PAST_LEN, H_A, 2 * HD_A), 1.0),
        "cache_b_k": nrm(ks[4], (DEPTH, DEC_BATCH, PAST_LEN, H_B, HD_B), 1.0),
        "cache_b_v": nrm(ks[5], (DEPTH, DEC_BATCH, PAST_LEN, H_B, HD_B), 1.0),
        "c_prompt": nrm(ks[6], (BATCH, D), 1.0),
        "c_sample": nrm(ks[7], (DEC_BATCH, D), 1.0),
        "w_in": nrm(ks[8], (DEPTH, D, IN_COLS), D ** -0.5),
        "w_proj_a": nrm(ks[9], (DEPTH, W_A_V, D), W_A_V ** -0.5),
        "w_proj_b": nrm(ks[10], (DEPTH, W_B, D), W_B ** -0.5),
        "w_out": nrm(ks[11], (DEPTH, D, D), D ** -0.5),
        "a_lambda": nrm(ks[12], (DEPTH, 4, HD_A), 0.1),
        "a_subln": 1.0 + nrm(ks[13], (DEPTH, 2 * HD_A), 0.02),
        "w_ada": nrm(ks[14], (DEPTH, D, 6 * D), 0.5 * D ** -0.5),
        "b_ada": nrm(ks[15], (DEPTH, 6 * D), 0.02),
        "norm_mix": 1.0 + nrm(ks[16], (DEPTH, D), 0.02),
        "norm_ffn": 1.0 + nrm(ks[17], (DEPTH, D), 0.02),
        "norm_final": 1.0 + nrm(ks[18], (D,), 0.02),
        "w_router": nrm(ks[19], (D, N_EXPERTS), D ** -0.5),
        "b_router": nrm(ks[20], (N_EXPERTS,), 0.01),
        "w_e_gate": nrm(ks[21], (DEPTH, N_EXPERTS, D, D_EXPERT), D ** -0.5),
        "w_e_up": nrm(ks[22], (DEPTH, N_EXPERTS, D, D_EXPERT), D ** -0.5),
        "w_e_down": nrm(ks[23], (DEPTH, N_EXPERTS, D_EXPERT, D), D_EXPERT ** -0.5),
    }


def reference(x_prompt, x_sample, cache_a_k, cache_a_v, cache_b_k, cache_b_v, c_prompt, c_sample,
              w_in, w_proj_a, w_proj_b, w_out, a_lambda, a_subln, w_ada, b_ada,
              norm_mix, norm_ffn, norm_final, w_router, b_router, w_e_gate, w_e_up, w_e_down):
    pos_p = jnp.arange(x_prompt.shape[1], dtype=jnp.int32)
    pos_s = cache_a_k.shape[2] + jnp.arange(x_sample.shape[1], dtype=jnp.int32)
    y_prompt, new_a_k_prompt, new_a_v_prompt, new_b_k_prompt, new_b_v_prompt = trunk(
        x_prompt, c_prompt, pos_p, None, w_in, w_proj_a, w_proj_b, w_out, a_lambda, a_subln,
        w_ada, b_ada, norm_mix, norm_ffn, norm_final, w_router, b_router,
        w_e_gate, w_e_up, w_e_down)
    y_sample, new_a_k_sample, new_a_v_sample, new_b_k_sample, new_b_v_sample = trunk(
        x_sample, c_sample, pos_s, (cache_a_k, cache_a_v, cache_b_k, cache_b_v),
        w_in, w_proj_a, w_proj_b, w_out, a_lambda, a_subln,
        w_ada, b_ada, norm_mix, norm_ffn, norm_final, w_router, b_router,
        w_e_gate, w_e_up, w_e_down)
    return (y_prompt, y_sample, new_a_k_prompt, new_a_v_prompt, new_b_k_prompt, new_b_v_prompt,
            new_a_k_sample, new_a_v_sample, new_b_k_sample, new_b_v_sample)
```

```python
import functools
import math

import jax
import jax.numpy as jnp
from jax import lax
from jax.experimental import pallas as pl
from jax.experimental.pallas import tpu as pltpu

D_MODEL = 1024
CHUNK = 64
N_HEADS = 4
HEAD_W = 128
HD_A = 64
HD_B = 128
W_BRANCH = N_HEADS * HEAD_W
N_EXPERTS = 16
EXPERTS_PER_GROUP = 4
D_EXPERT = 256
ROPE_THETA = 10000.0
EPS = 1e-6
NEG_INF = -1e30
STICK_DEAD_LOG = -104.0
LANES = 128
VMEM_LIMIT = 48 * 1024 * 1024

F32 = jnp.float32
BF16 = jnp.bfloat16


def _cparams(sem):
    return pltpu.CompilerParams(dimension_semantics=sem, vmem_limit_bytes=VMEM_LIMIT)


def _ada_kernel(c_ref, w_ref, b_ref, o_ref):
    c = c_ref[...]
    a = (c * jax.nn.sigmoid(c)).astype(BF16)
    w = w_ref[0].astype(BF16)
    o_ref[0] = jnp.dot(a, w, preferred_element_type=F32) + b_ref[0]


def _ada_mod(c_all, w_ada, b_ada, tn=1536):
    depth, d, n6 = w_ada.shape
    rows = c_all.shape[0]
    return pl.pallas_call(
        _ada_kernel,
        grid=(depth, n6 // tn),
        in_specs=[
            pl.BlockSpec((rows, d), lambda l, j: (0, 0)),
            pl.BlockSpec((1, d, tn), lambda l, j: (l, 0, j)),
            pl.BlockSpec((1, 1, tn), lambda l, j: (l, 0, j)),
        ],
        out_specs=pl.BlockSpec((1, rows, tn), lambda l, j: (l, 0, j)),
        out_shape=jax.ShapeDtypeStruct((depth, rows, n6), F32),
        compiler_params=_cparams(("arbitrary", "arbitrary")),
        name="ada_mod",
    )(c_all, w_ada, b_ada.reshape(depth, 1, n6))


def _rope128(x, cos, sin_signed, lane):
    fwd = pltpu.roll(x, LANES - HD_A // 2, 1)
    bwd = pltpu.roll(x, HD_A // 2, 1)
    partner = jnp.where((lane & (HD_A - 1)) < HD_A // 2, fwd, bwd)
    return x * cos + partner * sin_signed


def _inproj_kernel(x_ref, g_ref, sc_ref, sh_ref, cos_ref, sin_ref, w_ref,
                   qa_ref, kaf_ref, kab_ref, vaf_ref, vab_ref,
                   qb_ref, kbf_ref, kbb_ref, vbf_ref, vbb_ref, ga_ref, gb_ref):
    bb, tt, d = x_ref.shape
    rows = bb * tt
    x = x_ref[...]
    r = lax.rsqrt(jnp.mean(x * x, axis=-1, keepdims=True) + EPS)
    h = (x * r) * g_ref[...] * (1.0 + sc_ref[...]) + sh_ref[...]
    hb = h.reshape(rows, d).astype(BF16)
    cos = cos_ref[...]
    sin = sin_ref[...]
    if bb > 1:
        cos = jnp.concatenate([cos] * bb, axis=0)
        sin = jnp.concatenate([sin] * bb, axis=0)
    lane = lax.broadcasted_iota(jnp.int32, (rows, LANES), 1)

    def proj(c0, width):
        return jnp.dot(hb, w_ref[:, c0:c0 + width], preferred_element_type=F32)

    def put(ref, val):
        ref[...] = val.reshape(ref.shape).astype(ref.dtype)

    w = W_BRANCH
    qa = proj(0, w)
    ka = proj(w, w)
    qa_rot, ka_rot = [], []
    for hd in range(N_HEADS):
        sl = slice(hd * HEAD_W, (hd + 1) * HEAD_W)
        qa_rot.append(_rope128(qa[:, sl], cos, sin, lane) * (HD_A ** -0.5))
        ka_rot.append(_rope128(ka[:, sl], cos, sin, lane))
    qa = jnp.concatenate(qa_rot, axis=1)
    ka = jnp.concatenate(ka_rot, axis=1)
    put(qa_ref, qa)
    put(kaf_ref, ka)
    put(kab_ref, ka)
    va = proj(2 * w, w)
    put(vaf_ref, va)
    put(vab_ref, va)
    put(qb_ref, proj(3 * w, w) * (HD_B ** -0.5))
    kb = proj(4 * w, w)
    put(kbf_ref, kb)
    put(kbb_ref, kb)
    vb = proj(5 * w, w)
    put(vbf_ref, vb)
    put(vbb_ref, vb)
    put(ga_ref, jax.nn.sigmoid(proj(6 * w, d)))
    put(gb_ref, jax.nn.sigmoid(proj(6 * w + d, d)))


def _inproj(x, gain, sc, sh, cos, sin, w_bf, bb, tt):
    b, t, d = x.shape
    grid = (b // bb, t // tt)
    xmap = lambda i, j: (i, j, 0)
    modspec = pl.BlockSpec((bb, 1, d), lambda i, j: (i, 0, 0))
    tabspec = pl.BlockSpec((tt, LANES), lambda i, j: (j, 0))

    def out(width, dtype):
        return (pl.BlockSpec((bb, tt, width), xmap), jax.ShapeDtypeStruct((b, t, width), dtype))

    outs = [out(W_BRANCH, BF16),
            out(W_BRANCH, F32), out(W_BRANCH, BF16),
            out(W_BRANCH, F32), out(W_BRANCH, BF16),
            out(W_BRANCH, BF16),
            out(W_BRANCH, F32), out(W_BRANCH, BF16),
            out(W_BRANCH, F32), out(W_BRANCH, BF16),
            out(d, BF16), out(d, BF16)]
    return pl.pallas_call(
        _inproj_kernel,
        grid=grid,
        in_specs=[
            pl.BlockSpec((bb, tt, d), xmap),
            pl.BlockSpec((1, 1, d), lambda i, j: (0, 0, 0)),
            modspec, modspec, tabspec, tabspec,
            pl.BlockSpec(w_bf.shape, lambda i, j: (0, 0)),
        ],
        out_specs=[o[0] for o in outs],
        out_shape=[o[1] for o in outs],
        compiler_params=_cparams(("arbitrary", "arbitrary")),
        name="inproj",
    )(x, gain.reshape(1, 1, d), sc, sh, cos, sin, w_bf)


def _diff_kernel(lw_ref, sub_ref, q_ref, kd_ref, vd_ref, kp_ref, vp_ref, o_ref,
                 qz_s, m_s, l_s, acc_s, *, tq, tkp, prev_from_grid, lam_init):
    td = kd_ref.shape[1]
    q = q_ref[0]
    lane = lax.broadcasted_iota(jnp.int32, (tq, HEAD_W), 1)
    zero = jnp.zeros_like(q)
    qz = jnp.concatenate([jnp.where(lane < HD_A, q, zero), jnp.where(lane >= HD_A, q, zero)], axis=0)
    qz_s[...] = qz

    nt = (((1,), (1,)), ((), ()))
    row = lax.broadcasted_iota(jnp.int32, (2 * tq, td), 0)
    row = jnp.where(row >= tq, row - tq, row)
    col = lax.broadcasted_iota(jnp.int32, (2 * tq, td), 1)
    visible = (col >> 6) <= (row >> 6)
    s = lax.dot_general(qz, kd_ref[0].astype(BF16), nt, preferred_element_type=F32)
    s = jnp.where(visible, s, NEG_INF)
    m0 = jnp.max(s, axis=-1, keepdims=True)
    p = jnp.exp(s - m0)
    m_s[...] = m0
    l_s[...] = jnp.sum(p, axis=-1, keepdims=True)
    acc_s[...] = jnp.dot(p.astype(BF16), vd_ref[0].astype(BF16), preferred_element_type=F32)

    if prev_from_grid:
        n_prev = (pl.program_id(2) * tq) // tkp
    else:
        n_prev = kp_ref.shape[1] // tkp

    def body(j, carry):
        start = pl.multiple_of(j * tkp, tkp)
        k = kp_ref[0, pl.ds(start, tkp), :].astype(BF16)
        v = vp_ref[0, pl.ds(start, tkp), :].astype(BF16)
        sj = lax.dot_general(qz_s[...], k, nt, preferred_element_type=F32)
        m_prev = m_s[...]
        m_new = jnp.maximum(m_prev, jnp.max(sj, axis=-1, keepdims=True))
        alpha = jnp.exp(m_prev - m_new)
        pj = jnp.exp(sj - m_new)
        l_s[...] = alpha * l_s[...] + jnp.sum(pj, axis=-1, keepdims=True)
        acc_s[...] = alpha * acc_s[...] + jnp.dot(pj.astype(BF16), v, preferred_element_type=F32)
        m_s[...] = m_new
        return carry

    lax.fori_loop(0, n_prev, body, 0)

    lw = lw_ref[...]
    lam = (jnp.exp(jnp.sum(lw[0:1] * lw[1:2], axis=-1, keepdims=True))
           - jnp.exp(jnp.sum(lw[2:3] * lw[3:4], axis=-1, keepdims=True)) + lam_init)
    o = acc_s[...] / l_s[...]
    o = o[:tq] - lam * o[tq:]
    r = lax.rsqrt(jnp.mean(o * o, axis=-1, keepdims=True) + EPS)
    o_ref[0] = ((o * r) * sub_ref[...] * (1.0 - lam_init)).astype(o_ref.dtype)


def _diff_attn(lw, sub, q, k_new, v_new, k_prev, v_prev, *, tq, tkp, prev_from_grid, lam_init):
    b, t, _ = q.shape
    tp = k_prev.shape[1]
    grid = (b, N_HEADS, t // tq)
    tile = pl.BlockSpec((1, tq, HEAD_W), lambda bi, h, i: (bi, i, h))
    whole = pl.BlockSpec((1, tp, HEAD_W), lambda bi, h, i: (bi, 0, h))
    kern = functools.partial(_diff_kernel, tq=tq, tkp=tkp, prev_from_grid=prev_from_grid, lam_init=lam_init)
    return pl.pallas_call(
        kern,
        grid=grid,
        in_specs=[
            pl.BlockSpec(lw.shape, lambda bi, h, i: (0, 0)),
            pl.BlockSpec((1, HEAD_W), lambda bi, h, i: (0, 0)),
            tile, tile, tile, whole, whole,
        ],
        out_specs=tile,
        out_shape=jax.ShapeDtypeStruct((b, t, W_BRANCH), BF16),
        scratch_shapes=[
            pltpu.VMEM((2 * tq, HEAD_W), BF16),
            pltpu.VMEM((2 * tq, 1), F32),
            pltpu.VMEM((2 * tq, 1), F32),
            pltpu.VMEM((2 * tq, HEAD_W), F32),
        ],
        compiler_params=_cparams(("arbitrary", "arbitrary", "arbitrary")),
        name="diff_attn",
    )(lw, sub, q, k_new, v_new, k_prev, v_prev)


def _neg_softplus(z):
    return -(jnp.maximum(z, 0.0) + jnp.log1p(jnp.exp(-jnp.abs(z))))


def _suffix_sums(x, upper):
    hi = x.astype(BF16)
    lo = (x - hi.astype(F32)).astype(BF16)
    return (jnp.dot(hi, upper, preferred_element_type=F32)
            + jnp.dot(lo, upper, preferred_element_type=F32))


def _upper(n):
    j = lax.broadcasted_iota(jnp.int32, (n, n), 0)
    s = lax.broadcasted_iota(jnp.int32, (n, n), 1)
    return jnp.where(j >= s, 1.0, 0.0).astype(BF16)


def _stick_kernel(q_ref, kd_ref, vd_ref, kp_ref, vp_ref, o_ref, up_s, c_s, acc_s,
                  *, tq, tkp, prev_from_grid):
    td = kd_ref.shape[1]
    q = q_ref[0]
    nt = (((1,), (1,)), ((), ()))
    row = lax.broadcasted_iota(jnp.int32, (tq, td), 0)
    col = lax.broadcasted_iota(jnp.int32, (tq, td), 1)
    earlier = col < row
    z = lax.dot_general(q, kd_ref[0].astype(BF16), nt, preferred_element_type=F32)
    lneg = jnp.where(earlier, _neg_softplus(z), 0.0)
    cs = _suffix_sums(lneg, _upper(td))
    a = jnp.where(earlier, jnp.exp(z + cs), 0.0)
    acc_s[...] = jnp.dot(a.astype(BF16), vd_ref[0].astype(BF16), preferred_element_type=F32)
    c0 = jnp.sum(lneg, axis=-1, keepdims=True)
    c_s[...] = c0
    up_s[...] = _upper(tkp)

    if prev_from_grid:
        n_prev = (pl.program_id(2) * tq) // tkp
    else:
        n_prev = kp_ref.shape[1] // tkp

    def cond(state):
        j, c_max = state
        return jnp.logical_and(j >= 0, c_max > STICK_DEAD_LOG)

    def body(state):
        j, _ = state
        start = pl.multiple_of(j * tkp, tkp)
        k = kp_ref[0, pl.ds(start, tkp), :].astype(BF16)
        v = vp_ref[0, pl.ds(start, tkp), :].astype(BF16)
        zj = lax.dot_general(q_ref[0], k, nt, preferred_element_type=F32)
        ln = _neg_softplus(zj)
        csj = _suffix_sums(ln, up_s[...])
        c = c_s[...]
        aj = jnp.exp(zj + csj + c)
        acc_s[...] += jnp.dot(aj.astype(BF16), v, preferred_element_type=F32)
        c_new = c + jnp.sum(ln, axis=-1, keepdims=True)
        c_s[...] = c_new
        return j - 1, jnp.max(c_new)

    lax.while_loop(cond, body, (jnp.int32(n_prev - 1), jnp.max(c0)))
    o_ref[0] = acc_s[...].astype(o_ref.dtype)


def _stick_attn(q, k_new, v_new, k_prev, v_prev, *, tq, tkp, prev_from_grid):
    b, t, _ = q.shape
    tp = k_prev.shape[1]
    grid = (b, N_HEADS, t // tq)
    tile = pl.BlockSpec((1, tq, HEAD_W), lambda bi, h, i: (bi, i, h))
    whole = pl.BlockSpec((1, tp, HEAD_W), lambda bi, h, i: (bi, 0, h))
    kern = functools.partial(_stick_kernel, tq=tq, tkp=tkp, prev_from_grid=prev_from_grid)
    return pl.pallas_call(
        kern,
        grid=grid,
        in_specs=[tile, tile, tile, whole, whole],
        out_specs=tile,
        out_shape=jax.ShapeDtypeStruct((b, t, W_BRANCH), BF16),
        scratch_shapes=[
            pltpu.VMEM((tkp, tkp), BF16),
            pltpu.VMEM((tq, 1), F32),
            pltpu.VMEM((tq, HEAD_W), F32),
        ],
        compiler_params=_cparams(("arbitrary", "arbitrary", "arbitrary")),
        name="stick_attn",
    )(q, k_new, v_new, k_prev, v_prev)


def _group_partner(x, lane, d, width):
    near = pltpu.roll(x, LANES - d, 1)
    far = pltpu.roll(x, width - d, 1)
    return jnp.where((lane & (width - 1)) + d < width, near, far)


def _router_gates(logits, bias, lane):
    aff = jax.nn.sigmoid(logits)
    sel = aff + bias
    pos = lane & (EXPERTS_PER_GROUP - 1)
    rank = jnp.zeros_like(lane)
    for d in range(1, EXPERTS_PER_GROUP):
        other = _group_partner(sel, lane, d, EXPERTS_PER_GROUP)
        other_pos = (pos + d) & (EXPERTS_PER_GROUP - 1)
        ahead = jnp.logical_or(other > sel, jnp.logical_and(other == sel, other_pos < pos))
        rank = rank + jnp.where(ahead, 1, 0)
    in_top2 = rank < 2
    best2 = jnp.where(in_top2, sel, 0.0)
    score = best2
    for d in range(1, EXPERTS_PER_GROUP):
        score = score + _group_partner(best2, lane, d, EXPERTS_PER_GROUP)
    grp = (lane >> 2) & (N_EXPERTS // EXPERTS_PER_GROUP - 1)
    beaten = jnp.zeros_like(lane)
    for d in range(1, N_EXPERTS // EXPERTS_PER_GROUP):
        other = _group_partner(score, lane, d * EXPERTS_PER_GROUP, N_EXPERTS)
        other_grp = (grp + d) & (N_EXPERTS // EXPERTS_PER_GROUP - 1)
        ahead = jnp.logical_or(other > score, jnp.logical_and(other == score, other_grp < grp))
        beaten = beaten + jnp.where(ahead, 1, 0)
    chosen = jnp.logical_and(jnp.logical_and(in_top2, beaten == 0), lane < N_EXPERTS)
    w = jnp.where(chosen, aff, 0.0)
    return w / jnp.sum(w, axis=-1, keepdims=True)


def _mixout_kernel(x_ref, oa_ref, ob_ref, ga_ref, gb_ref, g1_ref, sc_ref, sh_ref, gain_ref,
                   wa_ref, wb_ref, wo_ref, wr_ref, br_ref, x1_ref, h2_ref, gates_ref):
    bb, tt, d = x_ref.shape
    rows = bb * tt
    ya = jnp.dot(oa_ref[...].reshape(rows, W_BRANCH), wa_ref[...], preferred_element_type=F32)
    yb = jnp.dot(ob_ref[...].reshape(rows, W_BRANCH), wb_ref[...], preferred_element_type=F32)
    y = (ga_ref[...].reshape(rows, d).astype(F32) * ya + gb_ref[...].reshape(rows, d).astype(F32) * yb)
    mix = jnp.dot(y.astype(BF16), wo_ref[...], preferred_element_type=F32)
    x1 = x_ref[...] + g1_ref[...] * mix.reshape(bb, tt, d)
    x1_ref[...] = x1
    r = lax.rsqrt(jnp.mean(x1 * x1, axis=-1, keepdims=True) + EPS)
    h2 = ((x1 * r) * gain_ref[...] * (1.0 + sc_ref[...]) + sh_ref[...]).reshape(rows, d).astype(BF16)
    h2_ref[...] = h2.reshape(bb, tt, d)
    logits = jnp.dot(h2, wr_ref[...], preferred_element_type=F32)
    lane = lax.broadcasted_iota(jnp.int32, (rows, LANES), 1)
    gates_ref[...] = _router_gates(logits, br_ref[...], lane).reshape(bb, tt, LANES)


def _mixout(x, oa, ob, ga, gb, g1, sc2, sh2, gain, wa, wb, wo, wr, br, bb, tt):
    b, t, d = x.shape
    grid = (b // bb, t // tt)
    xmap = lambda i, j: (i, j, 0)
    modspec = pl.BlockSpec((bb, 1, d), lambda i, j: (i, 0, 0))

    def full(a):
        return pl.BlockSpec(a.shape, lambda i, j: (0,) * a.ndim)

    return pl.pallas_call(
        _mixout_kernel,
        grid=grid,
        in_specs=[
            pl.BlockSpec((bb, tt, d), xmap),
            pl.BlockSpec((bb, tt, W_BRANCH), xmap), pl.BlockSpec((bb, tt, W_BRANCH), xmap),
            pl.BlockSpec((bb, tt, d), xmap), pl.BlockSpec((bb, tt, d), xmap),
            modspec, modspec, modspec,
            pl.BlockSpec((1, 1, d), lambda i, j: (0, 0, 0)),
            full(wa), full(wb), full(wo), full(wr), full(br),
        ],
        out_specs=[pl.BlockSpec((bb, tt, d), xmap), pl.BlockSpec((bb, tt, d), xmap),
                   pl.BlockSpec((bb, tt, LANES), xmap)],
        out_shape=[jax.ShapeDtypeStruct((b, t, d), F32), jax.ShapeDtypeStruct((b, t, d), BF16),
                   jax.ShapeDtypeStruct((b, t, LANES), F32)],
        compiler_params=_cparams(("arbitrary", "arbitrary")),
        name="mixout",
    )(x, oa, ob, ga, gb, g1, sc2, sh2, gain.reshape(1, 1, d), wa, wb, wo, wr, br)


def _moe_kernel(h_ref, gates_ref, x1_ref, g2_ref, gain_ref, wg_ref, wu_ref, wd_ref, o_ref, acc_s,
                *, final_norm):
    bb, tt, d = h_ref.shape
    rows = bb * tt
    e = pl.program_id(2)
    h = h_ref[...].reshape(rows, d)
    g = jnp.dot(h, wg_ref[0], preferred_element_type=F32)
    u = jnp.dot(h, wu_ref[0], preferred_element_type=F32)
    gates = gates_ref[...].reshape(rows, LANES)
    lane = lax.broadcasted_iota(jnp.int32, (rows, LANES), 1)
    gate_e = jnp.sum(jnp.where(lane == e, gates, 0.0), axis=-1, keepdims=True)
    he = ((g * jax.nn.sigmoid(g)) * u * gate_e).astype(BF16)
    part = jnp.dot(he, wd_ref[0], preferred_element_type=F32)

    @pl.when(e == 0)
    def _():
        acc_s[...] = part

    @pl.when(e > 0)
    def _():
        acc_s[...] += part

    @pl.when(e == N_EXPERTS - 1)
    def _():
        x2 = x1_ref[...] + g2_ref[...] * acc_s[...].reshape(bb, tt, d)
        if final_norm:
            r = lax.rsqrt(jnp.mean(x2 * x2, axis=-1, keepdims=True) + EPS)
            x2 = (x2 * r) * gain_ref[...]
        o_ref[...] = x2


def _moe(h2, gates, x1, g2, gain, wg, wu, wd, bb, tt, final_norm):
    b, t, d = x1.shape
    grid = (b // bb, t // tt, N_EXPERTS)
    xmap = lambda i, j, e: (i, j, 0)
    return pl.pallas_call(
        functools.partial(_moe_kernel, final_norm=final_norm),
        grid=grid,
        in_specs=[
            pl.BlockSpec((bb, tt, d), xmap),
            pl.BlockSpec((bb, tt, LANES), xmap),
            pl.BlockSpec((bb, tt, d), xmap),
            pl.BlockSpec((bb, 1, d), lambda i, j, e: (i, 0, 0)),
            pl.BlockSpec((1, 1, d), lambda i, j, e: (0, 0, 0)),
            pl.BlockSpec((1, d, D_EXPERT), lambda i, j, e: (e, 0, 0)),
            pl.BlockSpec((1, d, D_EXPERT), lambda i, j, e: (e, 0, 0)),
            pl.BlockSpec((1, D_EXPERT, d), lambda i, j, e: (e, 0, 0)),
        ],
        out_specs=pl.BlockSpec((bb, tt, d), xmap),
        out_shape=jax.ShapeDtypeStruct((b, t, d), F32),
        scratch_shapes=[pltpu.VMEM((bb * tt, d), F32)],
        compiler_params=_cparams(("arbitrary", "arbitrary", "arbitrary")),
        name="moe",
    )(h2, gates, x1, g2, gain.reshape(1, 1, d), wg, wu, wd)


def _rope_tables(pos):
    half = HD_A // 2
    inv = ROPE_THETA ** (-jnp.arange(half, dtype=F32) / half)
    ang = pos.astype(F32)[:, None] * inv[None, :]
    cos, sin = jnp.cos(ang), jnp.sin(ang)
    reps = LANES // HD_A
    cos_t = jnp.tile(jnp.concatenate([cos, cos], axis=1), (1, reps))
    sin_t = jnp.tile(jnp.concatenate([-sin, sin], axis=1), (1, reps))
    return cos_t, sin_t


def _trunk(x, mod, pos, past, p, *, row_block, moe_rows, attn_q, attn_prev):
    b, t, d = x.shape
    bb, tt = row_block
    depth = p["w_in"].shape[0]
    cos, sin = _rope_tables(pos)
    new_caches = [[], [], [], []]
    for l in range(depth):
        sh1, sc1, g1, sh2, sc2, g2 = [mod[l, :, i][:, None, :] for i in range(6)]
        (qa, kaf, kab, vaf, vab, qb, kbf, kbb, vbf, vbb, ga, gb) = _inproj(
            x, p["norm_mix"][l], sc1, sh1, cos, sin, p["w_in"][l], bb, tt)
        lam_init = 0.8 - 0.6 * math.exp(-0.3 * l)
        if past is None:
            prev = (kab, vab, kbb, vbb)
        else:
            prev = tuple(c[l] for c in past)
        oa = _diff_attn(p["a_lambda"][l], p["a_subln"][l][None, :], qa, kab, vab, prev[0], prev[1],
                        tq=attn_q, tkp=attn_prev, prev_from_grid=past is None, lam_init=lam_init)
        ob = _stick_attn(qb, kbb, vbb, prev[2], prev[3],
                         tq=attn_q, tkp=attn_prev, prev_from_grid=past is None)
        x1, h2, gates = _mixout(x, oa, ob, ga, gb, g1, sc2, sh2, p["norm_ffn"][l],
                                p["w_proj_a"][l], p["w_proj_b"][l], p["w_out"][l],
                                p["w_router"], p["b_router"], bb, tt)
        x = _moe(h2, gates, x1, g2, p["norm_final"], p["w_e_gate"][l], p["w_e_up"][l],
                 p["w_e_down"][l], bb, moe_rows, final_norm=(l == depth - 1))
        for lst, val in zip(new_caches, (kaf, vaf, kbf, vbf)):
            lst.append(val.reshape(b, t, N_HEADS, HEAD_W))
    return (x,) + tuple(jnp.stack(lst) for lst in new_caches)


def kernel(x_prompt, x_sample, cache_a_k, cache_a_v, cache_b_k, cache_b_v, c_prompt, c_sample,
           w_in, w_proj_a, w_proj_b, w_out, a_lambda, a_subln, w_ada, b_ada,
           norm_mix, norm_ffn, norm_final, w_router, b_router, w_e_gate, w_e_up, w_e_down):
    d = x_prompt.shape[-1]
    bp, tp = x_prompt.shape[:2]
    bs, ts = x_sample.shape[:2]
    depth = w_in.shape[0]
    past_len = cache_a_k.shape[2]
    assert d == D_MODEL and past_len % CHUNK == 0

    wr = jnp.zeros((d, LANES), BF16).at[:, :N_EXPERTS].set(w_router.astype(BF16))
    br = jnp.zeros((1, LANES), F32).at[0, :N_EXPERTS].set(b_router.astype(F32))
    p = dict(
        w_in=w_in.astype(BF16), w_proj_a=w_proj_a.astype(BF16), w_proj_b=w_proj_b.astype(BF16),
        w_out=w_out.astype(BF16), a_lambda=a_lambda, a_subln=a_subln,
        norm_mix=norm_mix, norm_ffn=norm_ffn, norm_final=norm_final,
        w_router=wr, b_router=br,
        w_e_gate=w_e_gate.astype(BF16), w_e_up=w_e_up.astype(BF16), w_e_down=w_e_down.astype(BF16),
    )

    n_c = bp + bs
    rows = -(-n_c // 8) * 8
    c_all = jnp.zeros((rows, d), F32).at[:bp].set(c_prompt).at[bp:n_c].set(c_sample)
    mod = _ada_mod(c_all, w_ada, b_ada).reshape(depth, rows, 6, d)

    pos_p = jnp.arange(tp, dtype=jnp.int32)
    pos_s = past_len + jnp.arange(ts, dtype=jnp.int32)
    out_p = _trunk(x_prompt, mod[:, :bp], pos_p, None, p,
                   row_block=(1, min(tp, 512)), moe_rows=min(tp, 1024),
                   attn_q=min(tp, 256), attn_prev=min(tp, 256))
    past = tuple(c.reshape(depth, bs, past_len, W_BRANCH)
                 for c in (cache_a_k, cache_a_v, cache_b_k, cache_b_v))
    out_s = _trunk(x_sample, mod[:, bp:n_c], pos_s, past, p,
                   row_block=(bs, ts), moe_rows=ts, attn_q=ts, attn_prev=min(past_len, 512))
    return (out_p[0], out_s[0]) + out_p[1:] + out_s[1:]
```

```python
import functools
import math

import jax
import jax.numpy as jnp
from jax import lax
from jax.experimental import pallas as pl
from jax.experimental.pallas import tpu as pltpu

D_MODEL = 1024
CHUNK = 64
N_HEADS = 4
HEAD_W = 128
HD_A = 64
HD_B = 128
W_BRANCH = N_HEADS * HEAD_W
N_EXPERTS = 16
EXPERTS_PER_GROUP = 4
N_GROUPS = N_EXPERTS // EXPERTS_PER_GROUP
D_EXPERT = 256
ROPE_THETA = 10000.0
EPS = 1e-6
NEG_INF = -1e30
STICK_DEAD_LOG = -104.0
LOG2_E = 1.4426950408889634
LANES = 128
VMEM_LIMIT = 56 * 1024 * 1024

F32 = jnp.float32
BF16 = jnp.bfloat16
NT_DIMS = (((1,), (1,)), ((), ()))


def _log2(n):
    assert n > 0 and n & (n - 1) == 0, n
    return n.bit_length() - 1


def _mm(a, b, dims=None):
    assert a.dtype == b.dtype, (a.dtype, b.dtype)
    prec = lax.Precision.HIGHEST if a.dtype == F32 else None
    if dims is None:
        return jnp.dot(a, b, preferred_element_type=F32, precision=prec)
    return lax.dot_general(a, b, dims, preferred_element_type=F32, precision=prec)


def _cparams(sem):
    return pltpu.CompilerParams(dimension_semantics=sem, vmem_limit_bytes=VMEM_LIMIT)


def _ada_kernel(c_ref, w_ref, b_ref, o_ref):
    c = c_ref[...]
    a = c * jax.nn.sigmoid(c)

    @pl.when(pl.program_id(0) == 0)
    def _():
        o_ref[0] = _mm(a, w_ref[0]) + b_ref[0]

    @pl.when(pl.program_id(0) > 0)
    def _():
        o_ref[0] = _mm(a.astype(BF16), w_ref[0].astype(BF16)) + b_ref[0]


def _ada_mod(c_all, w_ada, b_ada, tn=1536):
    depth, d, n6 = w_ada.shape
    rows = c_all.shape[0]
    return pl.pallas_call(
        _ada_kernel,
        grid=(depth, n6 // tn),
        in_specs=[
            pl.BlockSpec((rows, d), lambda l, j: (0, 0)),
            pl.BlockSpec((1, d, tn), lambda l, j: (l, 0, j)),
            pl.BlockSpec((1, 1, tn), lambda l, j: (l, 0, j)),
        ],
        out_specs=pl.BlockSpec((1, rows, tn), lambda l, j: (l, 0, j)),
        out_shape=jax.ShapeDtypeStruct((depth, rows, n6), F32),
        compiler_params=_cparams(("arbitrary", "arbitrary")),
        name="ada_mod",
    )(c_all, w_ada, b_ada.reshape(depth, 1, n6))


def _rope128(x, cos, sin_signed, lane):
    fwd = pltpu.roll(x, LANES - HD_A // 2, 1)
    bwd = pltpu.roll(x, HD_A // 2, 1)
    partner = jnp.where((lane & (HD_A - 1)) < HD_A // 2, fwd, bwd)
    return x * cos + partner * sin_signed


def _inproj_kernel(*refs, n_alias, feature_major):
    (x_ref, g_ref, sc_ref, sh_ref, cos_ref, sin_ref, w_ref) = refs[:7]
    outs = refs[7 + n_alias:]
    (ka_c, va_c, kb_c, vb_c, qa_ref, qb_ref, kab_ref, vab_ref, kbb_ref, vbb_ref, ga_ref, gb_ref) = outs
    bb, tt, d = x_ref.shape
    rows = bb * tt
    x = x_ref[...]
    r = lax.rsqrt(jnp.mean(x * x, axis=-1, keepdims=True) + EPS)
    h = (x * r) * g_ref[...] * (1.0 + sc_ref[...]) + sh_ref[...]
    hb = h.reshape(rows, d).astype(w_ref.dtype)
    cos = cos_ref[...]
    sin = sin_ref[...]
    if bb > 1:
        cos = jnp.concatenate([cos] * bb, axis=0)
        sin = jnp.concatenate([sin] * bb, axis=0)
    lane = lax.broadcasted_iota(jnp.int32, (rows, LANES), 1)

    def proj(c0, width):
        return _mm(hb, w_ref[:, c0:c0 + width])

    def put(ref, val):
        ref[...] = val.reshape(ref.shape).astype(ref.dtype)

    def put_cache(ref, heads):
        for b in range(bb):
            for hd in range(N_HEADS):
                ref[0, b, pl.ds(hd, tt, stride=N_HEADS), :] = heads[hd][b * tt:(b + 1) * tt]

    def put_feature_major(ref, heads):
        for hd in range(N_HEADS):
            ref[0, hd, 0] = heads[hd].T.astype(ref.dtype)

    def split(val):
        return [val[:, hd * HEAD_W:(hd + 1) * HEAD_W] for hd in range(N_HEADS)]

    w = W_BRANCH
    qa = [_rope128(v, cos, sin, lane) * (HD_A ** -0.5 * LOG2_E) for v in split(proj(0, w))]
    ka = [_rope128(v, cos, sin, lane) for v in split(proj(w, w))]
    va = split(proj(2 * w, w))
    put_cache(ka_c, ka)
    put_cache(va_c, va)
    put(kab_ref, jnp.concatenate(ka, axis=1))
    if feature_major:
        put_feature_major(qa_ref, qa)
        put_feature_major(vab_ref, va)
    else:
        put(qa_ref, jnp.concatenate(qa, axis=1))
        put(vab_ref, jnp.concatenate(va, axis=1))
    put(qb_ref, proj(3 * w, w) * (HD_B ** -0.5))
    kb = proj(4 * w, w)
    put_cache(kb_c, split(kb))
    put(kbb_ref, kb)
    vb = proj(5 * w, w)
    put_cache(vb_c, split(vb))
    put(vbb_ref, vb)
    put(ga_ref, jax.nn.sigmoid(proj(6 * w, d)))
    put(gb_ref, jax.nn.sigmoid(proj(6 * w + d, d)))


def _inproj(x, gain, sc, sh, cos, sin, w, bb, tt, layer, depth, caches, feature_major):
    b, t, d = x.shape
    grid = (b // bb, t // tt)
    xmap = lambda i, j: (i, j, 0)
    modspec = pl.BlockSpec((bb, 1, d), lambda i, j: (i, 0, 0))
    tabspec = pl.BlockSpec((tt, LANES), lambda i, j: (j, 0))
    n_alias = 0 if caches is None else 4
    act = w.dtype

    def out(width, dtype):
        return (pl.BlockSpec((bb, tt, width), xmap), jax.ShapeDtypeStruct((b, t, width), dtype))

    def out_t():
        return (pl.BlockSpec((1, N_HEADS, 1, HEAD_W, tt), lambda i, j: (i, 0, j, 0, 0)),
                jax.ShapeDtypeStruct((b, N_HEADS, t // tt, HEAD_W, tt), BF16))

    cache = (pl.BlockSpec((1, bb, tt * N_HEADS, HEAD_W), lambda i, j: (layer, i, j, 0)),
             jax.ShapeDtypeStruct((depth, b, t * N_HEADS, HEAD_W), F32))
    outs = [cache, cache, cache, cache,
            out_t() if feature_major else out(W_BRANCH, act),
            out(W_BRANCH, act),
            out(W_BRANCH, BF16),
            out_t() if feature_major else out(W_BRANCH, BF16),
            out(W_BRANCH, BF16), out(W_BRANCH, BF16),
            out(d, act), out(d, act)]
    in_specs = [
        pl.BlockSpec((bb, tt, d), xmap),
        pl.BlockSpec((1, 1, d), lambda i, j: (0, 0, 0)),
        modspec, modspec, tabspec, tabspec,
        pl.BlockSpec(w.shape, lambda i, j: (0, 0)),
    ] + [pl.BlockSpec(memory_space=pl.ANY)] * n_alias
    args = (x, gain.reshape(1, 1, d), sc, sh, cos, sin, w) + (() if caches is None else tuple(caches))
    return pl.pallas_call(
        functools.partial(_inproj_kernel, n_alias=n_alias, feature_major=feature_major),
        grid=grid,
        in_specs=in_specs,
        out_specs=[o[0] for o in outs],
        out_shape=[o[1] for o in outs],
        input_output_aliases={7 + k: k for k in range(n_alias)},
        compiler_params=_cparams(("arbitrary", "arbitrary")),
        name="inproj",
    )(*args)


def _diff_lambda(lw_ref, lam_init):
    lw = lw_ref[...]
    return (jnp.exp(jnp.sum(lw[0:1] * lw[1:2], axis=-1, keepdims=True))
            - jnp.exp(jnp.sum(lw[2:3] * lw[3:4], axis=-1, keepdims=True)) + lam_init)


def _diff_prompt_kernel(lw_ref, sub_ref, qt_ref, k_ref, vt_ref, o_ref, qz_s, s0_s, s1_s, m_s, l_s, acc_s,
                        *, tq, lam_init):
    i = pl.program_id(2)
    cols = 2 * tq
    qt = qt_ref[0, 0, 0]
    feat = lax.broadcasted_iota(jnp.int32, (HEAD_W, tq), 0)
    zero = jnp.zeros_like(qt)
    qz_s[...] = jnp.concatenate([jnp.where(feat < HD_A, qt, zero), jnp.where(feat >= HD_A, qt, zero)], axis=1)
    m_s[...] = jnp.full((1, cols), NEG_INF, F32)
    l_s[...] = jnp.zeros((1, cols), F32)
    acc_s[...] = jnp.zeros((HEAD_W, cols), F32)

    def scores(j, s_ref):
        start = pl.multiple_of(j * tq, tq)
        s_ref[...] = jnp.dot(k_ref[0, pl.ds(start, tq), :], qz_s[...], preferred_element_type=F32)

    def absorb(j, s_ref, masked):
        s = s_ref[...]
        if masked:
            key = lax.broadcasted_iota(jnp.int32, (tq, cols), 0)
            qry = lax.broadcasted_iota(jnp.int32, (tq, cols), 1)
            qry = jnp.where(qry >= tq, qry - tq, qry)
            s = jnp.where((key >> 6) <= (qry >> 6), s, NEG_INF)
        m_prev = m_s[...]
        m_new = jnp.maximum(m_prev, jnp.max(s, axis=0, keepdims=True))
        alpha = jnp.exp2(m_prev - m_new)
        p = jnp.exp2(s - m_new)
        l_s[...] = alpha * l_s[...] + jnp.sum(p, axis=0, keepdims=True)
        acc_s[...] = alpha * acc_s[...] + jnp.dot(vt_ref[0, 0, j], p.astype(BF16), preferred_element_type=F32)
        m_s[...] = m_new

    scores(0, s0_s)

    def pair(pi, carry):
        j = 2 * pi
        scores(j + 1, s1_s)
        absorb(j, s0_s, False)
        scores(j + 2, s0_s)
        absorb(j + 1, s1_s, False)
        return carry

    lax.fori_loop(0, i // 2, pair, 0)

    @pl.when((i & 1) == 0)
    def _():
        absorb(i, s0_s, True)

    @pl.when((i & 1) == 1)
    def _():
        scores(i, s1_s)
        absorb(i - 1, s0_s, False)
        absorb(i, s1_s, True)

    lam = _diff_lambda(lw_ref, lam_init)
    o = acc_s[...] / l_s[...]
    o = o[:, :tq] - lam * o[:, tq:]
    r = lax.rsqrt(jnp.mean(o * o, axis=0, keepdims=True) + EPS)
    o = (o * r) * sub_ref[...] * (1.0 - lam_init)
    o_ref[0] = o.T.astype(o_ref.dtype)


def _diff_prompt(lw, sub_col, qt, k, vt, *, tq, lam_init):
    b, t, _ = k.shape
    nt = t // tq
    grid = (b, N_HEADS, nt)
    return pl.pallas_call(
        functools.partial(_diff_prompt_kernel, tq=tq, lam_init=lam_init),
        grid=grid,
        in_specs=[
            pl.BlockSpec(lw.shape, lambda bi, h, i: (0, 0)),
            pl.BlockSpec((HEAD_W, 1), lambda bi, h, i: (0, 0)),
            pl.BlockSpec((1, 1, 1, HEAD_W, tq), lambda bi, h, i: (bi, h, i, 0, 0)),
            pl.BlockSpec((1, t, HEAD_W), lambda bi, h, i: (bi, 0, h)),
            pl.BlockSpec((1, 1, nt, HEAD_W, tq), lambda bi, h, i: (bi, h, 0, 0, 0)),
        ],
        out_specs=pl.BlockSpec((1, tq, HEAD_W), lambda bi, h, i: (bi, i, h)),
        out_shape=jax.ShapeDtypeStruct((b, t, W_BRANCH), BF16),
        scratch_shapes=[
            pltpu.VMEM((HEAD_W, 2 * tq), BF16),
            pltpu.VMEM((tq, 2 * tq), F32),
            pltpu.VMEM((tq, 2 * tq), F32),
            pltpu.VMEM((1, 2 * tq), F32),
            pltpu.VMEM((1, 2 * tq), F32),
            pltpu.VMEM((HEAD_W, 2 * tq), F32),
        ],
        compiler_params=_cparams(("arbitrary", "arbitrary", "arbitrary")),
        name="diff_prompt",
    )(lw, sub_col, qt, k, vt)


def _diff_sample_kernel(lw_ref, sub_ref, q_ref, kn_ref, vn_ref, kp_ref, vp_ref, o_ref,
                        qz_s, m_s, l_s, acc_s, *, tq, lam_init):
    kt = pl.program_id(1)
    rows = N_HEADS * 2 * tq

    def fold(k, v, new):
        n = k.shape[0]
        s = _mm(qz_s[...], k.astype(qz_s.dtype), NT_DIMS)
        row = lax.broadcasted_iota(jnp.int32, (rows, n), 0)
        col = lax.broadcasted_iota(jnp.int32, (rows, n), 1)
        keep = (row >> _log2(2 * tq)) == (col & (N_HEADS - 1))
        if new:
            keep = jnp.logical_and(keep, ((col >> 2) >> 6) <= ((row & (tq - 1)) >> 6))
        s = jnp.where(keep, s, NEG_INF)
        m_prev = m_s[...]
        m_new = jnp.maximum(m_prev, jnp.max(s, axis=-1, keepdims=True))
        alpha = jnp.exp2(m_prev - m_new)
        p = jnp.exp2(s - jnp.concatenate([m_new] * (n // LANES), axis=1))
        l_s[...] = alpha * l_s[...] + jnp.sum(p, axis=-1, keepdims=True)
        acc_s[...] = alpha * acc_s[...] + _mm(p.astype(qz_s.dtype), v.astype(qz_s.dtype))
        m_s[...] = m_new

    @pl.when(kt == 0)
    def _():
        q = q_ref[0]
        lane = lax.broadcasted_iota(jnp.int32, (tq, HEAD_W), 1)
        parts = []
        for hd in range(N_HEADS):
            qh = q[:, hd * HEAD_W:(hd + 1) * HEAD_W]
            zero = jnp.zeros_like(qh)
            parts += [jnp.where(lane < HD_A, qh, zero), jnp.where(lane >= HD_A, qh, zero)]
        qz_s[...] = jnp.concatenate(parts, axis=0)
        m_s[...] = jnp.full((rows, LANES), NEG_INF, F32)
        l_s[...] = jnp.zeros((rows, LANES), F32)
        acc_s[...] = jnp.zeros((rows, HEAD_W), F32)
        fold(kn_ref[0, 0], vn_ref[0, 0], True)

    fold(kp_ref[0, 0], vp_ref[0, 0], False)

    @pl.when(kt == pl.num_programs(1) - 1)
    def _():
        lam = _diff_lambda(lw_ref, lam_init)
        o = acc_s[...] / l_s[...]
        outs = []
        for hd in range(N_HEADS):
            base = hd * 2 * tq
            oh = o[base:base + tq] - lam * o[base + tq:base + 2 * tq]
            r = lax.rsqrt(jnp.mean(oh * oh, axis=-1, keepdims=True) + EPS)
            outs.append((oh * r) * sub_ref[...] * (1.0 - lam_init))
        o_ref[0] = jnp.concatenate(outs, axis=1).astype(o_ref.dtype)


def _diff_sample(lw, sub_row, q, k_new, v_new, k_past, v_past, layer, *, key_rows, lam_init):
    b, tq, _ = q.shape
    past_rows = k_past.shape[2]
    grid = (b, past_rows // key_rows)
    new_spec = pl.BlockSpec((1, 1, tq * N_HEADS, HEAD_W), lambda bi, kt: (layer, bi, 0, 0))
    past_spec = pl.BlockSpec((1, 1, key_rows, HEAD_W), lambda bi, kt: (layer, bi, kt, 0))
    rows = N_HEADS * 2 * tq
    return pl.pallas_call(
        functools.partial(_diff_sample_kernel, tq=tq, lam_init=lam_init),
        grid=grid,
        in_specs=[
            pl.BlockSpec(lw.shape, lambda bi, kt: (0, 0)),
            pl.BlockSpec((1, HEAD_W), lambda bi, kt: (0, 0)),
            pl.BlockSpec((1, tq, W_BRANCH), lambda bi, kt: (bi, 0, 0)),
            new_spec, new_spec, past_spec, past_spec,
        ],
        out_specs=pl.BlockSpec((1, tq, W_BRANCH), lambda bi, kt: (bi, 0, 0)),
        out_shape=jax.ShapeDtypeStruct((b, tq, W_BRANCH), q.dtype),
        scratch_shapes=[
            pltpu.VMEM((rows, HEAD_W), q.dtype),
            pltpu.VMEM((rows, LANES), F32),
            pltpu.VMEM((rows, LANES), F32),
            pltpu.VMEM((rows, HEAD_W), F32),
        ],
        compiler_params=_cparams(("arbitrary", "arbitrary")),
        name="diff_sample",
    )(lw, sub_row, q, k_new, v_new, k_past, v_past)


def _neg_softplus(z):
    return -(jnp.maximum(z, 0.0) + jnp.log1p(jnp.exp(-jnp.abs(z))))


def _suffix_sums(x, upper):
    if upper.dtype == F32:
        return _mm(x, upper)
    hi = x.astype(BF16)
    lo = (x - hi.astype(F32)).astype(BF16)
    return (jnp.dot(hi, upper, preferred_element_type=F32)
            + jnp.dot(lo, upper, preferred_element_type=F32))


def _upper(n, dtype=BF16):
    j = lax.broadcasted_iota(jnp.int32, (n, n), 0)
    s = lax.broadcasted_iota(jnp.int32, (n, n), 1)
    return jnp.where(j >= s, 1.0, 0.0).astype(dtype)


def _stick_prompt_kernel(q_ref, kd_ref, vd_ref, kp_ref, vp_ref, o_ref, up_s, c_s, acc_s, *, tq):
    row = lax.broadcasted_iota(jnp.int32, (tq, tq), 0)
    col = lax.broadcasted_iota(jnp.int32, (tq, tq), 1)
    earlier = col < row
    q = q_ref[0]
    up_s[...] = _upper(tq)
    z = lax.dot_general(q, kd_ref[0], NT_DIMS, preferred_element_type=F32)
    lneg = jnp.where(earlier, _neg_softplus(z), 0.0)
    cs = _suffix_sums(lneg, up_s[...])
    a = jnp.where(earlier, jnp.exp(z + cs), 0.0)
    acc_s[...] = jnp.dot(a.astype(BF16), vd_ref[0], preferred_element_type=F32)
    c0 = jnp.sum(lneg, axis=-1, keepdims=True)
    c_s[...] = c0

    def cond(state):
        j, c_max = state
        return jnp.logical_and(j >= 0, c_max > STICK_DEAD_LOG)

    def body(state):
        j, _ = state
        start = pl.multiple_of(j * tq, tq)
        k = kp_ref[0, pl.ds(start, tq), :]
        v = vp_ref[0, pl.ds(start, tq), :]
        zj = lax.dot_general(q_ref[0], k, NT_DIMS, preferred_element_type=F32)
        ln = _neg_softplus(zj)
        csj = _suffix_sums(ln, up_s[...])
        c = c_s[...]
        aj = jnp.exp(zj + csj + c)
        acc_s[...] += jnp.dot(aj.astype(BF16), v, preferred_element_type=F32)
        c_new = c + jnp.sum(ln, axis=-1, keepdims=True)
        c_s[...] = c_new
        return j - 1, jnp.max(c_new)

    lax.while_loop(cond, body, (pl.program_id(2) - 1, jnp.max(c0)))
    o_ref[0] = acc_s[...].astype(o_ref.dtype)


def _stick_prompt(q, k, v, *, tq):
    b, t, _ = q.shape
    grid = (b, N_HEADS, t // tq)
    tile = pl.BlockSpec((1, tq, HEAD_W), lambda bi, h, i: (bi, i, h))
    whole = pl.BlockSpec((1, t, HEAD_W), lambda bi, h, i: (bi, 0, h))
    return pl.pallas_call(
        functools.partial(_stick_prompt_kernel, tq=tq),
        grid=grid,
        in_specs=[tile, tile, tile, whole, whole],
        out_specs=tile,
        out_shape=jax.ShapeDtypeStruct((b, t, W_BRANCH), BF16),
        scratch_shapes=[
            pltpu.VMEM((tq, tq), BF16),
            pltpu.VMEM((tq, 1), F32),
            pltpu.VMEM((tq, HEAD_W), F32),
        ],
        compiler_params=_cparams(("arbitrary", "arbitrary", "arbitrary")),
        name="stick_prompt",
    )(q, k, v, k, v)


def _stick_sample_kernel(q_ref, kn_ref, vn_ref, kp_ref, vp_ref, o_ref, q_s, up_s, c_s, acc_s, live_s,
                         *, tq, sub_rows):
    kt = pl.program_id(1)
    rows = N_HEADS * tq

    def fold(k, v, new):
        n = k.shape[0]
        z = _mm(q_s[...], k.astype(q_s.dtype), NT_DIMS)
        row = lax.broadcasted_iota(jnp.int32, (rows, n), 0)
        col = lax.broadcasted_iota(jnp.int32, (rows, n), 1)
        keep = (row >> _log2(tq)) == (col & (N_HEADS - 1))
        if new:
            keep = jnp.logical_and(keep, (col >> 2) < (row & (tq - 1)))
        ln = jnp.where(keep, _neg_softplus(z), 0.0)
        cs = _suffix_sums(ln, _upper(n, up_s.dtype) if new else up_s[...])
        c = c_s[...]
        a = jnp.where(keep, jnp.exp(z + cs + c), 0.0)
        acc_s[...] += _mm(a.astype(q_s.dtype), v.astype(q_s.dtype))
        c_new = c + jnp.sum(ln, axis=-1, keepdims=True)
        c_s[...] = c_new
        live_s[0] = (jnp.max(c_new) > STICK_DEAD_LOG).astype(jnp.int32)

    @pl.when(kt == 0)
    def _():
        q = q_ref[0]
        q_s[...] = jnp.concatenate([q[:, hd * HEAD_W:(hd + 1) * HEAD_W] for hd in range(N_HEADS)], axis=0)
        up_s[...] = _upper(sub_rows, up_s.dtype)
        c_s[...] = jnp.zeros((rows, 1), F32)
        acc_s[...] = jnp.zeros((rows, HEAD_W), F32)
        fold(kn_ref[0, 0], vn_ref[0, 0], True)

    n_sub = kp_ref.shape[2] // sub_rows
    for sb in reversed(range(n_sub)):
        @pl.when(live_s[0] > 0)
        def _(sb=sb):
            sl = slice(sb * sub_rows, (sb + 1) * sub_rows)
            fold(kp_ref[0, 0, sl, :], vp_ref[0, 0, sl, :], False)

    @pl.when(kt == pl.num_programs(1) - 1)
    def _():
        acc = acc_s[...]
        o_ref[0] = jnp.concatenate([acc[hd * tq:(hd + 1) * tq] for hd in range(N_HEADS)],
                                   axis=1).astype(o_ref.dtype)


def _stick_sample(q, k_new, v_new, k_past, v_past, layer, *, key_rows, sub_rows):
    b, tq, _ = q.shape
    n_blocks = k_past.shape[2] // key_rows
    grid = (b, n_blocks)
    new_spec = pl.BlockSpec((1, 1, tq * N_HEADS, HEAD_W), lambda bi, kt: (layer, bi, 0, 0))
    past_spec = pl.BlockSpec((1, 1, key_rows, HEAD_W), lambda bi, kt: (layer, bi, n_blocks - 1 - kt, 0))
    rows = N_HEADS * tq
    return pl.pallas_call(
        functools.partial(_stick_sample_kernel, tq=tq, sub_rows=sub_rows),
        grid=grid,
        in_specs=[pl.BlockSpec((1, tq, W_BRANCH), lambda bi, kt: (bi, 0, 0)),
                  new_spec, new_spec, past_spec, past_spec],
        out_specs=pl.BlockSpec((1, tq, W_BRANCH), lambda bi, kt: (bi, 0, 0)),
        out_shape=jax.ShapeDtypeStruct((b, tq, W_BRANCH), q.dtype),
        scratch_shapes=[
            pltpu.VMEM((rows, HEAD_W), q.dtype),
            pltpu.VMEM((sub_rows, sub_rows), q.dtype),
            pltpu.VMEM((rows, 1), F32),
            pltpu.VMEM((rows, HEAD_W), F32),
            pltpu.SMEM((1,), jnp.int32),
        ],
        compiler_params=_cparams(("arbitrary", "arbitrary")),
        name="stick_sample",
    )(q, k_new, v_new, k_past, v_past)


def _group_partner(x, lane, d, width):
    near = pltpu.roll(x, LANES - d, 1)
    far = pltpu.roll(x, width - d, 1)
    return jnp.where((lane & (width - 1)) + d < width, near, far)


def _router_gates(logits, bias, lane):
    aff = jax.nn.sigmoid(logits)
    sel = aff + bias
    pos = lane & (EXPERTS_PER_GROUP - 1)
    rank = jnp.zeros_like(lane)
    for d in range(1, EXPERTS_PER_GROUP):
        other = _group_partner(sel, lane, d, EXPERTS_PER_GROUP)
        other_pos = (pos + d) & (EXPERTS_PER_GROUP - 1)
        ahead = jnp.logical_or(other > sel, jnp.logical_and(other == sel, other_pos < pos))
        rank = rank + jnp.where(ahead, 1, 0)
    in_top2 = rank < 2
    best2 = jnp.where(in_top2, sel, 0.0)
    score = best2
    for d in range(1, EXPERTS_PER_GROUP):
        score = score + _group_partner(best2, lane, d, EXPERTS_PER_GROUP)
    grp = (lane >> 2) & (N_GROUPS - 1)
    beaten = jnp.zeros_like(lane)
    for d in range(1, N_GROUPS):
        other = _group_partner(score, lane, d * EXPERTS_PER_GROUP, N_EXPERTS)
        other_grp = (grp + d) & (N_GROUPS - 1)
        ahead = jnp.logical_or(other > score, jnp.logical_and(other == score, other_grp < grp))
        beaten = beaten + jnp.where(ahead, 1, 0)
    chosen = jnp.logical_and(jnp.logical_and(in_top2, beaten == 0), lane < N_EXPERTS)
    w = jnp.where(chosen, aff, 0.0)
    return w / jnp.sum(w, axis=-1, keepdims=True)


def _mixout_kernel(x_ref, oa_ref, ob_ref, ga_ref, gb_ref, g1_ref, sc_ref, sh_ref, gain_ref,
                   wa_ref, wb_ref, wo_ref, wr_ref, br_ref, x1_ref, h2_ref, gates_ref):
    bb, tt, d = x_ref.shape
    rows = bb * tt
    ya = _mm(oa_ref[...].reshape(rows, W_BRANCH), wa_ref[...])
    yb = _mm(ob_ref[...].reshape(rows, W_BRANCH), wb_ref[...])
    y = (ga_ref[...].reshape(rows, d).astype(F32) * ya + gb_ref[...].reshape(rows, d).astype(F32) * yb)
    mix = _mm(y.astype(wo_ref.dtype), wo_ref[...])
    x1 = x_ref[...] + g1_ref[...] * mix.reshape(bb, tt, d)
    x1_ref[...] = x1
    r = lax.rsqrt(jnp.mean(x1 * x1, axis=-1, keepdims=True) + EPS)
    h2 = ((x1 * r) * gain_ref[...] * (1.0 + sc_ref[...]) + sh_ref[...]).reshape(rows, d)
    h2_ref[...] = h2.reshape(bb, tt, d).astype(h2_ref.dtype)
    logits = _mm(h2.astype(wr_ref.dtype), wr_ref[...])
    lane = lax.broadcasted_iota(jnp.int32, (rows, LANES), 1)
    gates_ref[...] = _router_gates(logits, br_ref[...], lane).reshape(bb, tt, LANES)


def _mixout(x, oa, ob, ga, gb, g1, sc2, sh2, gain, wa, wb, wo, wr, br, bb, tt):
    b, t, d = x.shape
    grid = (b // bb, t // tt)
    xmap = lambda i, j: (i, j, 0)
    modspec = pl.BlockSpec((bb, 1, d), lambda i, j: (i, 0, 0))

    def full(a):
        return pl.BlockSpec(a.shape, lambda i, j: (0,) * a.ndim)

    return pl.pallas_call(
        _mixout_kernel,
        grid=grid,
        in_specs=[
            pl.BlockSpec((bb, tt, d), xmap),
            pl.BlockSpec((bb, tt, W_BRANCH), xmap), pl.BlockSpec((bb, tt, W_BRANCH), xmap),
            pl.BlockSpec((bb, tt, d), xmap), pl.BlockSpec((bb, tt, d), xmap),
            modspec, modspec, modspec,
            pl.BlockSpec((1, 1, d), lambda i, j: (0, 0, 0)),
            full(wa), full(wb), full(wo), full(wr), full(br),
        ],
        out_specs=[pl.BlockSpec((bb, tt, d), xmap), pl.BlockSpec((bb, tt, d), xmap),
                   pl.BlockSpec((bb, tt, LANES), xmap)],
        out_shape=[jax.ShapeDtypeStruct((b, t, d), F32), jax.ShapeDtypeStruct((b, t, d), BF16),
                   jax.ShapeDtypeStruct((b, t, LANES), F32)],
        compiler_params=_cparams(("arbitrary", "arbitrary")),
        name="mixout",
    )(x, oa, ob, ga, gb, g1, sc2, sh2, gain.reshape(1, 1, d), wa, wb, wo, wr, br)


def _moe_kernel(h_ref, gates_ref, x1_ref, g2_ref, gain_ref, wg_ref, wu_ref, wd_ref, o_ref, acc_s,
                *, final_norm):
    bb, tt, d = h_ref.shape
    rows = bb * tt
    e = pl.program_id(2)
    h = h_ref[...].reshape(rows, d)
    g = jnp.dot(h, wg_ref[0], preferred_element_type=F32)
    u = jnp.dot(h, wu_ref[0], preferred_element_type=F32)
    gates = gates_ref[...].reshape(rows, LANES)
    lane = lax.broadcasted_iota(jnp.int32, (rows, LANES), 1)
    gate_e = jnp.sum(jnp.where(lane == e, gates, 0.0), axis=-1, keepdims=True)
    he = ((g * jax.nn.sigmoid(g)) * u * gate_e).astype(BF16)
    part = jnp.dot(he, wd_ref[0], preferred_element_type=F32)

    @pl.when(e == 0)
    def _():
        acc_s[...] = part

    @pl.when(e > 0)
    def _():
        acc_s[...] += part

    @pl.when(e == N_EXPERTS - 1)
    def _():
        x2 = x1_ref[...] + g2_ref[...] * acc_s[...].reshape(bb, tt, d)
        if final_norm:
            r = lax.rsqrt(jnp.mean(x2 * x2, axis=-1, keepdims=True) + EPS)
            x2 = (x2 * r) * gain_ref[...]
        o_ref[...] = x2


def _moe(h2, gates, x1, g2, gain, wg, wu, wd, bb, tt, final_norm):
    b, t, d = x1.shape
    grid = (b // bb, t // tt, N_EXPERTS)
    xmap = lambda i, j, e: (i, j, 0)
    return pl.pallas_call(
        functools.partial(_moe_kernel, final_norm=final_norm),
        grid=grid,
        in_specs=[
            pl.BlockSpec((bb, tt, d), xmap),
            pl.BlockSpec((bb, tt, LANES), xmap),
            pl.BlockSpec((bb, tt, d), xmap),
            pl.BlockSpec((bb, 1, d), lambda i, j, e: (i, 0, 0)),
            pl.BlockSpec((1, 1, d), lambda i, j, e: (0, 0, 0)),
            pl.BlockSpec((1, d, D_EXPERT), lambda i, j, e: (e, 0, 0)),
            pl.BlockSpec((1, d, D_EXPERT), lambda i, j, e: (e, 0, 0)),
            pl.BlockSpec((1, D_EXPERT, d), lambda i, j, e: (e, 0, 0)),
        ],
        out_specs=pl.BlockSpec((bb, tt, d), xmap),
        out_shape=jax.ShapeDtypeStruct((b, t, d), F32),
        scratch_shapes=[pltpu.VMEM((bb * tt, d), F32)],
        compiler_params=_cparams(("arbitrary", "arbitrary", "arbitrary")),
        name="moe",
    )(h2, gates, x1, g2, gain.reshape(1, 1, d), wg, wu, wd)


def _rope_tables(pos):
    half = HD_A // 2
    inv = ROPE_THETA ** (-jnp.arange(half, dtype=F32) / half)
    ang = pos.astype(F32)[:, None] * inv[None, :]
    cos, sin = jnp.cos(ang), jnp.sin(ang)
    reps = LANES // HD_A
    cos_t = jnp.tile(jnp.concatenate([cos, cos], axis=1), (1, reps))
    sin_t = jnp.tile(jnp.concatenate([-sin, sin], axis=1), (1, reps))
    return cos_t, sin_t


def _trunk(x, mod, pos, past, p, *, row_block, moe_rows, attn_tile):
    b, t, d = x.shape
    bb, tt = row_block
    depth = p["norm_mix"].shape[0]
    prompt = past is None
    cos, sin = _rope_tables(pos)
    caches = None
    for l in range(depth):
        wl = p["f32"] if (not prompt and l == 0) else p["bf16"]
        sh1, sc1, g1, sh2, sc2, g2 = [mod[l, :, i][:, None, :] for i in range(6)]
        outs = _inproj(x, p["norm_mix"][l], sc1, sh1, cos, sin, wl["w_in"][l], bb, tt, l, depth,
                       caches, feature_major=prompt)
        caches = outs[:4]
        qa, qb, kab, vab, kbb, vbb, ga, gb = outs[4:]
        lam_init = 0.8 - 0.6 * math.exp(-0.3 * l)
        lw, sub = p["a_lambda"][l], p["a_subln"][l]
        if prompt:
            oa = _diff_prompt(lw, sub[:, None], qa, kab, vab, tq=attn_tile, lam_init=lam_init)
            ob = _stick_prompt(qb, kbb, vbb, tq=min(t, 256))
        else:
            oa = _diff_sample(lw, sub[None, :], qa, caches[0], caches[1], past[0], past[1], l,
                              key_rows=attn_tile, lam_init=lam_init)
            ob = _stick_sample(qb, caches[2], caches[3], past[2], past[3], l,
                               key_rows=attn_tile, sub_rows=min(attn_tile, 512))
        x1, h2, gates = _mixout(x, oa, ob, ga, gb, g1, sc2, sh2, p["norm_ffn"][l],
                                wl["w_proj_a"][l], wl["w_proj_b"][l], wl["w_out"][l],
                                wl["w_router"], p["b_router"], bb, tt)
        x = _moe(h2, gates, x1, g2, p["norm_final"], p["w_e_gate"][l], p["w_e_up"][l],
                 p["w_e_down"][l], bb, moe_rows, final_norm=(l == depth - 1))
    return (x,) + tuple(c.reshape(depth, b, t, N_HEADS, HEAD_W) for c in caches)


def kernel(x_prompt, x_sample, cache_a_k, cache_a_v, cache_b_k, cache_b_v, c_prompt, c_sample,
           w_in, w_proj_a, w_proj_b, w_out, a_lambda, a_subln, w_ada, b_ada,
           norm_mix, norm_ffn, norm_final, w_router, b_router, w_e_gate, w_e_up, w_e_down):
    d = x_prompt.shape[-1]
    bp, tp = x_prompt.shape[:2]
    bs, ts = x_sample.shape[:2]
    depth = w_in.shape[0]
    past_len = cache_a_k.shape[2]
    assert d == D_MODEL and past_len % CHUNK == 0 and ts <= CHUNK

    wr = jnp.zeros((d, LANES), F32).at[:, :N_EXPERTS].set(w_router)
    br = jnp.zeros((1, LANES), F32).at[0, :N_EXPERTS].set(b_router)
    mixer_f32 = dict(w_in=w_in, w_proj_a=w_proj_a, w_proj_b=w_proj_b, w_out=w_out, w_router=wr)
    p = dict(
        f32=mixer_f32, bf16={k: v.astype(BF16) for k, v in mixer_f32.items()},
        a_lambda=a_lambda, a_subln=a_subln,
        norm_mix=norm_mix, norm_ffn=norm_ffn, norm_final=norm_final, b_router=br,
        w_e_gate=w_e_gate.astype(BF16), w_e_up=w_e_up.astype(BF16), w_e_down=w_e_down.astype(BF16),
    )

    n_c = bp + bs
    rows = -(-n_c // 8) * 8
    c_all = jnp.zeros((rows, d), F32).at[:bp].set(c_prompt).at[bp:n_c].set(c_sample)
    mod = _ada_mod(c_all, w_ada, b_ada).reshape(depth, rows, 6, d)

    pos_p = jnp.arange(tp, dtype=jnp.int32)
    pos_s = past_len + jnp.arange(ts, dtype=jnp.int32)
    tile_p = min(tp, 512)
    out_p = _trunk(x_prompt, mod[:, :bp], pos_p, None, p,
                   row_block=(1, tile_p), moe_rows=min(tp, 1024), attn_tile=tile_p)
    past = tuple(c.reshape(depth, bs, past_len * N_HEADS, HEAD_W)
                 for c in (cache_a_k, cache_a_v, cache_b_k, cache_b_v))
    out_s = _trunk(x_sample, mod[:, bp:n_c], pos_s, past, p,
                   row_block=(bs, ts), moe_rows=ts, attn_tile=min(past_len * N_HEADS, 2048))
    return (out_p[0], out_s[0]) + out_p[1:] + out_s[1:]
```

```python
import functools
import math

import jax
import jax.numpy as jnp
from jax import lax
from jax.experimental import pallas as pl
from jax.experimental.pallas import tpu as pltpu

D_MODEL = 1024
CHUNK = 64
N_HEADS = 4
HEAD_W = 128
HD_A = 64
HD_B = 128
W_BRANCH = N_HEADS * HEAD_W
N_EXPERTS = 16
EXPERTS_PER_GROUP = 4
N_GROUPS = N_EXPERTS // EXPERTS_PER_GROUP
D_EXPERT = 256
ROPE_THETA = 10000.0
EPS = 1e-6
NEG_INF = -1e30
STICK_DEAD_LOG = -104.0
LOG2_E = 1.4426950408889634
LANES = 128
VMEM_LIMIT = 56 * 1024 * 1024

F32 = jnp.float32
BF16 = jnp.bfloat16
NT_DIMS = (((1,), (1,)), ((), ()))


def _log2(n):
    assert n > 0 and n & (n - 1) == 0, n
    return n.bit_length() - 1


def _mm(a, b, dims=None):
    assert a.dtype == b.dtype, (a.dtype, b.dtype)
    prec = lax.Precision.HIGHEST if a.dtype == F32 else None
    if dims is None:
        return jnp.dot(a, b, preferred_element_type=F32, precision=prec)
    return lax.dot_general(a, b, dims, preferred_element_type=F32, precision=prec)


def _cparams(sem):
    return pltpu.CompilerParams(dimension_semantics=sem, vmem_limit_bytes=VMEM_LIMIT)


def _ada_kernel(c_ref, w_ref, b_ref, o_ref):
    c = c_ref[...]
    a = c * jax.nn.sigmoid(c)

    @pl.when(pl.program_id(0) == 0)
    def _():
        o_ref[0] = _mm(a, w_ref[0]) + b_ref[0]

    @pl.when(pl.program_id(0) > 0)
    def _():
        o_ref[0] = _mm(a.astype(BF16), w_ref[0].astype(BF16)) + b_ref[0]


def _ada_mod(c_all, w_ada, b_ada, tn=1536):
    depth, d, n6 = w_ada.shape
    rows = c_all.shape[0]
    return pl.pallas_call(
        _ada_kernel,
        grid=(depth, n6 // tn),
        in_specs=[
            pl.BlockSpec((rows, d), lambda l, j: (0, 0)),
            pl.BlockSpec((1, d, tn), lambda l, j: (l, 0, j)),
            pl.BlockSpec((1, 1, tn), lambda l, j: (l, 0, j)),
        ],
        out_specs=pl.BlockSpec((1, rows, tn), lambda l, j: (l, 0, j)),
        out_shape=jax.ShapeDtypeStruct((depth, rows, n6), F32),
        compiler_params=_cparams(("arbitrary", "arbitrary")),
        name="ada_mod",
    )(c_all, w_ada, b_ada.reshape(depth, 1, n6))


def _rope128(x, cos, sin_signed, lane):
    fwd = pltpu.roll(x, LANES - HD_A // 2, 1)
    bwd = pltpu.roll(x, HD_A // 2, 1)
    partner = jnp.where((lane & (HD_A - 1)) < HD_A // 2, fwd, bwd)
    return x * cos + partner * sin_signed


def _inproj_kernel(*refs, n_alias, feature_major):
    (x_ref, g_ref, sc_ref, sh_ref, cos_ref, sin_ref, w_ref) = refs[:7]
    outs = refs[7 + n_alias:]
    (ka_c, va_c, kb_c, vb_c, qa_ref, qb_ref, kab_ref, vab_ref, kbb_ref, vbb_ref, ga_ref, gb_ref) = outs
    bb, tt, d = x_ref.shape
    rows = bb * tt
    x = x_ref[...]
    r = lax.rsqrt(jnp.mean(x * x, axis=-1, keepdims=True) + EPS)
    h = (x * r) * g_ref[...] * (1.0 + sc_ref[...]) + sh_ref[...]
    hb = h.reshape(rows, d).astype(w_ref.dtype)
    cos = cos_ref[...]
    sin = sin_ref[...]
    if bb > 1:
        cos = jnp.concatenate([cos] * bb, axis=0)
        sin = jnp.concatenate([sin] * bb, axis=0)
    lane = lax.broadcasted_iota(jnp.int32, (rows, LANES), 1)

    def proj(c0, width):
        return _mm(hb, w_ref[:, c0:c0 + width])

    def put(ref, val):
        ref[...] = val.reshape(ref.shape).astype(ref.dtype)

    def put_cache(ref, heads):
        for b in range(bb):
            for hd in range(N_HEADS):
                ref[0, b, pl.ds(hd, tt, stride=N_HEADS), :] = heads[hd][b * tt:(b + 1) * tt]

    def put_feature_major(ref, heads):
        for hd in range(N_HEADS):
            ref[0, hd, 0] = heads[hd].T.astype(ref.dtype)

    def split(val):
        return [val[:, hd * HEAD_W:(hd + 1) * HEAD_W] for hd in range(N_HEADS)]

    w = W_BRANCH
    qa = [_rope128(v, cos, sin, lane) * (HD_A ** -0.5 * LOG2_E) for v in split(proj(0, w))]
    ka = [_rope128(v, cos, sin, lane) for v in split(proj(w, w))]
    va = split(proj(2 * w, w))
    put_cache(ka_c, ka)
    put_cache(va_c, va)
    put(kab_ref, jnp.concatenate(ka, axis=1))
    if feature_major:
        put_feature_major(qa_ref, qa)
        put_feature_major(vab_ref, va)
    else:
        put(qa_ref, jnp.concatenate(qa, axis=1))
        put(vab_ref, jnp.concatenate(va, axis=1))
    put(qb_ref, proj(3 * w, w) * (HD_B ** -0.5))
    kb = proj(4 * w, w)
    put_cache(kb_c, split(kb))
    put(kbb_ref, kb)
    vb = proj(5 * w, w)
    put_cache(vb_c, split(vb))
    put(vbb_ref, vb)
    put(ga_ref, jax.nn.sigmoid(proj(6 * w, d)))
    put(gb_ref, jax.nn.sigmoid(proj(6 * w + d, d)))


def _inproj(x, gain, sc, sh, cos, sin, w, bb, tt, layer, depth, caches, feature_major):
    b, t, d = x.shape
    grid = (b // bb, t // tt)
    xmap = lambda i, j: (i, j, 0)
    modspec = pl.BlockSpec((bb, 1, d), lambda i, j: (i, 0, 0))
    tabspec = pl.BlockSpec((tt, LANES), lambda i, j: (j, 0))
    n_alias = 0 if caches is None else 4
    act = w.dtype

    def out(width, dtype):
        return (pl.BlockSpec((bb, tt, width), xmap), jax.ShapeDtypeStruct((b, t, width), dtype))

    def out_t():
        return (pl.BlockSpec((1, N_HEADS, 1, HEAD_W, tt), lambda i, j: (i, 0, j, 0, 0)),
                jax.ShapeDtypeStruct((b, N_HEADS, t // tt, HEAD_W, tt), BF16))

    cache = (pl.BlockSpec((1, bb, tt * N_HEADS, HEAD_W), lambda i, j: (layer, i, j, 0)),
             jax.ShapeDtypeStruct((depth, b, t * N_HEADS, HEAD_W), F32))
    outs = [cache, cache, cache, cache,
            out_t() if feature_major else out(W_BRANCH, act),
            out(W_BRANCH, act),
            out(W_BRANCH, BF16),
            out_t() if feature_major else out(W_BRANCH, BF16),
            out(W_BRANCH, BF16), out(W_BRANCH, BF16),
            out(d, act), out(d, act)]
    in_specs = [
        pl.BlockSpec((bb, tt, d), xmap),
        pl.BlockSpec((1, 1, d), lambda i, j: (0, 0, 0)),
        modspec, modspec, tabspec, tabspec,
        pl.BlockSpec((None,) + w.shape[1:], lambda i, j: (layer, 0, 0)),
    ] + [pl.BlockSpec(memory_space=pl.ANY)] * n_alias
    args = (x, gain.reshape(1, 1, d), sc, sh, cos, sin, w) + (() if caches is None else tuple(caches))
    return pl.pallas_call(
        functools.partial(_inproj_kernel, n_alias=n_alias, feature_major=feature_major),
        grid=grid,
        in_specs=in_specs,
        out_specs=[o[0] for o in outs],
        out_shape=[o[1] for o in outs],
        input_output_aliases={7 + k: k for k in range(n_alias)},
        compiler_params=_cparams(("arbitrary", "arbitrary")),
        name="inproj",
    )(*args)


def _diff_lambda(lw_ref, lam_init):
    lw = lw_ref[...]
    return (jnp.exp(jnp.sum(lw[0:1] * lw[1:2], axis=-1, keepdims=True))
            - jnp.exp(jnp.sum(lw[2:3] * lw[3:4], axis=-1, keepdims=True)) + lam_init)


def _diff_prompt_kernel(lw_ref, sub_ref, qt_ref, k_ref, vt_ref, o_ref, qz_s, s0_s, s1_s, m_s, l_s, acc_s,
                        *, tq, lam_init):
    i = pl.program_id(2)
    cols = 2 * tq
    qt = qt_ref[0, 0, 0]
    feat = lax.broadcasted_iota(jnp.int32, (HEAD_W, tq), 0)
    zero = jnp.zeros_like(qt)
    qz_s[...] = jnp.concatenate([jnp.where(feat < HD_A, qt, zero), jnp.where(feat >= HD_A, qt, zero)], axis=1)
    m_s[...] = jnp.full((1, cols), NEG_INF, F32)
    l_s[...] = jnp.zeros((1, cols), F32)
    acc_s[...] = jnp.zeros((HEAD_W, cols), F32)

    def scores(j, s_ref):
        start = pl.multiple_of(j * tq, tq)
        s_ref[...] = jnp.dot(k_ref[0, pl.ds(start, tq), :], qz_s[...], preferred_element_type=F32)

    def absorb(j, s_ref, masked):
        s = s_ref[...]
        if masked:
            key = lax.broadcasted_iota(jnp.int32, (tq, cols), 0)
            qry = lax.broadcasted_iota(jnp.int32, (tq, cols), 1)
            qry = jnp.where(qry >= tq, qry - tq, qry)
            s = jnp.where((key >> 6) <= (qry >> 6), s, NEG_INF)
        m_prev = m_s[...]
        m_new = jnp.maximum(m_prev, jnp.max(s, axis=0, keepdims=True))
        alpha = jnp.exp2(m_prev - m_new)
        p = jnp.exp2(s - m_new)
        l_s[...] = alpha * l_s[...] + jnp.sum(p, axis=0, keepdims=True)
        acc_s[...] = alpha * acc_s[...] + jnp.dot(vt_ref[0, 0, j], p.astype(BF16), preferred_element_type=F32)
        m_s[...] = m_new

    scores(0, s0_s)

    def pair(pi, carry):
        j = 2 * pi
        scores(j + 1, s1_s)
        absorb(j, s0_s, False)
        scores(j + 2, s0_s)
        absorb(j + 1, s1_s, False)
        return carry

    lax.fori_loop(0, i // 2, pair, 0)

    @pl.when((i & 1) == 0)
    def _():
        absorb(i, s0_s, True)

    @pl.when((i & 1) == 1)
    def _():
        scores(i, s1_s)
        absorb(i - 1, s0_s, False)
        absorb(i, s1_s, True)

    lam = _diff_lambda(lw_ref, lam_init)
    o = acc_s[...] / l_s[...]
    o = o[:, :tq] - lam * o[:, tq:]
    r = lax.rsqrt(jnp.mean(o * o, axis=0, keepdims=True) + EPS)
    o = (o * r) * sub_ref[...] * (1.0 - lam_init)
    o_ref[0] = o.T.astype(o_ref.dtype)


def _diff_prompt(lw, sub_col, qt, k, vt, *, tq, lam_init):
    b, t, _ = k.shape
    nt = t // tq
    grid = (b, N_HEADS, nt)
    return pl.pallas_call(
        functools.partial(_diff_prompt_kernel, tq=tq, lam_init=lam_init),
        grid=grid,
        in_specs=[
            pl.BlockSpec(lw.shape, lambda bi, h, i: (0, 0)),
            pl.BlockSpec((HEAD_W, 1), lambda bi, h, i: (0, 0)),
            pl.BlockSpec((1, 1, 1, HEAD_W, tq), lambda bi, h, i: (bi, h, i, 0, 0)),
            pl.BlockSpec((1, t, HEAD_W), lambda bi, h, i: (bi, 0, h)),
            pl.BlockSpec((1, 1, nt, HEAD_W, tq), lambda bi, h, i: (bi, h, 0, 0, 0)),
        ],
        out_specs=pl.BlockSpec((1, tq, HEAD_W), lambda bi, h, i: (bi, i, h)),
        out_shape=jax.ShapeDtypeStruct((b, t, W_BRANCH), BF16),
        scratch_shapes=[
            pltpu.VMEM((HEAD_W, 2 * tq), BF16),
            pltpu.VMEM((tq, 2 * tq), F32),
            pltpu.VMEM((tq, 2 * tq), F32),
            pltpu.VMEM((1, 2 * tq), F32),
            pltpu.VMEM((1, 2 * tq), F32),
            pltpu.VMEM((HEAD_W, 2 * tq), F32),
        ],
        compiler_params=_cparams(("arbitrary", "arbitrary", "arbitrary")),
        name="diff_prompt",
    )(lw, sub_col, qt, k, vt)


def _diff_sample_kernel(lw_ref, sub_ref, q_ref, kn_ref, vn_ref, kp_ref, vp_ref, o_ref,
                        qz_s, m_s, l_s, acc_s, *, tq, lam_init):
    kt = pl.program_id(1)
    rows = N_HEADS * 2 * tq

    def fold(k, v, new):
        n = k.shape[0]
        s = _mm(qz_s[...], k.astype(qz_s.dtype), NT_DIMS)
        row = lax.broadcasted_iota(jnp.int32, (rows, n), 0)
        col = lax.broadcasted_iota(jnp.int32, (rows, n), 1)
        keep = (row >> _log2(2 * tq)) == (col & (N_HEADS - 1))
        if new:
            keep = jnp.logical_and(keep, ((col >> 2) >> 6) <= ((row & (tq - 1)) >> 6))
        s = jnp.where(keep, s, NEG_INF)
        m_prev = m_s[...]
        m_new = jnp.maximum(m_prev, jnp.max(s, axis=-1, keepdims=True))
        alpha = jnp.exp2(m_prev - m_new)
        p = jnp.exp2(s - jnp.concatenate([m_new] * (n // LANES), axis=1))
        l_s[...] = alpha * l_s[...] + jnp.sum(p, axis=-1, keepdims=True)
        acc_s[...] = alpha * acc_s[...] + _mm(p.astype(qz_s.dtype), v.astype(qz_s.dtype))
        m_s[...] = m_new

    @pl.when(kt == 0)
    def _():
        q = q_ref[0]
        lane = lax.broadcasted_iota(jnp.int32, (tq, HEAD_W), 1)
        parts = []
        for hd in range(N_HEADS):
            qh = q[:, hd * HEAD_W:(hd + 1) * HEAD_W]
            zero = jnp.zeros_like(qh)
            parts += [jnp.where(lane < HD_A, qh, zero), jnp.where(lane >= HD_A, qh, zero)]
        qz_s[...] = jnp.concatenate(parts, axis=0)
        m_s[...] = jnp.full((rows, LANES), NEG_INF, F32)
        l_s[...] = jnp.zeros((rows, LANES), F32)
        acc_s[...] = jnp.zeros((rows, HEAD_W), F32)
        fold(kn_ref[0, 0], vn_ref[0, 0], True)

    fold(kp_ref[0, 0], vp_ref[0, 0], False)

    @pl.when(kt == pl.num_programs(1) - 1)
    def _():
        lam = _diff_lambda(lw_ref, lam_init)
        o = acc_s[...] / l_s[...]
        outs = []
        for hd in range(N_HEADS):
            base = hd * 2 * tq
            oh = o[base:base + tq] - lam * o[base + tq:base + 2 * tq]
            r = lax.rsqrt(jnp.mean(oh * oh, axis=-1, keepdims=True) + EPS)
            outs.append((oh * r) * sub_ref[...] * (1.0 - lam_init))
        o_ref[0] = jnp.concatenate(outs, axis=1).astype(o_ref.dtype)


def _diff_sample(lw, sub_row, q, k_new, v_new, k_past, v_past, layer, *, key_rows, lam_init):
    b, tq, _ = q.shape
    past_rows = k_past.shape[2]
    grid = (b, past_rows // key_rows)
    new_spec = pl.BlockSpec((1, 1, tq * N_HEADS, HEAD_W), lambda bi, kt: (layer, bi, 0, 0))
    past_spec = pl.BlockSpec((1, 1, key_rows, HEAD_W), lambda bi, kt: (layer, bi, kt, 0))
    rows = N_HEADS * 2 * tq
    return pl.pallas_call(
        functools.partial(_diff_sample_kernel, tq=tq, lam_init=lam_init),
        grid=grid,
        in_specs=[
            pl.BlockSpec(lw.shape, lambda bi, kt: (0, 0)),
            pl.BlockSpec((1, HEAD_W), lambda bi, kt: (0, 0)),
            pl.BlockSpec((1, tq, W_BRANCH), lambda bi, kt: (bi, 0, 0)),
            new_spec, new_spec, past_spec, past_spec,
        ],
        out_specs=pl.BlockSpec((1, tq, W_BRANCH), lambda bi, kt: (bi, 0, 0)),
        out_shape=jax.ShapeDtypeStruct((b, tq, W_BRANCH), q.dtype),
        scratch_shapes=[
            pltpu.VMEM((rows, HEAD_W), q.dtype),
            pltpu.VMEM((rows, LANES), F32),
            pltpu.VMEM((rows, LANES), F32),
            pltpu.VMEM((rows, HEAD_W), F32),
        ],
        compiler_params=_cparams(("arbitrary", "arbitrary")),
        name="diff_sample",
    )(lw, sub_row, q, k_new, v_new, k_past, v_past)


def _neg_softplus(z):
    return -(jnp.maximum(z, 0.0) + jnp.log1p(jnp.exp(-jnp.abs(z))))


def _suffix_sums(x, upper):
    if upper.dtype == F32:
        return _mm(x, upper)
    hi = x.astype(BF16)
    lo = (x - hi.astype(F32)).astype(BF16)
    return (jnp.dot(hi, upper, preferred_element_type=F32)
            + jnp.dot(lo, upper, preferred_element_type=F32))


def _upper(n, dtype=BF16):
    j = lax.broadcasted_iota(jnp.int32, (n, n), 0)
    s = lax.broadcasted_iota(jnp.int32, (n, n), 1)
    return jnp.where(j >= s, 1.0, 0.0).astype(dtype)


def _stick_prompt_kernel(q_ref, kd_ref, vd_ref, kq_ref, vq_ref, kp_ref, vp_ref, o_ref, up_s, c_s, acc_s,
                         *, tq):
    i = pl.program_id(2)
    has_prev = jnp.where(i > 0, 1.0, 0.0)
    row = lax.broadcasted_iota(jnp.int32, (tq, tq), 0)
    col = lax.broadcasted_iota(jnp.int32, (tq, tq), 1)
    earlier = col < row
    q = q_ref[0]
    up_s[...] = _upper(tq)
    z = lax.dot_general(q, kd_ref[0], NT_DIMS, preferred_element_type=F32)
    zq = lax.dot_general(q, kq_ref[0], NT_DIMS, preferred_element_type=F32)
    lneg = jnp.where(earlier, _neg_softplus(z), 0.0)
    lnq = _neg_softplus(zq)
    cs = _suffix_sums(lneg, up_s[...])
    csq = _suffix_sums(lnq, up_s[...])
    c0 = jnp.sum(lneg, axis=-1, keepdims=True)
    a = jnp.where(earlier, jnp.exp(z + cs), 0.0)
    aq = jnp.exp(zq + csq + c0) * has_prev
    acc_s[...] = (jnp.dot(a.astype(BF16), vd_ref[0], preferred_element_type=F32)
                  + jnp.dot(aq.astype(BF16), vq_ref[0], preferred_element_type=F32))
    c1 = c0 + jnp.sum(lnq, axis=-1, keepdims=True) * has_prev
    c_s[...] = c1

    def cond(state):
        j, c_max = state
        return jnp.logical_and(j >= 0, c_max > STICK_DEAD_LOG)

    def body(state):
        j, _ = state
        start = pl.multiple_of(j * tq, tq)
        k = kp_ref[0, pl.ds(start, tq), :]
        v = vp_ref[0, pl.ds(start, tq), :]
        zj = lax.dot_general(q_ref[0], k, NT_DIMS, preferred_element_type=F32)
        ln = _neg_softplus(zj)
        csj = _suffix_sums(ln, up_s[...])
        c = c_s[...]
        aj = jnp.exp(zj + csj + c)
        acc_s[...] += jnp.dot(aj.astype(BF16), v, preferred_element_type=F32)
        c_new = c + jnp.sum(ln, axis=-1, keepdims=True)
        c_s[...] = c_new
        return j - 1, jnp.max(c_new)

    lax.while_loop(cond, body, (i - 2, jnp.max(c1)))
    o_ref[0] = acc_s[...].astype(o_ref.dtype)


def _stick_prompt(q, k, v, *, tq):
    b, t, _ = q.shape
    grid = (b, N_HEADS, t // tq)
    tile = pl.BlockSpec((1, tq, HEAD_W), lambda bi, h, i: (bi, i, h))
    before = pl.BlockSpec((1, tq, HEAD_W), lambda bi, h, i: (bi, jnp.maximum(i - 1, 0), h))
    whole = pl.BlockSpec((1, t, HEAD_W), lambda bi, h, i: (bi, 0, h))
    return pl.pallas_call(
        functools.partial(_stick_prompt_kernel, tq=tq),
        grid=grid,
        in_specs=[tile, tile, tile, before, before, whole, whole],
        out_specs=tile,
        out_shape=jax.ShapeDtypeStruct((b, t, W_BRANCH), BF16),
        scratch_shapes=[
            pltpu.VMEM((tq, tq), BF16),
            pltpu.VMEM((tq, 1), F32),
            pltpu.VMEM((tq, HEAD_W), F32),
        ],
        compiler_params=_cparams(("arbitrary", "arbitrary", "arbitrary")),
        name="stick_prompt",
    )(q, k, v, k, v, k, v)


def _stick_sample_kernel(q_ref, kn_ref, vn_ref, kp_ref, vp_ref, o_ref, q_s, up_s, c_s, acc_s, live_s,
                         *, tq, sub_rows):
    kt = pl.program_id(1)
    rows = N_HEADS * tq

    def fold(k, v, new):
        n = k.shape[0]
        z = _mm(q_s[...], k.astype(q_s.dtype), NT_DIMS)
        row = lax.broadcasted_iota(jnp.int32, (rows, n), 0)
        col = lax.broadcasted_iota(jnp.int32, (rows, n), 1)
        keep = (row >> _log2(tq)) == (col & (N_HEADS - 1))
        if new:
            keep = jnp.logical_and(keep, (col >> 2) < (row & (tq - 1)))
        ln = jnp.where(keep, _neg_softplus(z), 0.0)
        cs = _suffix_sums(ln, _upper(n, up_s.dtype) if new else up_s[...])
        c = c_s[...]
        a = jnp.where(keep, jnp.exp(z + cs + c), 0.0)
        acc_s[...] += _mm(a.astype(q_s.dtype), v.astype(q_s.dtype))
        c_new = c + jnp.sum(ln, axis=-1, keepdims=True)
        c_s[...] = c_new
        live_s[0] = (jnp.max(c_new) > STICK_DEAD_LOG).astype(jnp.int32)

    @pl.when(kt == 0)
    def _():
        q = q_ref[0]
        q_s[...] = jnp.concatenate([q[:, hd * HEAD_W:(hd + 1) * HEAD_W] for hd in range(N_HEADS)], axis=0)
        up_s[...] = _upper(sub_rows, up_s.dtype)
        c_s[...] = jnp.zeros((rows, 1), F32)
        acc_s[...] = jnp.zeros((rows, HEAD_W), F32)
        fold(kn_ref[0, 0], vn_ref[0, 0], True)

    n_sub = kp_ref.shape[2] // sub_rows
    for sb in reversed(range(n_sub)):
        @pl.when(live_s[0] > 0)
        def _(sb=sb):
            sl = slice(sb * sub_rows, (sb + 1) * sub_rows)
            fold(kp_ref[0, 0, sl, :], vp_ref[0, 0, sl, :], False)

    @pl.when(kt == pl.num_programs(1) - 1)
    def _():
        acc = acc_s[...]
        o_ref[0] = jnp.concatenate([acc[hd * tq:(hd + 1) * tq] for hd in range(N_HEADS)],
                                   axis=1).astype(o_ref.dtype)


def _stick_sample(q, k_new, v_new, k_past, v_past, layer, *, key_rows, sub_rows):
    b, tq, _ = q.shape
    n_blocks = k_past.shape[2] // key_rows
    grid = (b, n_blocks)
    new_spec = pl.BlockSpec((1, 1, tq * N_HEADS, HEAD_W), lambda bi, kt: (layer, bi, 0, 0))
    past_spec = pl.BlockSpec((1, 1, key_rows, HEAD_W), lambda bi, kt: (layer, bi, n_blocks - 1 - kt, 0))
    rows = N_HEADS * tq
    return pl.pallas_call(
        functools.partial(_stick_sample_kernel, tq=tq, sub_rows=sub_rows),
        grid=grid,
        in_specs=[pl.BlockSpec((1, tq, W_BRANCH), lambda bi, kt: (bi, 0, 0)),
                  new_spec, new_spec, past_spec, past_spec],
        out_specs=pl.BlockSpec((1, tq, W_BRANCH), lambda bi, kt: (bi, 0, 0)),
        out_shape=jax.ShapeDtypeStruct((b, tq, W_BRANCH), q.dtype),
        scratch_shapes=[
            pltpu.VMEM((rows, HEAD_W), q.dtype),
            pltpu.VMEM((sub_rows, sub_rows), q.dtype),
            pltpu.VMEM((rows, 1), F32),
            pltpu.VMEM((rows, HEAD_W), F32),
            pltpu.SMEM((1,), jnp.int32),
        ],
        compiler_params=_cparams(("arbitrary", "arbitrary")),
        name="stick_sample",
    )(q, k_new, v_new, k_past, v_past)


def _router_gates_t(logits_t, bias_col):
    n = EXPERTS_PER_GROUP
    aff = jax.nn.sigmoid(logits_t)
    sel = aff + bias_col
    pos = [sel[n * k:n * (k + 1)] for k in range(n)]
    in_top2 = []
    for j in range(n):
        rank = jnp.zeros(pos[j].shape, jnp.int32)
        for i in range(n):
            if i != j:
                ahead = (pos[i] >= pos[j]) if i < j else (pos[i] > pos[j])
                rank = rank + jnp.where(ahead, 1, 0)
        in_top2.append(rank < 2)
    score = sum(jnp.where(in_top2[j], pos[j], 0.0) for j in range(n))
    grp = lax.broadcasted_iota(jnp.int32, score.shape, 0)
    beaten = jnp.zeros(score.shape, jnp.int32)
    for g2 in range(N_GROUPS):
        other = score[g2:g2 + 1]
        ahead = jnp.logical_or(other > score, jnp.logical_and(other == score, g2 < grp))
        beaten = beaten + jnp.where(ahead, 1, 0)
    chosen = beaten == 0
    w = [jnp.where(jnp.logical_and(in_top2[j], chosen), aff[n * j:n * (j + 1)], 0.0) for j in range(n)]
    total = jnp.sum(sum(w), axis=0, keepdims=True)
    return jnp.concatenate([wj / total for wj in w], axis=0)


def _mixout_kernel(x_ref, oa_ref, ob_ref, ga_ref, gb_ref, g1_ref, sc_ref, sh_ref, gain_ref,
                   wa_ref, wb_ref, wo_ref, wr_ref, br_ref, x1_ref, h2_ref, gates_ref):
    bb, tt, d = x_ref.shape
    rows = bb * tt
    ya = _mm(oa_ref[...].reshape(rows, W_BRANCH), wa_ref[...])
    yb = _mm(ob_ref[...].reshape(rows, W_BRANCH), wb_ref[...])
    y = (ga_ref[...].reshape(rows, d).astype(F32) * ya + gb_ref[...].reshape(rows, d).astype(F32) * yb)
    mix = _mm(y.astype(wo_ref.dtype), wo_ref[...])
    x1 = x_ref[...] + g1_ref[...] * mix.reshape(bb, tt, d)
    x1_ref[...] = x1
    r = lax.rsqrt(jnp.mean(x1 * x1, axis=-1, keepdims=True) + EPS)
    h2 = ((x1 * r) * gain_ref[...] * (1.0 + sc_ref[...]) + sh_ref[...]).reshape(rows, d)
    h2_ref[...] = h2.reshape(bb, tt, d).astype(h2_ref.dtype)
    logits_t = _mm(wr_ref[...], h2.astype(wr_ref.dtype), NT_DIMS)
    gates_t = _router_gates_t(logits_t, br_ref[...])
    pad = jnp.zeros((LANES - N_EXPERTS, rows), F32)
    gates_ref[...] = jnp.concatenate([gates_t, pad], axis=0).T.reshape(bb, tt, LANES)


def _mixout(x, oa, ob, ga, gb, g1, sc2, sh2, gain, wa, wb, wo, wr, br, bb, tt, layer):
    b, t, d = x.shape
    grid = (b // bb, t // tt)
    xmap = lambda i, j: (i, j, 0)
    modspec = pl.BlockSpec((bb, 1, d), lambda i, j: (i, 0, 0))

    def full(a):
        return pl.BlockSpec(a.shape, lambda i, j: (0,) * a.ndim)

    def of_layer(a):
        return pl.BlockSpec((None,) + a.shape[1:], lambda i, j: (layer,) + (0,) * (a.ndim - 1))

    return pl.pallas_call(
        _mixout_kernel,
        grid=grid,
        in_specs=[
            pl.BlockSpec((bb, tt, d), xmap),
            pl.BlockSpec((bb, tt, W_BRANCH), xmap), pl.BlockSpec((bb, tt, W_BRANCH), xmap),
            pl.BlockSpec((bb, tt, d), xmap), pl.BlockSpec((bb, tt, d), xmap),
            modspec, modspec, modspec,
            pl.BlockSpec((1, 1, d), lambda i, j: (0, 0, 0)),
            of_layer(wa), of_layer(wb), of_layer(wo), full(wr), full(br),
        ],
        out_specs=[pl.BlockSpec((bb, tt, d), xmap), pl.BlockSpec((bb, tt, d), xmap),
                   pl.BlockSpec((bb, tt, LANES), xmap)],
        out_shape=[jax.ShapeDtypeStruct((b, t, d), F32), jax.ShapeDtypeStruct((b, t, d), BF16),
                   jax.ShapeDtypeStruct((b, t, LANES), F32)],
        compiler_params=_cparams(("arbitrary", "arbitrary")),
        name="mixout",
    )(x, oa, ob, ga, gb, g1, sc2, sh2, gain.reshape(1, 1, d), wa, wb, wo, wr, br)


def _moe_kernel(h_ref, gates_ref, x1_ref, g2_ref, gain_ref, wg_ref, wu_ref, wd_ref, o_ref, acc_s,
                *, final_norm):
    bb, tt, d = h_ref.shape
    rows = bb * tt
    grp = pl.program_id(2)
    h = h_ref[...].reshape(rows, d)
    gates = gates_ref[...].reshape(rows, LANES)
    lane = lax.broadcasted_iota(jnp.int32, (rows, LANES), 1)

    @pl.when(grp == 0)
    def _():
        acc_s[...] = jnp.zeros(acc_s.shape, F32)

    hidden = []
    for k in range(EXPERTS_PER_GROUP):
        g = jnp.dot(h, wg_ref[k], preferred_element_type=F32)
        u = jnp.dot(h, wu_ref[k], preferred_element_type=F32)
        gate = jnp.sum(jnp.where(lane == EXPERTS_PER_GROUP * k + grp, gates, 0.0), axis=-1, keepdims=True)
        hidden.append(((g * jax.nn.sigmoid(g)) * u * gate).astype(BF16))
    acc_s[...] += jnp.dot(jnp.concatenate(hidden, axis=1), wd_ref[...], preferred_element_type=F32)

    @pl.when(grp == N_GROUPS - 1)
    def _():
        x2 = x1_ref[...] + g2_ref[...] * acc_s[...].reshape(bb, tt, d)
        if final_norm:
            r = lax.rsqrt(jnp.mean(x2 * x2, axis=-1, keepdims=True) + EPS)
            x2 = (x2 * r) * gain_ref[...]
        o_ref[...] = x2


def _moe(h2, gates, x1, g2, gain, wg, wu, wd, bb, tt, layer, final_norm):
    b, t, d = x1.shape
    grid = (b // bb, t // tt, N_GROUPS)
    xmap = lambda i, j, e: (i, j, 0)
    n = EXPERTS_PER_GROUP
    return pl.pallas_call(
        functools.partial(_moe_kernel, final_norm=final_norm),
        grid=grid,
        in_specs=[
            pl.BlockSpec((bb, tt, d), xmap),
            pl.BlockSpec((bb, tt, LANES), xmap),
            pl.BlockSpec((bb, tt, d), xmap),
            pl.BlockSpec((bb, 1, d), lambda i, j, e: (i, 0, 0)),
            pl.BlockSpec((1, 1, d), lambda i, j, e: (0, 0, 0)),
            pl.BlockSpec((None, n, d, D_EXPERT), lambda i, j, e: (layer, e, 0, 0)),
            pl.BlockSpec((None, n, d, D_EXPERT), lambda i, j, e: (layer, e, 0, 0)),
            pl.BlockSpec((None, None, n * D_EXPERT, d), lambda i, j, e: (layer, e, 0, 0)),
        ],
        out_specs=pl.BlockSpec((bb, tt, d), xmap),
        out_shape=jax.ShapeDtypeStruct((b, t, d), F32),
        scratch_shapes=[pltpu.VMEM((bb * tt, d), F32)],
        compiler_params=_cparams(("arbitrary", "arbitrary", "arbitrary")),
        name="moe",
    )(h2, gates, x1, g2, gain.reshape(1, 1, d), wg, wu, wd)


def _rope_tables(pos):
    half = HD_A // 2
    inv = ROPE_THETA ** (-jnp.arange(half, dtype=F32) / half)
    ang = pos.astype(F32)[:, None] * inv[None, :]
    cos, sin = jnp.cos(ang), jnp.sin(ang)
    reps = LANES // HD_A
    cos_t = jnp.tile(jnp.concatenate([cos, cos], axis=1), (1, reps))
    sin_t = jnp.tile(jnp.concatenate([-sin, sin], axis=1), (1, reps))
    return cos_t, sin_t


def _trunk(x, mod, pos, past, p, *, row_block, moe_rows, attn_tile):
    b, t, d = x.shape
    bb, tt = row_block
    depth = p["norm_mix"].shape[0]
    prompt = past is None
    cos, sin = _rope_tables(pos)
    caches = None
    for l in range(depth):
        wl = p["f32"] if (not prompt and l == 0) else p["bf16"]
        sh1, sc1, g1, sh2, sc2, g2 = [mod[l, :, i][:, None, :] for i in range(6)]
        outs = _inproj(x, p["norm_mix"][l], sc1, sh1, cos, sin, wl["w_in"], bb, tt, l, depth,
                       caches, feature_major=prompt)
        caches = outs[:4]
        qa, qb, kab, vab, kbb, vbb, ga, gb = outs[4:]
        lam_init = 0.8 - 0.6 * math.exp(-0.3 * l)
        lw, sub = p["a_lambda"][l], p["a_subln"][l]
        if prompt:
            oa = _diff_prompt(lw, sub[:, None], qa, kab, vab, tq=attn_tile, lam_init=lam_init)
            ob = _stick_prompt(qb, kbb, vbb, tq=min(t, 256))
        else:
            oa = _diff_sample(lw, sub[None, :], qa, caches[0], caches[1], past[0], past[1], l,
                              key_rows=attn_tile, lam_init=lam_init)
            ob = _stick_sample(qb, caches[2], caches[3], past[2], past[3], l,
                               key_rows=attn_tile, sub_rows=min(attn_tile, 512))
        x1, h2, gates = _mixout(x, oa, ob, ga, gb, g1, sc2, sh2, p["norm_ffn"][l],
                                wl["w_proj_a"], wl["w_proj_b"], wl["w_out"],
                                wl["w_router"], p["b_router"], bb, tt, l)
        x = _moe(h2, gates, x1, g2, p["norm_final"], p["w_e_gate"], p["w_e_up"],
                 p["w_e_down"], bb, moe_rows, l, final_norm=(l == depth - 1))
    return (x,) + tuple(c.reshape(depth, b, t, N_HEADS, HEAD_W) for c in caches)


def kernel(x_prompt, x_sample, cache_a_k, cache_a_v, cache_b_k, cache_b_v, c_prompt, c_sample,
           w_in, w_proj_a, w_proj_b, w_out, a_lambda, a_subln, w_ada, b_ada,
           norm_mix, norm_ffn, norm_final, w_router, b_router, w_e_gate, w_e_up, w_e_down):
    d = x_prompt.shape[-1]
    bp, tp = x_prompt.shape[:2]
    bs, ts = x_sample.shape[:2]
    depth = w_in.shape[0]
    past_len = cache_a_k.shape[2]
    assert d == D_MODEL and past_len % CHUNK == 0 and ts <= CHUNK

    wr = w_router.T.reshape(N_GROUPS, EXPERTS_PER_GROUP, d).transpose(1, 0, 2).reshape(N_EXPERTS, d)
    br = b_router.reshape(N_GROUPS, EXPERTS_PER_GROUP).T.reshape(N_EXPERTS, 1)
    mixer_f32 = dict(w_in=w_in, w_proj_a=w_proj_a, w_proj_b=w_proj_b, w_out=w_out, w_router=wr)
    p = dict(
        f32=mixer_f32, bf16={k: v.astype(BF16) for k, v in mixer_f32.items()},
        a_lambda=a_lambda, a_subln=a_subln,
        norm_mix=norm_mix, norm_ffn=norm_ffn, norm_final=norm_final, b_router=br,
        w_e_gate=w_e_gate.astype(BF16), w_e_up=w_e_up.astype(BF16),
        w_e_down=w_e_down.astype(BF16).reshape(depth, N_GROUPS, EXPERTS_PER_GROUP * D_EXPERT, d),
    )

    n_c = bp + bs
    rows = -(-n_c // 8) * 8
    c_all = jnp.zeros((rows, d), F32).at[:bp].set(c_prompt).at[bp:n_c].set(c_sample)
    mod = _ada_mod(c_all, w_ada, b_ada).reshape(depth, rows, 6, d)

    pos_p = jnp.arange(tp, dtype=jnp.int32)
    pos_s = past_len + jnp.arange(ts, dtype=jnp.int32)
    tile_p = min(tp, 512)
    out_p = _trunk(x_prompt, mod[:, :bp], pos_p, None, p,
                   row_block=(1, tile_p), moe_rows=min(tp, 1024), attn_tile=tile_p)
    past = tuple(c.reshape(depth, bs, past_len * N_HEADS, HEAD_W)
                 for c in (cache_a_k, cache_a_v, cache_b_k, cache_b_v))
    out_s = _trunk(x_sample, mod[:, bp:n_c], pos_s, past, p,
                   row_block=(bs, ts), moe_rows=ts, attn_tile=min(past_len * N_HEADS, 2048))
    return (out_p[0], out_s[0]) + out_p[1:] + out_s[1:]
```

```python
import functools
import math

import jax
import jax.numpy as jnp
from jax import lax
from jax.experimental import pallas as pl
from jax.experimental.pallas import tpu as pltpu

D_MODEL = 1024
CHUNK = 64
N_HEADS = 4
HEAD_W = 128
HD_A = 64
HD_B = 128
W_BRANCH = N_HEADS * HEAD_W
N_EXPERTS = 16
EXPERTS_PER_GROUP = 4
N_GROUPS = N_EXPERTS // EXPERTS_PER_GROUP
D_EXPERT = 256
ROPE_THETA = 10000.0
EPS = 1e-6
NEG_INF = -1e30
STICK_DEAD_LOG = -104.0
LOG2_E = 1.4426950408889634
LANES = 128
VMEM_LIMIT = 56 * 1024 * 1024

F32 = jnp.float32
BF16 = jnp.bfloat16
NT_DIMS = (((1,), (1,)), ((), ()))


def _log2(n):
    assert n > 0 and n & (n - 1) == 0, n
    return n.bit_length() - 1


def _mm(a, b, dims=None):
    assert a.dtype == b.dtype, (a.dtype, b.dtype)
    prec = lax.Precision.HIGHEST if a.dtype == F32 else None
    if dims is None:
        return jnp.dot(a, b, preferred_element_type=F32, precision=prec)
    return lax.dot_general(a, b, dims, preferred_element_type=F32, precision=prec)


def _cparams(sem):
    return pltpu.CompilerParams(dimension_semantics=sem, vmem_limit_bytes=VMEM_LIMIT)


def _ada_kernel(c_ref, w_ref, b_ref, o_ref):
    c = c_ref[...]
    a = c * jax.nn.sigmoid(c)

    @pl.when(pl.program_id(0) == 0)
    def _():
        o_ref[0] = _mm(a, w_ref[0]) + b_ref[0]

    @pl.when(pl.program_id(0) > 0)
    def _():
        o_ref[0] = _mm(a.astype(BF16), w_ref[0].astype(BF16)) + b_ref[0]


def _ada_mod(c_all, w_ada, b_ada, tn=1536):
    depth, d, n6 = w_ada.shape
    rows = c_all.shape[0]
    return pl.pallas_call(
        _ada_kernel,
        grid=(depth, n6 // tn),
        in_specs=[
            pl.BlockSpec((rows, d), lambda l, j: (0, 0)),
            pl.BlockSpec((1, d, tn), lambda l, j: (l, 0, j)),
            pl.BlockSpec((1, 1, tn), lambda l, j: (l, 0, j)),
        ],
        out_specs=pl.BlockSpec((1, rows, tn), lambda l, j: (l, 0, j)),
        out_shape=jax.ShapeDtypeStruct((depth, rows, n6), F32),
        compiler_params=_cparams(("arbitrary", "arbitrary")),
        name="ada_mod",
    )(c_all, w_ada, b_ada.reshape(depth, 1, n6))


def _rope128(x, cos, sin_signed, lane):
    fwd = pltpu.roll(x, LANES - HD_A // 2, 1)
    bwd = pltpu.roll(x, HD_A // 2, 1)
    partner = jnp.where((lane & (HD_A - 1)) < HD_A // 2, fwd, bwd)
    return x * cos + partner * sin_signed


def _inproj_kernel(*refs, n_alias, feature_major):
    (x_ref, g_ref, sc_ref, sh_ref, cos_ref, sin_ref, w_ref) = refs[:7]
    outs = refs[7 + n_alias:]
    (ka_c, va_c, kb_c, vb_c, qa_ref, qb_ref, kab_ref, vab_ref, kbb_ref, vbb_ref, ga_ref, gb_ref) = outs
    bb, tt, d = x_ref.shape
    rows = bb * tt
    x = x_ref[...]
    r = lax.rsqrt(jnp.mean(x * x, axis=-1, keepdims=True) + EPS)
    h = (x * r) * g_ref[...] * (1.0 + sc_ref[...]) + sh_ref[...]
    hb = h.reshape(rows, d).astype(w_ref.dtype)
    cos = cos_ref[...]
    sin = sin_ref[...]
    if bb > 1:
        cos = jnp.concatenate([cos] * bb, axis=0)
        sin = jnp.concatenate([sin] * bb, axis=0)
    lane = lax.broadcasted_iota(jnp.int32, (rows, LANES), 1)

    def proj(c0, width):
        return _mm(hb, w_ref[:, c0:c0 + width])

    def put(ref, val):
        ref[...] = val.reshape(ref.shape).astype(ref.dtype)

    def put_cache(ref, heads):
        for b in range(bb):
            for hd in range(N_HEADS):
                ref[0, b, pl.ds(hd, tt, stride=N_HEADS), :] = heads[hd][b * tt:(b + 1) * tt]

    def put_feature_major(ref, heads):
        for hd in range(N_HEADS):
            ref[0, hd, 0] = heads[hd].T.astype(ref.dtype)

    def split(val):
        return [val[:, hd * HEAD_W:(hd + 1) * HEAD_W] for hd in range(N_HEADS)]

    w = W_BRANCH
    qa = [_rope128(v, cos, sin, lane) * (HD_A ** -0.5 * LOG2_E) for v in split(proj(0, w))]
    ka = [_rope128(v, cos, sin, lane) for v in split(proj(w, w))]
    va = split(proj(2 * w, w))
    put_cache(ka_c, ka)
    put_cache(va_c, va)
    put(kab_ref, jnp.concatenate(ka, axis=1))
    if feature_major:
        put_feature_major(qa_ref, qa)
        put_feature_major(vab_ref, va)
    else:
        put(qa_ref, jnp.concatenate(qa, axis=1))
        put(vab_ref, jnp.concatenate(va, axis=1))
    put(qb_ref, proj(3 * w, w) * (HD_B ** -0.5))
    kb = proj(4 * w, w)
    put_cache(kb_c, split(kb))
    put(kbb_ref, kb)
    vb = proj(5 * w, w)
    put_cache(vb_c, split(vb))
    put(vbb_ref, vb)
    put(ga_ref, jax.nn.sigmoid(proj(6 * w, d)))
    put(gb_ref, jax.nn.sigmoid(proj(6 * w + d, d)))


def _inproj(x, gain, sc, sh, cos, sin, w, bb, tt, layer, depth, caches, feature_major):
    b, t, d = x.shape
    grid = (b // bb, t // tt)
    xmap = lambda i, j: (i, j, 0)
    modspec = pl.BlockSpec((bb, 1, d), lambda i, j: (i, 0, 0))
    tabspec = pl.BlockSpec((tt, LANES), lambda i, j: (j, 0))
    n_alias = 0 if caches is None else 4
    act = w.dtype

    def out(width, dtype):
        return (pl.BlockSpec((bb, tt, width), xmap), jax.ShapeDtypeStruct((b, t, width), dtype))

    def out_t():
        return (pl.BlockSpec((1, N_HEADS, 1, HEAD_W, tt), lambda i, j: (i, 0, j, 0, 0)),
                jax.ShapeDtypeStruct((b, N_HEADS, t // tt, HEAD_W, tt), BF16))

    cache = (pl.BlockSpec((1, bb, tt * N_HEADS, HEAD_W), lambda i, j: (layer, i, j, 0)),
             jax.ShapeDtypeStruct((depth, b, t * N_HEADS, HEAD_W), F32))
    outs = [cache, cache, cache, cache,
            out_t() if feature_major else out(W_BRANCH, act),
            out(W_BRANCH, act),
            out(W_BRANCH, BF16),
            out_t() if feature_major else out(W_BRANCH, BF16),
            out(W_BRANCH, BF16), out(W_BRANCH, BF16),
            out(d, act), out(d, act)]
    in_specs = [
        pl.BlockSpec((bb, tt, d), xmap),
        pl.BlockSpec((1, 1, d), lambda i, j: (0, 0, 0)),
        modspec, modspec, tabspec, tabspec,
        pl.BlockSpec((None,) + w.shape[1:], lambda i, j: (layer, 0, 0)),
    ] + [pl.BlockSpec(memory_space=pl.ANY)] * n_alias
    args = (x, gain.reshape(1, 1, d), sc, sh, cos, sin, w) + (() if caches is None else tuple(caches))
    return pl.pallas_call(
        functools.partial(_inproj_kernel, n_alias=n_alias, feature_major=feature_major),
        grid=grid,
        in_specs=in_specs,
        out_specs=[o[0] for o in outs],
        out_shape=[o[1] for o in outs],
        input_output_aliases={7 + k: k for k in range(n_alias)},
        compiler_params=_cparams(("arbitrary", "arbitrary")),
        name="inproj",
    )(*args)


def _diff_lambda(lw_ref, lam_init):
    lw = lw_ref[...]
    return (jnp.exp(jnp.sum(lw[0:1] * lw[1:2], axis=-1, keepdims=True))
            - jnp.exp(jnp.sum(lw[2:3] * lw[3:4], axis=-1, keepdims=True)) + lam_init)


def _diff_prompt_kernel(lw_ref, sub_ref, qt_ref, k_ref, vt_ref, o_ref, qz_s, s0_s, s1_s, m_s, l_s, acc_s,
                        *, tq, lam_init):
    i = pl.program_id(2)
    cols = 2 * tq
    qt = qt_ref[0, 0, 0]
    feat = lax.broadcasted_iota(jnp.int32, (HEAD_W, tq), 0)
    zero = jnp.zeros_like(qt)
    qz_s[...] = jnp.concatenate([jnp.where(feat < HD_A, qt, zero), jnp.where(feat >= HD_A, qt, zero)], axis=1)
    m_s[...] = jnp.full((1, cols), NEG_INF, F32)
    l_s[...] = jnp.zeros((1, cols), F32)
    acc_s[...] = jnp.zeros((HEAD_W, cols), F32)

    def scores(j, s_ref):
        start = pl.multiple_of(j * tq, tq)
        s_ref[...] = jnp.dot(k_ref[0, pl.ds(start, tq), :], qz_s[...], preferred_element_type=F32)

    def absorb(j, s_ref, masked):
        s = s_ref[...]
        if masked:
            key = lax.broadcasted_iota(jnp.int32, (tq, cols), 0)
            qry = lax.broadcasted_iota(jnp.int32, (tq, cols), 1)
            qry = jnp.where(qry >= tq, qry - tq, qry)
            s = jnp.where((key >> 6) <= (qry >> 6), s, NEG_INF)
        m_prev = m_s[...]
        m_new = jnp.maximum(m_prev, jnp.max(s, axis=0, keepdims=True))
        alpha = jnp.exp2(m_prev - m_new)
        p = jnp.exp2(s - m_new)
        l_s[...] = alpha * l_s[...] + jnp.sum(p, axis=0, keepdims=True)
        acc_s[...] = alpha * acc_s[...] + jnp.dot(vt_ref[0, 0, j], p.astype(BF16), preferred_element_type=F32)
        m_s[...] = m_new

    scores(0, s0_s)

    def pair(pi, carry):
        j = 2 * pi
        scores(j + 1, s1_s)
        absorb(j, s0_s, False)
        scores(j + 2, s0_s)
        absorb(j + 1, s1_s, False)
        return carry

    lax.fori_loop(0, i // 2, pair, 0)

    @pl.when((i & 1) == 0)
    def _():
        absorb(i, s0_s, True)

    @pl.when((i & 1) == 1)
    def _():
        scores(i, s1_s)
        absorb(i - 1, s0_s, False)
        absorb(i, s1_s, True)

    lam = _diff_lambda(lw_ref, lam_init)
    o = acc_s[...] / l_s[...]
    o = o[:, :tq] - lam * o[:, tq:]
    r = lax.rsqrt(jnp.mean(o * o, axis=0, keepdims=True) + EPS)
    o = (o * r) * sub_ref[...] * (1.0 - lam_init)
    o_ref[0] = o.T.astype(o_ref.dtype)


def _diff_prompt(lw, sub_col, qt, k, vt, *, tq, lam_init):
    b, t, _ = k.shape
    nt = t // tq
    grid = (b, N_HEADS, nt)
    return pl.pallas_call(
        functools.partial(_diff_prompt_kernel, tq=tq, lam_init=lam_init),
        grid=grid,
        in_specs=[
            pl.BlockSpec(lw.shape, lambda bi, h, i: (0, 0)),
            pl.BlockSpec((HEAD_W, 1), lambda bi, h, i: (0, 0)),
            pl.BlockSpec((1, 1, 1, HEAD_W, tq), lambda bi, h, i: (bi, h, i, 0, 0)),
            pl.BlockSpec((1, t, HEAD_W), lambda bi, h, i: (bi, 0, h)),
            pl.BlockSpec((1, 1, nt, HEAD_W, tq), lambda bi, h, i: (bi, h, 0, 0, 0)),
        ],
        out_specs=pl.BlockSpec((1, tq, HEAD_W), lambda bi, h, i: (bi, i, h)),
        out_shape=jax.ShapeDtypeStruct((b, t, W_BRANCH), BF16),
        scratch_shapes=[
            pltpu.VMEM((HEAD_W, 2 * tq), BF16),
            pltpu.VMEM((tq, 2 * tq), F32),
            pltpu.VMEM((tq, 2 * tq), F32),
            pltpu.VMEM((1, 2 * tq), F32),
            pltpu.VMEM((1, 2 * tq), F32),
            pltpu.VMEM((HEAD_W, 2 * tq), F32),
        ],
        compiler_params=_cparams(("arbitrary", "arbitrary", "arbitrary")),
        name="diff_prompt",
    )(lw, sub_col, qt, k, vt)


def _diff_sample_kernel(lw_ref, sub_ref, q_ref, kn_ref, vn_ref, kp_ref, vp_ref, o_ref,
                        qz_s, m_s, l_s, acc_s, *, tq, lam_init):
    kt = pl.program_id(1)
    rows = N_HEADS * 2 * tq
    cdt = qz_s.dtype

    def fold(k_ref, v_ref, new):
        n = k_ref.shape[2] // N_HEADS
        heads = range(N_HEADS)
        mine = [pl.ds(hd * 2 * tq, 2 * tq) for hd in heads]
        scores = []
        for hd in heads:
            k = k_ref[0, 0, pl.ds(hd, n, stride=N_HEADS), :].astype(cdt)
            s = _mm(qz_s[mine[hd], :], k, NT_DIMS)
            if new:
                row = lax.broadcasted_iota(jnp.int32, s.shape, 0)
                col = lax.broadcasted_iota(jnp.int32, s.shape, 1)
                s = jnp.where((col >> 6) <= ((row & (tq - 1)) >> 6), s, NEG_INF)
            scores.append(s)
        m_prev = [m_s[mine[hd], :] for hd in heads]
        m_new = [jnp.maximum(m_prev[hd], jnp.max(scores[hd], axis=-1, keepdims=True)) for hd in heads]
        alpha = [jnp.exp2(m_prev[hd] - m_new[hd]) for hd in heads]
        probs = [jnp.exp2(scores[hd] - m_new[hd][:, :1]) for hd in heads]
        for hd in heads:
            v = v_ref[0, 0, pl.ds(hd, n, stride=N_HEADS), :].astype(cdt)
            l_s[mine[hd], :] = alpha[hd] * l_s[mine[hd], :] + jnp.sum(probs[hd], axis=-1, keepdims=True)
            acc_s[mine[hd], :] = alpha[hd] * acc_s[mine[hd], :] + _mm(probs[hd].astype(cdt), v)
            m_s[mine[hd], :] = m_new[hd]

    @pl.when(kt == 0)
    def _():
        q = q_ref[0]
        lane = lax.broadcasted_iota(jnp.int32, (tq, HEAD_W), 1)
        parts = []
        for hd in range(N_HEADS):
            qh = q[:, hd * HEAD_W:(hd + 1) * HEAD_W]
            zero = jnp.zeros_like(qh)
            parts += [jnp.where(lane < HD_A, qh, zero), jnp.where(lane >= HD_A, qh, zero)]
        qz_s[...] = jnp.concatenate(parts, axis=0)
        m_s[...] = jnp.full((rows, LANES), NEG_INF, F32)
        l_s[...] = jnp.zeros((rows, LANES), F32)
        acc_s[...] = jnp.zeros((rows, HEAD_W), F32)
        fold(kn_ref, vn_ref, True)

    fold(kp_ref, vp_ref, False)

    @pl.when(kt == pl.num_programs(1) - 1)
    def _():
        lam = _diff_lambda(lw_ref, lam_init)
        o = acc_s[...] / l_s[...]
        outs = []
        for hd in range(N_HEADS):
            base = hd * 2 * tq
            oh = o[base:base + tq] - lam * o[base + tq:base + 2 * tq]
            r = lax.rsqrt(jnp.mean(oh * oh, axis=-1, keepdims=True) + EPS)
            outs.append((oh * r) * sub_ref[...] * (1.0 - lam_init))
        o_ref[0] = jnp.concatenate(outs, axis=1).astype(o_ref.dtype)


def _diff_sample(lw, sub_row, q, k_new, v_new, k_past, v_past, layer, *, key_rows, lam_init):
    b, tq, _ = q.shape
    past_rows = k_past.shape[2]
    grid = (b, past_rows // key_rows)
    new_spec = pl.BlockSpec((1, 1, tq * N_HEADS, HEAD_W), lambda bi, kt: (layer, bi, 0, 0))
    past_spec = pl.BlockSpec((1, 1, key_rows, HEAD_W), lambda bi, kt: (layer, bi, kt, 0))
    rows = N_HEADS * 2 * tq
    return pl.pallas_call(
        functools.partial(_diff_sample_kernel, tq=tq, lam_init=lam_init),
        grid=grid,
        in_specs=[
            pl.BlockSpec(lw.shape, lambda bi, kt: (0, 0)),
            pl.BlockSpec((1, HEAD_W), lambda bi, kt: (0, 0)),
            pl.BlockSpec((1, tq, W_BRANCH), lambda bi, kt: (bi, 0, 0)),
            new_spec, new_spec, past_spec, past_spec,
        ],
        out_specs=pl.BlockSpec((1, tq, W_BRANCH), lambda bi, kt: (bi, 0, 0)),
        out_shape=jax.ShapeDtypeStruct((b, tq, W_BRANCH), q.dtype),
        scratch_shapes=[
            pltpu.VMEM((rows, HEAD_W), q.dtype),
            pltpu.VMEM((rows, LANES), F32),
            pltpu.VMEM((rows, LANES), F32),
            pltpu.VMEM((rows, HEAD_W), F32),
        ],
        compiler_params=_cparams(("arbitrary", "arbitrary")),
        name="diff_sample",
    )(lw, sub_row, q, k_new, v_new, k_past, v_past)


def _neg_softplus(z):
    return -(jnp.maximum(z, 0.0) + jnp.log1p(jnp.exp(-jnp.abs(z))))


def _suffix_sums(x, upper):
    if upper.dtype == F32:
        return _mm(x, upper)
    hi = x.astype(BF16)
    lo = (x - hi.astype(F32)).astype(BF16)
    return (jnp.dot(hi, upper, preferred_element_type=F32)
            + jnp.dot(lo, upper, preferred_element_type=F32))


def _upper(n, dtype=BF16):
    j = lax.broadcasted_iota(jnp.int32, (n, n), 0)
    s = lax.broadcasted_iota(jnp.int32, (n, n), 1)
    return jnp.where(j >= s, 1.0, 0.0).astype(dtype)


def _stick_prompt_kernel(q_ref, kd_ref, vd_ref, kq_ref, vq_ref, kp_ref, vp_ref, o_ref, up_s, c_s, acc_s,
                         *, tq):
    i = pl.program_id(2)
    has_prev = jnp.where(i > 0, 1.0, 0.0)
    row = lax.broadcasted_iota(jnp.int32, (tq, tq), 0)
    col = lax.broadcasted_iota(jnp.int32, (tq, tq), 1)
    earlier = col < row
    q = q_ref[0]
    up_s[...] = _upper(tq)
    z = lax.dot_general(q, kd_ref[0], NT_DIMS, preferred_element_type=F32)
    zq = lax.dot_general(q, kq_ref[0], NT_DIMS, preferred_element_type=F32)
    lneg = jnp.where(earlier, _neg_softplus(z), 0.0)
    lnq = _neg_softplus(zq)
    cs = _suffix_sums(lneg, up_s[...])
    csq = _suffix_sums(lnq, up_s[...])
    c0 = jnp.sum(lneg, axis=-1, keepdims=True)
    a = jnp.where(earlier, jnp.exp(z + cs), 0.0)
    aq = jnp.exp(zq + csq + c0) * has_prev
    acc_s[...] = (jnp.dot(a.astype(BF16), vd_ref[0], preferred_element_type=F32)
                  + jnp.dot(aq.astype(BF16), vq_ref[0], preferred_element_type=F32))
    c1 = c0 + jnp.sum(lnq, axis=-1, keepdims=True) * has_prev
    c_s[...] = c1

    def cond(state):
        j, c_max = state
        return jnp.logical_and(j >= 0, c_max > STICK_DEAD_LOG)

    def body(state):
        j, _ = state
        start = pl.multiple_of(j * tq, tq)
        k = kp_ref[0, pl.ds(start, tq), :]
        v = vp_ref[0, pl.ds(start, tq), :]
        zj = lax.dot_general(q_ref[0], k, NT_DIMS, preferred_element_type=F32)
        ln = _neg_softplus(zj)
        csj = _suffix_sums(ln, up_s[...])
        c = c_s[...]
        aj = jnp.exp(zj + csj + c)
        acc_s[...] += jnp.dot(aj.astype(BF16), v, preferred_element_type=F32)
        c_new = c + jnp.sum(ln, axis=-1, keepdims=True)
        c_s[...] = c_new
        return j - 1, jnp.max(c_new)

    lax.while_loop(cond, body, (i - 2, jnp.max(c1)))
    o_ref[0] = acc_s[...].astype(o_ref.dtype)


def _stick_prompt(q, k, v, *, tq):
    b, t, _ = q.shape
    grid = (b, N_HEADS, t // tq)
    tile = pl.BlockSpec((1, tq, HEAD_W), lambda bi, h, i: (bi, i, h))
    before = pl.BlockSpec((1, tq, HEAD_W), lambda bi, h, i: (bi, jnp.maximum(i - 1, 0), h))
    whole = pl.BlockSpec((1, t, HEAD_W), lambda bi, h, i: (bi, 0, h))
    return pl.pallas_call(
        functools.partial(_stick_prompt_kernel, tq=tq),
        grid=grid,
        in_specs=[tile, tile, tile, before, before, whole, whole],
        out_specs=tile,
        out_shape=jax.ShapeDtypeStruct((b, t, W_BRANCH), BF16),
        scratch_shapes=[
            pltpu.VMEM((tq, tq), BF16),
            pltpu.VMEM((tq, 1), F32),
            pltpu.VMEM((tq, HEAD_W), F32),
        ],
        compiler_params=_cparams(("arbitrary", "arbitrary", "arbitrary")),
        name="stick_prompt",
    )(q, k, v, k, v, k, v)


def _stick_recent_kernel(q_ref, kd_ref, vd_ref, kq_ref, vq_ref, o_ref, rest_ref, *, tq):
    i = pl.program_id(1)
    has_prev = jnp.where(i > 0, 1.0, 0.0)
    row = lax.broadcasted_iota(jnp.int32, (tq, tq), 0)
    col = lax.broadcasted_iota(jnp.int32, (tq, tq), 1)
    earlier = col < row
    upper = _upper(tq)
    outs, worst = [], None
    for hd in range(N_HEADS):
        sl = slice(hd * HEAD_W, (hd + 1) * HEAD_W)
        q = q_ref[0, :, sl]
        z = lax.dot_general(q, kd_ref[0, :, sl], NT_DIMS, preferred_element_type=F32)
        zq = lax.dot_general(q, kq_ref[0, :, sl], NT_DIMS, preferred_element_type=F32)
        lneg = jnp.where(earlier, _neg_softplus(z), 0.0)
        lnq = _neg_softplus(zq)
        cs = _suffix_sums(lneg, upper)
        csq = _suffix_sums(lnq, upper)
        c0 = jnp.sum(lneg, axis=-1, keepdims=True)
        a = jnp.where(earlier, jnp.exp(z + cs), 0.0)
        aq = jnp.exp(zq + csq + c0) * has_prev
        outs.append(jnp.dot(a.astype(BF16), vd_ref[0, :, sl], preferred_element_type=F32)
                    + jnp.dot(aq.astype(BF16), vq_ref[0, :, sl], preferred_element_type=F32))
        c_max = jnp.max(c0 + jnp.sum(lnq, axis=-1, keepdims=True) * has_prev)
        worst = c_max if worst is None else jnp.maximum(worst, c_max)
    o_ref[0] = jnp.concatenate(outs, axis=1).astype(o_ref.dtype)
    rest_ref[...] = jnp.full(rest_ref.shape, jnp.where(i >= 2, worst, 2.0 * STICK_DEAD_LOG), F32)


def _stick_recent(q, k, v, *, tq):
    b, t, _ = q.shape
    nt = t // tq
    tile = pl.BlockSpec((1, tq, W_BRANCH), lambda bi, i: (bi, i, 0))
    before = pl.BlockSpec((1, tq, W_BRANCH), lambda bi, i: (bi, jnp.maximum(i - 1, 0), 0))
    return pl.pallas_call(
        functools.partial(_stick_recent_kernel, tq=tq),
        grid=(b, nt),
        in_specs=[tile, tile, tile, before, before],
        out_specs=[tile, pl.BlockSpec((1, 1, 8, LANES), lambda bi, i: (bi, i, 0, 0))],
        out_shape=[jax.ShapeDtypeStruct((b, t, W_BRANCH), BF16),
                   jax.ShapeDtypeStruct((b, nt, 8, LANES), F32)],
        compiler_params=_cparams(("arbitrary", "arbitrary")),
        name="stick_recent",
    )(q, k, v, k, v)


def _stick_sample_kernel(q_ref, kn_ref, vn_ref, kp_ref, vp_ref, o_ref, rest_ref,
                         q_s, up_s, c_s, acc_s, live_s, *, tq, sub_rows):
    kt = pl.program_id(1)
    rows = N_HEADS * tq

    def fold(k, v, new):
        n = k.shape[0]
        z = _mm(q_s[...], k.astype(q_s.dtype), NT_DIMS)
        row = lax.broadcasted_iota(jnp.int32, (rows, n), 0)
        col = lax.broadcasted_iota(jnp.int32, (rows, n), 1)
        keep = (row >> _log2(tq)) == (col & (N_HEADS - 1))
        if new:
            keep = jnp.logical_and(keep, (col >> 2) < (row & (tq - 1)))
        ln = jnp.where(keep, _neg_softplus(z), 0.0)
        cs = _suffix_sums(ln, _upper(n, up_s.dtype) if new else up_s[...])
        c = c_s[...]
        a = jnp.where(keep, jnp.exp(z + cs + c), 0.0)
        acc_s[...] += _mm(a.astype(q_s.dtype), v.astype(q_s.dtype))
        c_new = c + jnp.sum(ln, axis=-1, keepdims=True)
        c_s[...] = c_new
        live_s[0] = (jnp.max(c_new) > STICK_DEAD_LOG).astype(jnp.int32)

    @pl.when(kt == 0)
    def _():
        q = q_ref[0]
        q_s[...] = jnp.concatenate([q[:, hd * HEAD_W:(hd + 1) * HEAD_W] for hd in range(N_HEADS)], axis=0)
        up_s[...] = _upper(sub_rows, up_s.dtype)
        c_s[...] = jnp.zeros((rows, 1), F32)
        acc_s[...] = jnp.zeros((rows, HEAD_W), F32)
        fold(kn_ref[0, 0], vn_ref[0, 0], True)

    n_sub = kp_ref.shape[2] // sub_rows
    for sb in reversed(range(n_sub)):
        @pl.when(live_s[0] > 0)
        def _(sb=sb):
            sl = slice(sb * sub_rows, (sb + 1) * sub_rows)
            fold(kp_ref[0, 0, sl, :], vp_ref[0, 0, sl, :], False)

    @pl.when(kt == pl.num_programs(1) - 1)
    def _():
        acc = acc_s[...]
        o_ref[0] = jnp.concatenate([acc[hd * tq:(hd + 1) * tq] for hd in range(N_HEADS)],
                                   axis=1).astype(o_ref.dtype)
        rest_ref[...] = jnp.full(rest_ref.shape, jnp.max(c_s[...]), F32)


def _stick_sample(q, k_new, v_new, k_past, v_past, layer, *, key_rows, sub_rows, newest_only):
    b, tq, _ = q.shape
    n_blocks = k_past.shape[2] // key_rows
    grid = (b, 1 if newest_only else n_blocks)
    new_spec = pl.BlockSpec((1, 1, tq * N_HEADS, HEAD_W), lambda bi, kt: (layer, bi, 0, 0))
    past_spec = pl.BlockSpec((1, 1, key_rows, HEAD_W), lambda bi, kt: (layer, bi, n_blocks - 1 - kt, 0))
    rows = N_HEADS * tq
    return pl.pallas_call(
        functools.partial(_stick_sample_kernel, tq=tq, sub_rows=sub_rows),
        grid=grid,
        in_specs=[pl.BlockSpec((1, tq, W_BRANCH), lambda bi, kt: (bi, 0, 0)),
                  new_spec, new_spec, past_spec, past_spec],
        out_specs=[pl.BlockSpec((1, tq, W_BRANCH), lambda bi, kt: (bi, 0, 0)),
                   pl.BlockSpec((1, 8, LANES), lambda bi, kt: (bi, 0, 0))],
        out_shape=[jax.ShapeDtypeStruct((b, tq, W_BRANCH), q.dtype),
                   jax.ShapeDtypeStruct((b, 8, LANES), F32)],
        scratch_shapes=[
            pltpu.VMEM((rows, HEAD_W), q.dtype),
            pltpu.VMEM((sub_rows, sub_rows), q.dtype),
            pltpu.VMEM((rows, 1), F32),
            pltpu.VMEM((rows, HEAD_W), F32),
            pltpu.SMEM((1,), jnp.int32),
        ],
        compiler_params=_cparams(("arbitrary", "arbitrary")),
        name="stick_sample",
    )(q, k_new, v_new, k_past, v_past)


def _router_gates_t(logits_t, bias_col):
    n = EXPERTS_PER_GROUP
    aff = jax.nn.sigmoid(logits_t)
    sel = aff + bias_col
    pos = [sel[n * k:n * (k + 1)] for k in range(n)]
    in_top2 = []
    for j in range(n):
        rank = jnp.zeros(pos[j].shape, jnp.int32)
        for i in range(n):
            if i != j:
                ahead = (pos[i] >= pos[j]) if i < j else (pos[i] > pos[j])
                rank = rank + jnp.where(ahead, 1, 0)
        in_top2.append(rank < 2)
    score = sum(jnp.where(in_top2[j], pos[j], 0.0) for j in range(n))
    grp = lax.broadcasted_iota(jnp.int32, score.shape, 0)
    beaten = jnp.zeros(score.shape, jnp.int32)
    for g2 in range(N_GROUPS):
        other = score[g2:g2 + 1]
        ahead = jnp.logical_or(other > score, jnp.logical_and(other == score, g2 < grp))
        beaten = beaten + jnp.where(ahead, 1, 0)
    chosen = beaten == 0
    w = [jnp.where(jnp.logical_and(in_top2[j], chosen), aff[n * j:n * (j + 1)], 0.0) for j in range(n)]
    total = jnp.sum(sum(w), axis=0, keepdims=True)
    return jnp.concatenate([wj / total for wj in w], axis=0)


def _mixout_kernel(x_ref, oa_ref, ob_ref, ga_ref, gb_ref, g1_ref, sc_ref, sh_ref, gain_ref,
                   wa_ref, wb_ref, wo_ref, wr_ref, br_ref, x1_ref, h2_ref, gates_ref):
    bb, tt, d = x_ref.shape
    rows = bb * tt
    ya = _mm(oa_ref[...].reshape(rows, W_BRANCH), wa_ref[...])
    yb = _mm(ob_ref[...].reshape(rows, W_BRANCH), wb_ref[...])
    y = (ga_ref[...].reshape(rows, d).astype(F32) * ya + gb_ref[...].reshape(rows, d).astype(F32) * yb)
    mix = _mm(y.astype(wo_ref.dtype), wo_ref[...])
    x1 = x_ref[...] + g1_ref[...] * mix.reshape(bb, tt, d)
    x1_ref[...] = x1
    r = lax.rsqrt(jnp.mean(x1 * x1, axis=-1, keepdims=True) + EPS)
    h2 = ((x1 * r) * gain_ref[...] * (1.0 + sc_ref[...]) + sh_ref[...]).reshape(rows, d)
    h2_ref[...] = h2.reshape(bb, tt, d).astype(h2_ref.dtype)
    logits_t = _mm(wr_ref[...], h2.astype(wr_ref.dtype), NT_DIMS)
    gates_t = _router_gates_t(logits_t, br_ref[...])
    pad = jnp.zeros((LANES - N_EXPERTS, rows), F32)
    gates_ref[...] = jnp.concatenate([gates_t, pad], axis=0).T.reshape(bb, tt, LANES)


def _mixout(x, oa, ob, ga, gb, g1, sc2, sh2, gain, wa, wb, wo, wr, br, bb, tt, layer):
    b, t, d = x.shape
    grid = (b // bb, t // tt)
    xmap = lambda i, j: (i, j, 0)
    modspec = pl.BlockSpec((bb, 1, d), lambda i, j: (i, 0, 0))

    def full(a):
        return pl.BlockSpec(a.shape, lambda i, j: (0,) * a.ndim)

    def of_layer(a):
        return pl.BlockSpec((None,) + a.shape[1:], lambda i, j: (layer,) + (0,) * (a.ndim - 1))

    return pl.pallas_call(
        _mixout_kernel,
        grid=grid,
        in_specs=[
            pl.BlockSpec((bb, tt, d), xmap),
            pl.BlockSpec((bb, tt, W_BRANCH), xmap), pl.BlockSpec((bb, tt, W_BRANCH), xmap),
            pl.BlockSpec((bb, tt, d), xmap), pl.BlockSpec((bb, tt, d), xmap),
            modspec, modspec, modspec,
            pl.BlockSpec((1, 1, d), lambda i, j: (0, 0, 0)),
            of_layer(wa), of_layer(wb), of_layer(wo), full(wr), full(br),
        ],
        out_specs=[pl.BlockSpec((bb, tt, d), xmap), pl.BlockSpec((bb, tt, d), xmap),
                   pl.BlockSpec((bb, tt, LANES), xmap)],
        out_shape=[jax.ShapeDtypeStruct((b, t, d), F32), jax.ShapeDtypeStruct((b, t, d), BF16),
                   jax.ShapeDtypeStruct((b, t, LANES), F32)],
        compiler_params=_cparams(("arbitrary", "arbitrary")),
        name="mixout",
    )(x, oa, ob, ga, gb, g1, sc2, sh2, gain.reshape(1, 1, d), wa, wb, wo, wr, br)


def _moe_kernel(h_ref, gates_ref, x1_ref, g2_ref, gain_ref, wg_ref, wu_ref, wd_ref, o_ref, acc_s,
                *, final_norm):
    bb, tt, d = h_ref.shape
    rows = bb * tt
    grp = pl.program_id(2)
    h = h_ref[...].reshape(rows, d)
    gates = gates_ref[...].reshape(rows, LANES)
    lane = lax.broadcasted_iota(jnp.int32, (rows, LANES), 1)

    @pl.when(grp == 0)
    def _():
        acc_s[...] = jnp.zeros(acc_s.shape, F32)

    hidden = []
    for k in range(EXPERTS_PER_GROUP):
        g = jnp.dot(h, wg_ref[k], preferred_element_type=F32)
        u = jnp.dot(h, wu_ref[k], preferred_element_type=F32)
        gate = jnp.sum(jnp.where(lane == EXPERTS_PER_GROUP * k + grp, gates, 0.0), axis=-1, keepdims=True)
        hidden.append(((g * jax.nn.sigmoid(g)) * u * gate).astype(BF16))
    acc_s[...] += jnp.dot(jnp.concatenate(hidden, axis=1), wd_ref[...], preferred_element_type=F32)

    @pl.when(grp == N_GROUPS - 1)
    def _():
        x2 = x1_ref[...] + g2_ref[...] * acc_s[...].reshape(bb, tt, d)
        if final_norm:
            r = lax.rsqrt(jnp.mean(x2 * x2, axis=-1, keepdims=True) + EPS)
            x2 = (x2 * r) * gain_ref[...]
        o_ref[...] = x2


def _moe(h2, gates, x1, g2, gain, wg, wu, wd, bb, tt, layer, final_norm):
    b, t, d = x1.shape
    grid = (b // bb, t // tt, N_GROUPS)
    xmap = lambda i, j, e: (i, j, 0)
    n = EXPERTS_PER_GROUP
    return pl.pallas_call(
        functools.partial(_moe_kernel, final_norm=final_norm),
        grid=grid,
        in_specs=[
            pl.BlockSpec((bb, tt, d), xmap),
            pl.BlockSpec((bb, tt, LANES), xmap),
            pl.BlockSpec((bb, tt, d), xmap),
            pl.BlockSpec((bb, 1, d), lambda i, j, e: (i, 0, 0)),
            pl.BlockSpec((1, 1, d), lambda i, j, e: (0, 0, 0)),
            pl.BlockSpec((None, n, d, D_EXPERT), lambda i, j, e: (layer, e, 0, 0)),
            pl.BlockSpec((None, n, d, D_EXPERT), lambda i, j, e: (layer, e, 0, 0)),
            pl.BlockSpec((None, None, n * D_EXPERT, d), lambda i, j, e: (layer, e, 0, 0)),
        ],
        out_specs=pl.BlockSpec((bb, tt, d), xmap),
        out_shape=jax.ShapeDtypeStruct((b, t, d), F32),
        scratch_shapes=[pltpu.VMEM((bb * tt, d), F32)],
        compiler_params=_cparams(("arbitrary", "arbitrary", "arbitrary")),
        name="moe",
    )(h2, gates, x1, g2, gain.reshape(1, 1, d), wg, wu, wd)


def _rope_tables(pos):
    half = HD_A // 2
    inv = ROPE_THETA ** (-jnp.arange(half, dtype=F32) / half)
    lane = jnp.arange(LANES)
    ang = pos.astype(F32)[:, None] * inv[lane % half][None, :]
    sign = jnp.where(lane % HD_A < half, -1.0, 1.0).astype(F32)
    return jnp.cos(ang), jnp.sin(ang) * sign[None, :]


def _trunk(x, mod, pos, past, p, *, row_block, moe_rows, attn_tile):
    b, t, d = x.shape
    bb, tt = row_block
    depth = p["norm_mix"].shape[0]
    prompt = past is None
    cos, sin = _rope_tables(pos)
    caches = None
    for l in range(depth):
        wl = p["f32"] if (not prompt and l == 0) else p["bf16"]
        sh1, sc1, g1, sh2, sc2, g2 = [mod[l, :, i][:, None, :] for i in range(6)]
        outs = _inproj(x, p["norm_mix"][l], sc1, sh1, cos, sin, wl["w_in"], bb, tt, l, depth,
                       caches, feature_major=prompt)
        caches = outs[:4]
        qa, qb, kab, vab, kbb, vbb, ga, gb = outs[4:]
        lam_init = 0.8 - 0.6 * math.exp(-0.3 * l)
        lw, sub = p["a_lambda"][l], p["a_subln"][l]
        if prompt:
            oa = _diff_prompt(lw, sub[:, None], qa, kab, vab, tq=attn_tile, lam_init=lam_init)
            stick_tile = min(t, 256)
            ob, rest = _stick_recent(qb, kbb, vbb, tq=stick_tile)
            ob = lax.cond(jnp.max(rest) > STICK_DEAD_LOG,
                          lambda: _stick_prompt(qb, kbb, vbb, tq=stick_tile), lambda: ob)
        else:
            oa = _diff_sample(lw, sub[None, :], qa, caches[0], caches[1], past[0], past[1], l,
                              key_rows=attn_tile, lam_init=lam_init)
            stick = functools.partial(_stick_sample, qb, caches[2], caches[3], past[2], past[3], l,
                                      key_rows=attn_tile, sub_rows=min(attn_tile, 512))
            ob, rest = stick(newest_only=True)
            if past[2].shape[2] > attn_tile:
                ob = lax.cond(jnp.max(rest) > STICK_DEAD_LOG,
                              lambda: stick(newest_only=False)[0], lambda: ob)
        x1, h2, gates = _mixout(x, oa, ob, ga, gb, g1, sc2, sh2, p["norm_ffn"][l],
                                wl["w_proj_a"], wl["w_proj_b"], wl["w_out"],
                                wl["w_router"], p["b_router"], bb, tt, l)
        x = _moe(h2, gates, x1, g2, p["norm_final"], p["w_e_gate"], p["w_e_up"],
                 p["w_e_down"], bb, moe_rows, l, final_norm=(l == depth - 1))
    return (x,) + tuple(c.reshape(depth, b, t, N_HEADS, HEAD_W) for c in caches)


def kernel(x_prompt, x_sample, cache_a_k, cache_a_v, cache_b_k, cache_b_v, c_prompt, c_sample,
           w_in, w_proj_a, w_proj_b, w_out, a_lambda, a_subln, w_ada, b_ada,
           norm_mix, norm_ffn, norm_final, w_router, b_router, w_e_gate, w_e_up, w_e_down):
    d = x_prompt.shape[-1]
    bp, tp = x_prompt.shape[:2]
    bs, ts = x_sample.shape[:2]
    depth = w_in.shape[0]
    past_len = cache_a_k.shape[2]
    assert d == D_MODEL and past_len % CHUNK == 0 and ts <= CHUNK

    wr = w_router.T.reshape(N_GROUPS, EXPERTS_PER_GROUP, d).transpose(1, 0, 2).reshape(N_EXPERTS, d)
    br = b_router.reshape(N_GROUPS, EXPERTS_PER_GROUP).T.reshape(N_EXPERTS, 1)
    mixer_f32 = dict(w_in=w_in, w_proj_a=w_proj_a, w_proj_b=w_proj_b, w_out=w_out, w_router=wr)
    p = dict(
        f32=mixer_f32, bf16={k: v.astype(BF16) for k, v in mixer_f32.items()},
        a_lambda=a_lambda, a_subln=a_subln,
        norm_mix=norm_mix, norm_ffn=norm_ffn, norm_final=norm_final, b_router=br,
        w_e_gate=w_e_gate.astype(BF16), w_e_up=w_e_up.astype(BF16),
        w_e_down=w_e_down.astype(BF16).reshape(depth, N_GROUPS, EXPERTS_PER_GROUP * D_EXPERT, d),
    )

    n_c = bp + bs
    rows = -(-n_c // 8) * 8
    c_all = jnp.zeros((rows, d), F32).at[:bp].set(c_prompt).at[bp:n_c].set(c_sample)
    mod = _ada_mod(c_all, w_ada, b_ada).reshape(depth, rows, 6, d)

    pos_p = jnp.arange(tp, dtype=jnp.int32)
    pos_s = past_len + jnp.arange(ts, dtype=jnp.int32)
    tile_p = min(tp, 512)
    out_p = _trunk(x_prompt, mod[:, :bp], pos_p, None, p,
                   row_block=(1, tile_p), moe_rows=min(tp, 1024), attn_tile=tile_p)
    past = tuple(c.reshape(depth, bs, past_len * N_HEADS, HEAD_W)
                 for c in (cache_a_k, cache_a_v, cache_b_k, cache_b_v))
    out_s = _trunk(x_sample, mod[:, bp:n_c], pos_s, past, p,
                   row_block=(bs, ts), moe_rows=ts, attn_tile=min(past_len * N_HEADS, 2048))
    return (out_p[0], out_s[0]) + out_p[1:] + out_s[1:]
```

```python
import functools
import math

import jax
import jax.numpy as jnp
from jax import lax
from jax.experimental import pallas as pl
from jax.experimental.pallas import tpu as pltpu

D_MODEL = 1024
CHUNK = 64
N_HEADS = 4
HEAD_W = 128
HD_A = 64
HD_B = 128
W_BRANCH = N_HEADS * HEAD_W
N_EXPERTS = 16
EXPERTS_PER_GROUP = 4
N_GROUPS = N_EXPERTS // EXPERTS_PER_GROUP
D_EXPERT = 256
ROPE_THETA = 10000.0
EPS = 1e-6
NEG_INF = -1e30
STICK_DEAD_LOG = -104.0
LOG2_E = 1.4426950408889634
LANES = 128
VMEM_LIMIT = 56 * 1024 * 1024

F32 = jnp.float32
BF16 = jnp.bfloat16
NT_DIMS = (((1,), (1,)), ((), ()))


def _log2(n):
    assert n > 0 and n & (n - 1) == 0, n
    return n.bit_length() - 1


def _mm(a, b, dims=None):
    assert a.dtype == b.dtype, (a.dtype, b.dtype)
    prec = lax.Precision.HIGHEST if a.dtype == F32 else None
    if dims is None:
        return jnp.dot(a, b, preferred_element_type=F32, precision=prec)
    return lax.dot_general(a, b, dims, preferred_element_type=F32, precision=prec)


def _cparams(sem):
    return pltpu.CompilerParams(dimension_semantics=sem, vmem_limit_bytes=VMEM_LIMIT)


def _ada_kernel(c_ref, w_ref, b_ref, o_ref):
    c = c_ref[...]
    a = c * jax.nn.sigmoid(c)

    @pl.when(pl.program_id(0) == 0)
    def _():
        o_ref[0] = _mm(a, w_ref[0]) + b_ref[0]

    @pl.when(pl.program_id(0) > 0)
    def _():
        o_ref[0] = _mm(a.astype(BF16), w_ref[0].astype(BF16)) + b_ref[0]


def _ada_mod(c_all, w_ada, b_ada, tn=1536):
    depth, d, n6 = w_ada.shape
    rows = c_all.shape[0]
    return pl.pallas_call(
        _ada_kernel,
        grid=(depth, n6 // tn),
        in_specs=[
            pl.BlockSpec((rows, d), lambda l, j: (0, 0)),
            pl.BlockSpec((1, d, tn), lambda l, j: (l, 0, j)),
            pl.BlockSpec((1, 1, tn), lambda l, j: (l, 0, j)),
        ],
        out_specs=pl.BlockSpec((1, rows, tn), lambda l, j: (l, 0, j)),
        out_shape=jax.ShapeDtypeStruct((depth, rows, n6), F32),
        compiler_params=_cparams(("arbitrary", "arbitrary")),
        name="ada_mod",
    )(c_all, w_ada, b_ada.reshape(depth, 1, n6))


def _rope128(x, cos, sin_signed, lane):
    fwd = pltpu.roll(x, LANES - HD_A // 2, 1)
    bwd = pltpu.roll(x, HD_A // 2, 1)
    partner = jnp.where((lane & (HD_A - 1)) < HD_A // 2, fwd, bwd)
    return x * cos + partner * sin_signed


def _inproj_kernel(*refs, n_alias, feature_major):
    (x_ref, g_ref, sc_ref, sh_ref, cos_ref, sin_ref, w_ref) = refs[:7]
    outs = refs[7 + n_alias:]
    (ka_c, va_c, kb_c, vb_c, qa_ref, qb_ref, kab_ref, vab_ref, kbb_ref, vbb_ref, ga_ref, gb_ref) = outs
    bb, tt, d = x_ref.shape
    rows = bb * tt
    x = x_ref[...]
    r = lax.rsqrt(jnp.mean(x * x, axis=-1, keepdims=True) + EPS)
    h = (x * r) * g_ref[...] * (1.0 + sc_ref[...]) + sh_ref[...]
    hb = h.reshape(rows, d).astype(w_ref.dtype)
    cos = cos_ref[...]
    sin = sin_ref[...]
    if bb > 1:
        cos = jnp.concatenate([cos] * bb, axis=0)
        sin = jnp.concatenate([sin] * bb, axis=0)
    lane = lax.broadcasted_iota(jnp.int32, (rows, LANES), 1)

    def proj(c0, width):
        return _mm(hb, w_ref[:, c0:c0 + width])

    def put(ref, val):
        ref[...] = val.reshape(ref.shape).astype(ref.dtype)

    def put_cache(ref, heads):
        for b in range(bb):
            for hd in range(N_HEADS):
                ref[0, b, pl.ds(hd, tt, stride=N_HEADS), :] = heads[hd][b * tt:(b + 1) * tt]

    def put_feature_major(ref, heads):
        for hd in range(N_HEADS):
            ref[0, hd, 0] = heads[hd].T.astype(ref.dtype)

    def split(val):
        return [val[:, hd * HEAD_W:(hd + 1) * HEAD_W] for hd in range(N_HEADS)]

    w = W_BRANCH
    qa = [_rope128(v, cos, sin, lane) * (HD_A ** -0.5 * LOG2_E) for v in split(proj(0, w))]
    ka = [_rope128(v, cos, sin, lane) for v in split(proj(w, w))]
    va = split(proj(2 * w, w))
    put_cache(ka_c, ka)
    put_cache(va_c, va)
    put(kab_ref, jnp.concatenate(ka, axis=1))
    if feature_major:
        put_feature_major(qa_ref, qa)
        put_feature_major(vab_ref, va)
    else:
        put(qa_ref, jnp.concatenate(qa, axis=1))
        put(vab_ref, jnp.concatenate(va, axis=1))
    put(qb_ref, proj(3 * w, w) * (HD_B ** -0.5))
    kb = proj(4 * w, w)
    put_cache(kb_c, split(kb))
    put(kbb_ref, kb)
    vb = proj(5 * w, w)
    put_cache(vb_c, split(vb))
    put(vbb_ref, vb)
    put(ga_ref, jax.nn.sigmoid(proj(6 * w, d)))
    put(gb_ref, jax.nn.sigmoid(proj(6 * w + d, d)))


def _inproj(x, gain, sc, sh, cos, sin, w, bb, tt, layer, depth, caches, feature_major):
    b, t, d = x.shape
    grid = (b // bb, t // tt)
    xmap = lambda i, j: (i, j, 0)
    modspec = pl.BlockSpec((bb, 1, d), lambda i, j: (i, 0, 0))
    tabspec = pl.BlockSpec((tt, LANES), lambda i, j: (j, 0))
    n_alias = 0 if caches is None else 4
    act = w.dtype

    def out(width, dtype):
        return (pl.BlockSpec((bb, tt, width), xmap), jax.ShapeDtypeStruct((b, t, width), dtype))

    def out_t():
        return (pl.BlockSpec((1, N_HEADS, 1, HEAD_W, tt), lambda i, j: (i, 0, j, 0, 0)),
                jax.ShapeDtypeStruct((b, N_HEADS, t // tt, HEAD_W, tt), BF16))

    cache = (pl.BlockSpec((1, bb, tt * N_HEADS, HEAD_W), lambda i, j: (layer, i, j, 0)),
             jax.ShapeDtypeStruct((depth, b, t * N_HEADS, HEAD_W), F32))
    outs = [cache, cache, cache, cache,
            out_t() if feature_major else out(W_BRANCH, act),
            out(W_BRANCH, act),
            out(W_BRANCH, BF16),
            out_t() if feature_major else out(W_BRANCH, BF16),
            out(W_BRANCH, BF16), out(W_BRANCH, BF16),
            out(d, act), out(d, act)]
    in_specs = [
        pl.BlockSpec((bb, tt, d), xmap),
        pl.BlockSpec((1, 1, d), lambda i, j: (0, 0, 0)),
        modspec, modspec, tabspec, tabspec,
        pl.BlockSpec((None,) + w.shape[1:], lambda i, j: (layer, 0, 0)),
    ] + [pl.BlockSpec(memory_space=pl.ANY)] * n_alias
    args = (x, gain.reshape(1, 1, d), sc, sh, cos, sin, w) + (() if caches is None else tuple(caches))
    return pl.pallas_call(
        functools.partial(_inproj_kernel, n_alias=n_alias, feature_major=feature_major),
        grid=grid,
        in_specs=in_specs,
        out_specs=[o[0] for o in outs],
        out_shape=[o[1] for o in outs],
        input_output_aliases={7 + k: k for k in range(n_alias)},
        compiler_params=_cparams(("arbitrary", "arbitrary")),
        name="inproj",
    )(*args)


def _diff_lambda(lw_ref, lam_init):
    lw = lw_ref[...]
    return (jnp.exp(jnp.sum(lw[0:1] * lw[1:2], axis=-1, keepdims=True))
            - jnp.exp(jnp.sum(lw[2:3] * lw[3:4], axis=-1, keepdims=True)) + lam_init)


def _diff_prompt_kernel(lw_ref, sub_ref, qt_ref, k_ref, vt_ref, o_ref, qz_s, s0_s, s1_s, m_s, l_s, acc_s,
                        *, tq, lam_init):
    i = pl.program_id(2)
    cols = 2 * tq
    qt = qt_ref[0, 0, 0]
    feat = lax.broadcasted_iota(jnp.int32, (HEAD_W, tq), 0)
    zero = jnp.zeros_like(qt)
    qz_s[...] = jnp.concatenate([jnp.where(feat < HD_A, qt, zero), jnp.where(feat >= HD_A, qt, zero)], axis=1)
    m_s[...] = jnp.full((1, cols), NEG_INF, F32)
    l_s[...] = jnp.zeros((1, cols), F32)
    acc_s[...] = jnp.zeros((HEAD_W, cols), F32)

    def scores(j, s_ref):
        start = pl.multiple_of(j * tq, tq)
        s_ref[...] = jnp.dot(k_ref[0, pl.ds(start, tq), :], qz_s[...], preferred_element_type=F32)

    def absorb(j, s_ref, masked):
        s = s_ref[...]
        if masked:
            key = lax.broadcasted_iota(jnp.int32, (tq, cols), 0)
            qry = lax.broadcasted_iota(jnp.int32, (tq, cols), 1)
            qry = jnp.where(qry >= tq, qry - tq, qry)
            s = jnp.where((key >> 6) <= (qry >> 6), s, NEG_INF)
        m_prev = m_s[...]
        m_new = jnp.maximum(m_prev, jnp.max(s, axis=0, keepdims=True))
        alpha = jnp.exp2(m_prev - m_new)
        p = jnp.exp2(s - m_new)
        l_s[...] = alpha * l_s[...] + jnp.sum(p, axis=0, keepdims=True)
        acc_s[...] = alpha * acc_s[...] + jnp.dot(vt_ref[0, 0, j], p.astype(BF16), preferred_element_type=F32)
        m_s[...] = m_new

    scores(0, s0_s)

    def pair(pi, carry):
        j = 2 * pi
        scores(j + 1, s1_s)
        absorb(j, s0_s, False)
        scores(j + 2, s0_s)
        absorb(j + 1, s1_s, False)
        return carry

    lax.fori_loop(0, i // 2, pair, 0)

    @pl.when((i & 1) == 0)
    def _():
        absorb(i, s0_s, True)

    @pl.when((i & 1) == 1)
    def _():
        scores(i, s1_s)
        absorb(i - 1, s0_s, False)
        absorb(i, s1_s, True)

    lam = _diff_lambda(lw_ref, lam_init)
    o = acc_s[...] / l_s[...]
    o = o[:, :tq] - lam * o[:, tq:]
    r = lax.rsqrt(jnp.mean(o * o, axis=0, keepdims=True) + EPS)
    o = (o * r) * sub_ref[...] * (1.0 - lam_init)
    o_ref[0] = o.T.astype(o_ref.dtype)


def _diff_prompt(lw, sub_col, qt, k, vt, *, tq, lam_init):
    b, t, _ = k.shape
    nt = t // tq
    grid = (b, N_HEADS, nt)
    return pl.pallas_call(
        functools.partial(_diff_prompt_kernel, tq=tq, lam_init=lam_init),
        grid=grid,
        in_specs=[
            pl.BlockSpec(lw.shape, lambda bi, h, i: (0, 0)),
            pl.BlockSpec((HEAD_W, 1), lambda bi, h, i: (0, 0)),
            pl.BlockSpec((1, 1, 1, HEAD_W, tq), lambda bi, h, i: (bi, h, i, 0, 0)),
            pl.BlockSpec((1, t, HEAD_W), lambda bi, h, i: (bi, 0, h)),
            pl.BlockSpec((1, 1, nt, HEAD_W, tq), lambda bi, h, i: (bi, h, 0, 0, 0)),
        ],
        out_specs=pl.BlockSpec((1, tq, HEAD_W), lambda bi, h, i: (bi, i, h)),
        out_shape=jax.ShapeDtypeStruct((b, t, W_BRANCH), BF16),
        scratch_shapes=[
            pltpu.VMEM((HEAD_W, 2 * tq), BF16),
            pltpu.VMEM((tq, 2 * tq), F32),
            pltpu.VMEM((tq, 2 * tq), F32),
            pltpu.VMEM((1, 2 * tq), F32),
            pltpu.VMEM((1, 2 * tq), F32),
            pltpu.VMEM((HEAD_W, 2 * tq), F32),
        ],
        compiler_params=_cparams(("arbitrary", "arbitrary", "arbitrary")),
        name="diff_prompt",
    )(lw, sub_col, qt, k, vt)


def _diff_sample_kernel(lw_ref, sub_ref, q_ref, kn_ref, vn_ref, kp_ref, vp_ref, o_ref,
                        qz_s, m_s, l_s, acc_s, *, tq, lam_init):
    kt = pl.program_id(1)
    rows = N_HEADS * 2 * tq
    cdt = qz_s.dtype

    def fold(k_ref, v_ref, new):
        n = k_ref.shape[2] // N_HEADS
        heads = range(N_HEADS)
        mine = [pl.ds(hd * 2 * tq, 2 * tq) for hd in heads]
        scores = []
        for hd in heads:
            k = k_ref[0, 0, pl.ds(hd, n, stride=N_HEADS), :].astype(cdt)
            s = _mm(qz_s[mine[hd], :], k, NT_DIMS)
            if new:
                row = lax.broadcasted_iota(jnp.int32, s.shape, 0)
                col = lax.broadcasted_iota(jnp.int32, s.shape, 1)
                s = jnp.where((col >> 6) <= ((row & (tq - 1)) >> 6), s, NEG_INF)
            scores.append(s)
        m_prev = [m_s[mine[hd], :] for hd in heads]
        m_new = [jnp.maximum(m_prev[hd], jnp.max(scores[hd], axis=-1, keepdims=True)) for hd in heads]
        alpha = [jnp.exp2(m_prev[hd] - m_new[hd]) for hd in heads]
        probs = [jnp.exp2(scores[hd] - m_new[hd][:, :1]) for hd in heads]
        for hd in heads:
            v = v_ref[0, 0, pl.ds(hd, n, stride=N_HEADS), :].astype(cdt)
            l_s[mine[hd], :] = alpha[hd] * l_s[mine[hd], :] + jnp.sum(probs[hd], axis=-1, keepdims=True)
            acc_s[mine[hd], :] = alpha[hd] * acc_s[mine[hd], :] + _mm(probs[hd].astype(cdt), v)
            m_s[mine[hd], :] = m_new[hd]

    @pl.when(kt == 0)
    def _():
        q = q_ref[0]
        lane = lax.broadcasted_iota(jnp.int32, (tq, HEAD_W), 1)
        parts = []
        for hd in range(N_HEADS):
            qh = q[:, hd * HEAD_W:(hd + 1) * HEAD_W]
            zero = jnp.zeros_like(qh)
            parts += [jnp.where(lane < HD_A, qh, zero), jnp.where(lane >= HD_A, qh, zero)]
        qz_s[...] = jnp.concatenate(parts, axis=0)
        m_s[...] = jnp.full((rows, LANES), NEG_INF, F32)
        l_s[...] = jnp.zeros((rows, LANES), F32)
        acc_s[...] = jnp.zeros((rows, HEAD_W), F32)
        fold(kn_ref, vn_ref, True)

    fold(kp_ref, vp_ref, False)

    @pl.when(kt == pl.num_programs(1) - 1)
    def _():
        lam = _diff_lambda(lw_ref, lam_init)
        o = acc_s[...] / l_s[...]
        outs = []
        for hd in range(N_HEADS):
            base = hd * 2 * tq
            oh = o[base:base + tq] - lam * o[base + tq:base + 2 * tq]
            r = lax.rsqrt(jnp.mean(oh * oh, axis=-1, keepdims=True) + EPS)
            outs.append((oh * r) * sub_ref[...] * (1.0 - lam_init))
        o_ref[0] = jnp.concatenate(outs, axis=1).astype(o_ref.dtype)


def _diff_sample(lw, sub_row, q, k_new, v_new, k_past, v_past, layer, *, key_rows, lam_init):
    b, tq, _ = q.shape
    past_rows = k_past.shape[2]
    grid = (b, past_rows // key_rows)
    new_spec = pl.BlockSpec((1, 1, tq * N_HEADS, HEAD_W), lambda bi, kt: (layer, bi, 0, 0))
    past_spec = pl.BlockSpec((1, 1, key_rows, HEAD_W), lambda bi, kt: (layer, bi, kt, 0))
    rows = N_HEADS * 2 * tq
    return pl.pallas_call(
        functools.partial(_diff_sample_kernel, tq=tq, lam_init=lam_init),
        grid=grid,
        in_specs=[
            pl.BlockSpec(lw.shape, lambda bi, kt: (0, 0)),
            pl.BlockSpec((1, HEAD_W), lambda bi, kt: (0, 0)),
            pl.BlockSpec((1, tq, W_BRANCH), lambda bi, kt: (bi, 0, 0)),
            new_spec, new_spec, past_spec, past_spec,
        ],
        out_specs=pl.BlockSpec((1, tq, W_BRANCH), lambda bi, kt: (bi, 0, 0)),
        out_shape=jax.ShapeDtypeStruct((b, tq, W_BRANCH), q.dtype),
        scratch_shapes=[
            pltpu.VMEM((rows, HEAD_W), q.dtype),
            pltpu.VMEM((rows, LANES), F32),
            pltpu.VMEM((rows, LANES), F32),
            pltpu.VMEM((rows, HEAD_W), F32),
        ],
        compiler_params=_cparams(("arbitrary", "arbitrary")),
        name="diff_sample",
    )(lw, sub_row, q, k_new, v_new, k_past, v_past)


def _neg_softplus(z):
    return -(jnp.maximum(z, 0.0) + jnp.log1p(jnp.exp(-jnp.abs(z))))


def _suffix_sums(x, upper):
    if upper.dtype == F32:
        return _mm(x, upper)
    hi = x.astype(BF16)
    lo = (x - hi.astype(F32)).astype(BF16)
    return (jnp.dot(hi, upper, preferred_element_type=F32)
            + jnp.dot(lo, upper, preferred_element_type=F32))


def _upper(n, dtype=BF16):
    j = lax.broadcasted_iota(jnp.int32, (n, n), 0)
    s = lax.broadcasted_iota(jnp.int32, (n, n), 1)
    return jnp.where(j >= s, 1.0, 0.0).astype(dtype)


def _stick_prompt_kernel(q_ref, kd_ref, vd_ref, kq_ref, vq_ref, kp_ref, vp_ref, o_ref, up_s, c_s, acc_s,
                         *, tq):
    i = pl.program_id(2)
    has_prev = jnp.where(i > 0, 1.0, 0.0)
    row = lax.broadcasted_iota(jnp.int32, (tq, tq), 0)
    col = lax.broadcasted_iota(jnp.int32, (tq, tq), 1)
    earlier = col < row
    q = q_ref[0]
    up_s[...] = _upper(tq)
    z = lax.dot_general(q, kd_ref[0], NT_DIMS, preferred_element_type=F32)
    zq = lax.dot_general(q, kq_ref[0], NT_DIMS, preferred_element_type=F32)
    lneg = jnp.where(earlier, _neg_softplus(z), 0.0)
    lnq = _neg_softplus(zq)
    cs = _suffix_sums(lneg, up_s[...])
    csq = _suffix_sums(lnq, up_s[...])
    c0 = jnp.sum(lneg, axis=-1, keepdims=True)
    a = jnp.where(earlier, jnp.exp(z + cs), 0.0)
    aq = jnp.exp(zq + csq + c0) * has_prev
    acc_s[...] = (jnp.dot(a.astype(BF16), vd_ref[0], preferred_element_type=F32)
                  + jnp.dot(aq.astype(BF16), vq_ref[0], preferred_element_type=F32))
    c1 = c0 + jnp.sum(lnq, axis=-1, keepdims=True) * has_prev
    c_s[...] = c1

    def cond(state):
        j, c_max = state
        return jnp.logical_and(j >= 0, c_max > STICK_DEAD_LOG)

    def body(state):
        j, _ = state
        start = pl.multiple_of(j * tq, tq)
        k = kp_ref[0, pl.ds(start, tq), :]
        v = vp_ref[0, pl.ds(start, tq), :]
        zj = lax.dot_general(q_ref[0], k, NT_DIMS, preferred_element_type=F32)
        ln = _neg_softplus(zj)
        csj = _suffix_sums(ln, up_s[...])
        c = c_s[...]
        aj = jnp.exp(zj + csj + c)
        acc_s[...] += jnp.dot(aj.astype(BF16), v, preferred_element_type=F32)
        c_new = c + jnp.sum(ln, axis=-1, keepdims=True)
        c_s[...] = c_new
        return j - 1, jnp.max(c_new)

    lax.while_loop(cond, body, (i - 2, jnp.max(c1)))
    o_ref[0] = acc_s[...].astype(o_ref.dtype)


def _stick_prompt(q, k, v, *, tq):
    b, t, _ = q.shape
    grid = (b, N_HEADS, t // tq)
    tile = pl.BlockSpec((1, tq, HEAD_W), lambda bi, h, i: (bi, i, h))
    before = pl.BlockSpec((1, tq, HEAD_W), lambda bi, h, i: (bi, jnp.maximum(i - 1, 0), h))
    whole = pl.BlockSpec((1, t, HEAD_W), lambda bi, h, i: (bi, 0, h))
    return pl.pallas_call(
        functools.partial(_stick_prompt_kernel, tq=tq),
        grid=grid,
        in_specs=[tile, tile, tile, before, before, whole, whole],
        out_specs=tile,
        out_shape=jax.ShapeDtypeStruct((b, t, W_BRANCH), BF16),
        scratch_shapes=[
            pltpu.VMEM((tq, tq), BF16),
            pltpu.VMEM((tq, 1), F32),
            pltpu.VMEM((tq, HEAD_W), F32),
        ],
        compiler_params=_cparams(("arbitrary", "arbitrary", "arbitrary")),
        name="stick_prompt",
    )(q, k, v, k, v, k, v)


STICK_RECENT_EXTRA = 2


def _stick_recent_kernel(q_ref, kd_ref, vd_ref, kq_ref, vq_ref, *rest, tq):
    older = rest[:2 * STICK_RECENT_EXTRA]
    o_ref, rest_ref, c_s, acc_s, worst_s = rest[2 * STICK_RECENT_EXTRA:]
    i = pl.program_id(1)
    has_prev = jnp.where(i > 0, 1.0, 0.0)
    row = lax.broadcasted_iota(jnp.int32, (tq, tq), 0)
    col = lax.broadcasted_iota(jnp.int32, (tq, tq), 1)
    earlier = col < row
    upper = _upper(tq)
    heads = [slice(hd * HEAD_W, (hd + 1) * HEAD_W) for hd in range(N_HEADS)]
    worst = None
    for hd, sl in enumerate(heads):
        q = q_ref[0, :, sl]
        z = lax.dot_general(q, kd_ref[0, :, sl], NT_DIMS, preferred_element_type=F32)
        zq = lax.dot_general(q, kq_ref[0, :, sl], NT_DIMS, preferred_element_type=F32)
        lneg = jnp.where(earlier, _neg_softplus(z), 0.0)
        lnq = _neg_softplus(zq)
        cs = _suffix_sums(lneg, upper)
        csq = _suffix_sums(lnq, upper)
        c0 = jnp.sum(lneg, axis=-1, keepdims=True)
        a = jnp.where(earlier, jnp.exp(z + cs), 0.0)
        aq = jnp.exp(zq + csq + c0) * has_prev
        acc_s[:, sl] = (jnp.dot(a.astype(BF16), vd_ref[0, :, sl], preferred_element_type=F32)
                        + jnp.dot(aq.astype(BF16), vq_ref[0, :, sl], preferred_element_type=F32))
        c1 = c0 + jnp.sum(lnq, axis=-1, keepdims=True) * has_prev
        c_s[hd] = c1
        c_max = jnp.max(c1)
        worst = c_max if worst is None else jnp.maximum(worst, c_max)
    worst_s[0] = worst

    for e in range(STICK_RECENT_EXTRA):
        k_ref, v_ref = older[2 * e], older[2 * e + 1]

        @pl.when(jnp.logical_and(worst_s[0] > STICK_DEAD_LOG, i >= e + 2))
        def _(k_ref=k_ref, v_ref=v_ref):
            worst_e = None
            for hd, sl in enumerate(heads):
                zj = lax.dot_general(q_ref[0, :, sl], k_ref[0, :, sl], NT_DIMS, preferred_element_type=F32)
                ln = _neg_softplus(zj)
                csj = _suffix_sums(ln, upper)
                c = c_s[hd]
                aj = jnp.exp(zj + csj + c)
                acc_s[:, sl] += jnp.dot(aj.astype(BF16), v_ref[0, :, sl], preferred_element_type=F32)
                c_new = c + jnp.sum(ln, axis=-1, keepdims=True)
                c_s[hd] = c_new
                c_max = jnp.max(c_new)
                worst_e = c_max if worst_e is None else jnp.maximum(worst_e, c_max)
            worst_s[0] = worst_e

    o_ref[0] = acc_s[...].astype(o_ref.dtype)
    left = jnp.where(i >= STICK_RECENT_EXTRA + 2, worst_s[0], 2.0 * STICK_DEAD_LOG)
    rest_ref[...] = jnp.full(rest_ref.shape, left, F32)


def _stick_recent(q, k, v, *, tq):
    b, t, _ = q.shape
    nt = t // tq
    tile = pl.BlockSpec((1, tq, W_BRANCH), lambda bi, i: (bi, i, 0))

    def back(n):
        return pl.BlockSpec((1, tq, W_BRANCH), lambda bi, i: (bi, jnp.maximum(i - n, 0), 0))

    n_older = STICK_RECENT_EXTRA
    return pl.pallas_call(
        functools.partial(_stick_recent_kernel, tq=tq),
        grid=(b, nt),
        in_specs=[tile, tile, tile, back(1), back(1)] + [back(2 + e // 2) for e in range(2 * n_older)],
        out_specs=[tile, pl.BlockSpec((1, 1, 8, LANES), lambda bi, i: (bi, i, 0, 0))],
        out_shape=[jax.ShapeDtypeStruct((b, t, W_BRANCH), BF16),
                   jax.ShapeDtypeStruct((b, nt, 8, LANES), F32)],
        scratch_shapes=[
            pltpu.VMEM((N_HEADS, tq, 1), F32),
            pltpu.VMEM((tq, W_BRANCH), F32),
            pltpu.SMEM((1,), F32),
        ],
        compiler_params=_cparams(("arbitrary", "arbitrary")),
        name="stick_recent",
    )(q, k, v, k, v, *([k, v] * n_older))


def _stick_sample_kernel(q_ref, kn_ref, vn_ref, kp_ref, vp_ref, o_ref, rest_ref,
                         q_s, up_s, c_s, acc_s, live_s, *, tq, sub_rows):
    kt = pl.program_id(1)
    rows = N_HEADS * tq

    def fold(k, v, new):
        n = k.shape[0]
        z = _mm(q_s[...], k.astype(q_s.dtype), NT_DIMS)
        row = lax.broadcasted_iota(jnp.int32, (rows, n), 0)
        col = lax.broadcasted_iota(jnp.int32, (rows, n), 1)
        keep = (row >> _log2(tq)) == (col & (N_HEADS - 1))
        if new:
            keep = jnp.logical_and(keep, (col >> 2) < (row & (tq - 1)))
        ln = jnp.where(keep, _neg_softplus(z), 0.0)
        cs = _suffix_sums(ln, _upper(n, up_s.dtype) if new else up_s[...])
        c = c_s[...]
        a = jnp.where(keep, jnp.exp(z + cs + c), 0.0)
        acc_s[...] += _mm(a.astype(q_s.dtype), v.astype(q_s.dtype))
        c_new = c + jnp.sum(ln, axis=-1, keepdims=True)
        c_s[...] = c_new
        live_s[0] = (jnp.max(c_new) > STICK_DEAD_LOG).astype(jnp.int32)

    @pl.when(kt == 0)
    def _():
        q = q_ref[0]
        q_s[...] = jnp.concatenate([q[:, hd * HEAD_W:(hd + 1) * HEAD_W] for hd in range(N_HEADS)], axis=0)
        up_s[...] = _upper(sub_rows, up_s.dtype)
        c_s[...] = jnp.zeros((rows, 1), F32)
        acc_s[...] = jnp.zeros((rows, HEAD_W), F32)
        fold(kn_ref[0, 0], vn_ref[0, 0], True)

    n_sub = kp_ref.shape[2] // sub_rows
    for sb in reversed(range(n_sub)):
        @pl.when(live_s[0] > 0)
        def _(sb=sb):
            sl = slice(sb * sub_rows, (sb + 1) * sub_rows)
            fold(kp_ref[0, 0, sl, :], vp_ref[0, 0, sl, :], False)

    @pl.when(kt == pl.num_programs(1) - 1)
    def _():
        acc = acc_s[...]
        o_ref[0] = jnp.concatenate([acc[hd * tq:(hd + 1) * tq] for hd in range(N_HEADS)],
                                   axis=1).astype(o_ref.dtype)
        rest_ref[...] = jnp.full(rest_ref.shape, jnp.max(c_s[...]), F32)


def _stick_sample(q, k_new, v_new, k_past, v_past, layer, *, key_rows, sub_rows, newest_only):
    b, tq, _ = q.shape
    n_blocks = k_past.shape[2] // key_rows
    grid = (b, 1 if newest_only else n_blocks)
    new_spec = pl.BlockSpec((1, 1, tq * N_HEADS, HEAD_W), lambda bi, kt: (layer, bi, 0, 0))
    past_spec = pl.BlockSpec((1, 1, key_rows, HEAD_W), lambda bi, kt: (layer, bi, n_blocks - 1 - kt, 0))
    rows = N_HEADS * tq
    return pl.pallas_call(
        functools.partial(_stick_sample_kernel, tq=tq, sub_rows=sub_rows),
        grid=grid,
        in_specs=[pl.BlockSpec((1, tq, W_BRANCH), lambda bi, kt: (bi, 0, 0)),
                  new_spec, new_spec, past_spec, past_spec],
        out_specs=[pl.BlockSpec((1, tq, W_BRANCH), lambda bi, kt: (bi, 0, 0)),
                   pl.BlockSpec((1, 8, LANES), lambda bi, kt: (bi, 0, 0))],
        out_shape=[jax.ShapeDtypeStruct((b, tq, W_BRANCH), q.dtype),
                   jax.ShapeDtypeStruct((b, 8, LANES), F32)],
        scratch_shapes=[
            pltpu.VMEM((rows, HEAD_W), q.dtype),
            pltpu.VMEM((sub_rows, sub_rows), q.dtype),
            pltpu.VMEM((rows, 1), F32),
            pltpu.VMEM((rows, HEAD_W), F32),
            pltpu.SMEM((1,), jnp.int32),
        ],
        compiler_params=_cparams(("arbitrary", "arbitrary")),
        name="stick_sample",
    )(q, k_new, v_new, k_past, v_past)


def _router_gates_t(logits_t, bias_col):
    n = EXPERTS_PER_GROUP
    aff = jax.nn.sigmoid(logits_t)
    sel = aff + bias_col
    pos = [sel[n * k:n * (k + 1)] for k in range(n)]
    in_top2 = []
    for j in range(n):
        rank = jnp.zeros(pos[j].shape, jnp.int32)
        for i in range(n):
            if i != j:
                ahead = (pos[i] >= pos[j]) if i < j else (pos[i] > pos[j])
                rank = rank + jnp.where(ahead, 1, 0)
        in_top2.append(rank < 2)
    score = sum(jnp.where(in_top2[j], pos[j], 0.0) for j in range(n))
    grp = lax.broadcasted_iota(jnp.int32, score.shape, 0)
    beaten = jnp.zeros(score.shape, jnp.int32)
    for g2 in range(N_GROUPS):
        other = score[g2:g2 + 1]
        ahead = jnp.logical_or(other > score, jnp.logical_and(other == score, g2 < grp))
        beaten = beaten + jnp.where(ahead, 1, 0)
    chosen = beaten == 0
    w = [jnp.where(jnp.logical_and(in_top2[j], chosen), aff[n * j:n * (j + 1)], 0.0) for j in range(n)]
    total = jnp.sum(sum(w), axis=0, keepdims=True)
    return jnp.concatenate([wj / total for wj in w], axis=0)


def _mixout_kernel(x_ref, oa_ref, ob_ref, ga_ref, gb_ref, g1_ref, sc_ref, sh_ref, gain_ref,
                   wa_ref, wb_ref, wo_ref, wr_ref, br_ref, x1_ref, h2_ref, gates_ref):
    bb, tt, d = x_ref.shape
    rows = bb * tt
    ya = _mm(oa_ref[...].reshape(rows, W_BRANCH), wa_ref[...])
    yb = _mm(ob_ref[...].reshape(rows, W_BRANCH), wb_ref[...])
    y = (ga_ref[...].reshape(rows, d).astype(F32) * ya + gb_ref[...].reshape(rows, d).astype(F32) * yb)
    mix = _mm(y.astype(wo_ref.dtype), wo_ref[...])
    x1 = x_ref[...] + g1_ref[...] * mix.reshape(bb, tt, d)
    x1_ref[...] = x1
    r = lax.rsqrt(jnp.mean(x1 * x1, axis=-1, keepdims=True) + EPS)
    h2 = ((x1 * r) * gain_ref[...] * (1.0 + sc_ref[...]) + sh_ref[...]).reshape(rows, d)
    h2_ref[...] = h2.reshape(bb, tt, d).astype(h2_ref.dtype)
    logits_t = _mm(wr_ref[...], h2.astype(wr_ref.dtype), NT_DIMS)
    gates_t = _router_gates_t(logits_t, br_ref[...])
    pad = jnp.zeros((LANES - N_EXPERTS, rows), F32)
    gates_ref[...] = jnp.concatenate([gates_t, pad], axis=0).T.reshape(bb, tt, LANES)


def _mixout(x, oa, ob, ga, gb, g1, sc2, sh2, gain, wa, wb, wo, wr, br, bb, tt, layer):
    b, t, d = x.shape
    grid = (b // bb, t // tt)
    xmap = lambda i, j: (i, j, 0)
    modspec = pl.BlockSpec((bb, 1, d), lambda i, j: (i, 0, 0))

    def full(a):
        return pl.BlockSpec(a.shape, lambda i, j: (0,) * a.ndim)

    def of_layer(a):
        return pl.BlockSpec((None,) + a.shape[1:], lambda i, j: (layer,) + (0,) * (a.ndim - 1))

    return pl.pallas_call(
        _mixout_kernel,
        grid=grid,
        in_specs=[
            pl.BlockSpec((bb, tt, d), xmap),
            pl.BlockSpec((bb, tt, W_BRANCH), xmap), pl.BlockSpec((bb, tt, W_BRANCH), xmap),
            pl.BlockSpec((bb, tt, d), xmap), pl.BlockSpec((bb, tt, d), xmap),
            modspec, modspec, modspec,
            pl.BlockSpec((1, 1, d), lambda i, j: (0, 0, 0)),
            of_layer(wa), of_layer(wb), of_layer(wo), full(wr), full(br),
        ],
        out_specs=[pl.BlockSpec((bb, tt, d), xmap), pl.BlockSpec((bb, tt, d), xmap),
                   pl.BlockSpec((bb, tt, LANES), xmap)],
        out_shape=[jax.ShapeDtypeStruct((b, t, d), F32), jax.ShapeDtypeStruct((b, t, d), BF16),
                   jax.ShapeDtypeStruct((b, t, LANES), F32)],
        compiler_params=_cparams(("arbitrary", "arbitrary")),
        name="mixout",
    )(x, oa, ob, ga, gb, g1, sc2, sh2, gain.reshape(1, 1, d), wa, wb, wo, wr, br)


def _moe_kernel(h_ref, gates_ref, x1_ref, g2_ref, gain_ref, wg_ref, wu_ref, wd_ref, o_ref, acc_s,
                *, final_norm):
    bb, tt, d = h_ref.shape
    rows = bb * tt
    grp = pl.program_id(2)
    h = h_ref[...].reshape(rows, d)
    gates = gates_ref[...].reshape(rows, LANES)
    lane = lax.broadcasted_iota(jnp.int32, (rows, LANES), 1)

    @pl.when(grp == 0)
    def _():
        acc_s[...] = jnp.zeros(acc_s.shape, F32)

    hidden = []
    for k in range(EXPERTS_PER_GROUP):
        g = jnp.dot(h, wg_ref[k], preferred_element_type=F32)
        u = jnp.dot(h, wu_ref[k], preferred_element_type=F32)
        gate = jnp.sum(jnp.where(lane == EXPERTS_PER_GROUP * k + grp, gates, 0.0), axis=-1, keepdims=True)
        hidden.append(((g * jax.nn.sigmoid(g)) * u * gate).astype(BF16))
    acc_s[...] += jnp.dot(jnp.concatenate(hidden, axis=1), wd_ref[...], preferred_element_type=F32)

    @pl.when(grp == N_GROUPS - 1)
    def _():
        x2 = x1_ref[...] + g2_ref[...] * acc_s[...].reshape(bb, tt, d)
        if final_norm:
            r = lax.rsqrt(jnp.mean(x2 * x2, axis=-1, keepdims=True) + EPS)
            x2 = (x2 * r) * gain_ref[...]
        o_ref[...] = x2


def _moe(h2, gates, x1, g2, gain, wg, wu, wd, bb, tt, layer, final_norm):
    b, t, d = x1.shape
    grid = (b // bb, t // tt, N_GROUPS)
    xmap = lambda i, j, e: (i, j, 0)
    n = EXPERTS_PER_GROUP
    return pl.pallas_call(
        functools.partial(_moe_kernel, final_norm=final_norm),
        grid=grid,
        in_specs=[
            pl.BlockSpec((bb, tt, d), xmap),
            pl.BlockSpec((bb, tt, LANES), xmap),
            pl.BlockSpec((bb, tt, d), xmap),
            pl.BlockSpec((bb, 1, d), lambda i, j, e: (i, 0, 0)),
            pl.BlockSpec((1, 1, d), lambda i, j, e: (0, 0, 0)),
            pl.BlockSpec((None, n, d, D_EXPERT), lambda i, j, e: (layer, e, 0, 0)),
            pl.BlockSpec((None, n, d, D_EXPERT), lambda i, j, e: (layer, e, 0, 0)),
            pl.BlockSpec((None, None, n * D_EXPERT, d), lambda i, j, e: (layer, e, 0, 0)),
        ],
        out_specs=pl.BlockSpec((bb, tt, d), xmap),
        out_shape=jax.ShapeDtypeStruct((b, t, d), F32),
        scratch_shapes=[pltpu.VMEM((bb * tt, d), F32)],
        compiler_params=_cparams(("arbitrary", "arbitrary", "arbitrary")),
        name="moe",
    )(h2, gates, x1, g2, gain.reshape(1, 1, d), wg, wu, wd)


def _rope_tables(pos):
    half = HD_A // 2
    inv = ROPE_THETA ** (-jnp.arange(half, dtype=F32) / half)
    lane = jnp.arange(LANES)
    ang = pos.astype(F32)[:, None] * inv[lane % half][None, :]
    sign = jnp.where(lane % HD_A < half, -1.0, 1.0).astype(F32)
    return jnp.cos(ang), jnp.sin(ang) * sign[None, :]


def _trunk(x, mod, pos, past, p, *, row_block, moe_rows, attn_tile):
    b, t, d = x.shape
    bb, tt = row_block
    depth = p["norm_mix"].shape[0]
    prompt = past is None
    cos, sin = _rope_tables(pos)
    caches = None
    for l in range(depth):
        wl = p["f32"] if (not prompt and l == 0) else p["bf16"]
        sh1, sc1, g1, sh2, sc2, g2 = [mod[l, :, i][:, None, :] for i in range(6)]
        outs = _inproj(x, p["norm_mix"][l], sc1, sh1, cos, sin, wl["w_in"], bb, tt, l, depth,
                       caches, feature_major=prompt)
        caches = outs[:4]
        qa, qb, kab, vab, kbb, vbb, ga, gb = outs[4:]
        lam_init = 0.8 - 0.6 * math.exp(-0.3 * l)
        lw, sub = p["a_lambda"][l], p["a_subln"][l]
        if prompt:
            oa = _diff_prompt(lw, sub[:, None], qa, kab, vab, tq=attn_tile, lam_init=lam_init)
            stick_tile = min(t, 256)
            ob, rest = _stick_recent(qb, kbb, vbb, tq=stick_tile)
            ob = lax.cond(jnp.max(rest) > STICK_DEAD_LOG,
                          lambda: _stick_prompt(qb, kbb, vbb, tq=stick_tile), lambda: ob)
        else:
            oa = _diff_sample(lw, sub[None, :], qa, caches[0], caches[1], past[0], past[1], l,
                              key_rows=attn_tile, lam_init=lam_init)
            stick = functools.partial(_stick_sample, qb, caches[2], caches[3], past[2], past[3], l,
                                      key_rows=attn_tile, sub_rows=min(attn_tile, 512))
            ob, rest = stick(newest_only=True)
            if past[2].shape[2] > attn_tile:
                ob = lax.cond(jnp.max(rest) > STICK_DEAD_LOG,
                              lambda: stick(newest_only=False)[0], lambda: ob)
        x1, h2, gates = _mixout(x, oa, ob, ga, gb, g1, sc2, sh2, p["norm_ffn"][l],
                                wl["w_proj_a"], wl["w_proj_b"], wl["w_out"],
                                wl["w_router"], p["b_router"], bb, tt, l)
        x = _moe(h2, gates, x1, g2, p["norm_final"], p["w_e_gate"], p["w_e_up"],
                 p["w_e_down"], bb, moe_rows, l, final_norm=(l == depth - 1))
    return (x,) + tuple(c.reshape(depth, b, t, N_HEADS, HEAD_W) for c in caches)


def kernel(x_prompt, x_sample, cache_a_k, cache_a_v, cache_b_k, cache_b_v, c_prompt, c_sample,
           w_in, w_proj_a, w_proj_b, w_out, a_lambda, a_subln, w_ada, b_ada,
           norm_mix, norm_ffn, norm_final, w_router, b_router, w_e_gate, w_e_up, w_e_down):
    d = x_prompt.shape[-1]
    bp, tp = x_prompt.shape[:2]
    bs, ts = x_sample.shape[:2]
    depth = w_in.shape[0]
    past_len = cache_a_k.shape[2]
    assert d == D_MODEL and past_len % CHUNK == 0 and ts <= CHUNK

    wr = w_router.T.reshape(N_GROUPS, EXPERTS_PER_GROUP, d).transpose(1, 0, 2).reshape(N_EXPERTS, d)
    br = b_router.reshape(N_GROUPS, EXPERTS_PER_GROUP).T.reshape(N_EXPERTS, 1)
    mixer_f32 = dict(w_in=w_in, w_proj_a=w_proj_a, w_proj_b=w_proj_b, w_out=w_out, w_router=wr)
    p = dict(
        f32=mixer_f32, bf16={k: v.astype(BF16) for k, v in mixer_f32.items()},
        a_lambda=a_lambda, a_subln=a_subln,
        norm_mix=norm_mix, norm_ffn=norm_ffn, norm_final=norm_final, b_router=br,
        w_e_gate=w_e_gate.astype(BF16), w_e_up=w_e_up.astype(BF16),
        w_e_down=w_e_down.astype(BF16).reshape(depth, N_GROUPS, EXPERTS_PER_GROUP * D_EXPERT, d),
    )

    n_c = bp + bs
    rows = -(-n_c // 8) * 8
    c_all = jnp.zeros((rows, d), F32).at[:bp].set(c_prompt).at[bp:n_c].set(c_sample)
    mod = _ada_mod(c_all, w_ada, b_ada).reshape(depth, rows, 6, d)

    pos_p = jnp.arange(tp, dtype=jnp.int32)
    pos_s = past_len + jnp.arange(ts, dtype=jnp.int32)
    tile_p = min(tp, 512)
    out_p = _trunk(x_prompt, mod[:, :bp], pos_p, None, p,
                   row_block=(1, tile_p), moe_rows=min(tp, 1024), attn_tile=tile_p)
    past = tuple(c.reshape(depth, bs, past_len * N_HEADS, HEAD_W)
                 for c in (cache_a_k, cache_a_v, cache_b_k, cache_b_v))
    out_s = _trunk(x_sample, mod[:, bp:n_c], pos_s, past, p,
                   row_block=(bs, ts), moe_rows=ts, attn_tile=min(past_len * N_HEADS, 2048))
    return (out_p[0], out_s[0]) + out_p[1:] + out_s[1:]
```

```python
import functools
import math

import jax
import jax.numpy as jnp
from jax import lax
from jax.experimental import pallas as pl
from jax.experimental.pallas import tpu as pltpu

D_MODEL = 1024
CHUNK = 64
N_HEADS = 4
HEAD_W = 128
HD_A = 64
HD_B = 128
W_BRANCH = N_HEADS * HEAD_W
N_EXPERTS = 16
EXPERTS_PER_GROUP = 4
N_GROUPS = N_EXPERTS // EXPERTS_PER_GROUP
D_EXPERT = 256
ROPE_THETA = 10000.0
EPS = 1e-6
NEG_INF = -1e30
STICK_DEAD_LOG = -104.0
LOG2_E = 1.4426950408889634
LANES = 128
VMEM_LIMIT = 56 * 1024 * 1024

F32 = jnp.float32
BF16 = jnp.bfloat16
NT_DIMS = (((1,), (1,)), ((), ()))


def _log2(n):
    assert n > 0 and n & (n - 1) == 0, n
    return n.bit_length() - 1


def _split_bf16(x):
    hi = x.astype(BF16)
    return hi, (x - hi.astype(F32)).astype(BF16)


def _mm(a, b, dims=None):
    assert a.dtype == b.dtype, (a.dtype, b.dtype)

    def one(x, y):
        if dims is None:
            return jnp.dot(x, y, preferred_element_type=F32)
        return lax.dot_general(x, y, dims, preferred_element_type=F32)

    if a.dtype != F32:
        return one(a, b)
    a_hi, a_lo = _split_bf16(a)
    b_hi, b_lo = _split_bf16(b)
    return one(a_hi, b_hi) + (one(a_hi, b_lo) + one(a_lo, b_hi))


def _cparams(sem):
    return pltpu.CompilerParams(dimension_semantics=sem, vmem_limit_bytes=VMEM_LIMIT)


def _ada_kernel(c_ref, w_ref, b_ref, o_ref):
    c = c_ref[...]
    a = c * jax.nn.sigmoid(c)

    @pl.when(pl.program_id(0) == 0)
    def _():
        o_ref[0] = _mm(a, w_ref[0]) + b_ref[0]

    @pl.when(pl.program_id(0) > 0)
    def _():
        o_ref[0] = _mm(a.astype(BF16), w_ref[0].astype(BF16)) + b_ref[0]


def _ada_mod(c_all, w_ada, b_ada, tn=1536):
    depth, d, n6 = w_ada.shape
    rows = c_all.shape[0]
    return pl.pallas_call(
        _ada_kernel,
        grid=(depth, n6 // tn),
        in_specs=[
            pl.BlockSpec((rows, d), lambda l, j: (0, 0)),
            pl.BlockSpec((1, d, tn), lambda l, j: (l, 0, j)),
            pl.BlockSpec((1, 1, tn), lambda l, j: (l, 0, j)),
        ],
        out_specs=pl.BlockSpec((1, rows, tn), lambda l, j: (l, 0, j)),
        out_shape=jax.ShapeDtypeStruct((depth, rows, n6), F32),
        compiler_params=_cparams(("arbitrary", "arbitrary")),
        name="ada_mod",
    )(c_all, w_ada, b_ada.reshape(depth, 1, n6))


def _rope128(x, cos, sin_signed, lane):
    fwd = pltpu.roll(x, LANES - HD_A // 2, 1)
    bwd = pltpu.roll(x, HD_A // 2, 1)
    partner = jnp.where((lane & (HD_A - 1)) < HD_A // 2, fwd, bwd)
    return x * cos + partner * sin_signed


def _inproj_kernel(*refs, n_alias, feature_major):
    (x_ref, g_ref, sc_ref, sh_ref, cos_ref, sin_ref, w_ref) = refs[:7]
    outs = refs[7 + n_alias:]
    (ka_c, va_c, kb_c, vb_c, qa_ref, qb_ref, kab_ref, vab_ref, kbb_ref, vbb_ref, ga_ref, gb_ref) = outs
    bb, tt, d = x_ref.shape
    rows = bb * tt
    x = x_ref[...]
    r = lax.rsqrt(jnp.mean(x * x, axis=-1, keepdims=True) + EPS)
    h = (x * r) * g_ref[...] * (1.0 + sc_ref[...]) + sh_ref[...]
    hb = h.reshape(rows, d).astype(w_ref.dtype)
    cos = cos_ref[...]
    sin = sin_ref[...]
    if bb > 1:
        cos = jnp.concatenate([cos] * bb, axis=0)
        sin = jnp.concatenate([sin] * bb, axis=0)
    lane = lax.broadcasted_iota(jnp.int32, (rows, LANES), 1)

    def proj(c0, width):
        return _mm(hb, w_ref[:, c0:c0 + width])

    def put(ref, val):
        ref[...] = val.reshape(ref.shape).astype(ref.dtype)

    def put_cache(ref, heads):
        for b in range(bb):
            for hd in range(N_HEADS):
                ref[0, b, pl.ds(hd, tt, stride=N_HEADS), :] = heads[hd][b * tt:(b + 1) * tt]

    def put_feature_major(ref, heads):
        for hd in range(N_HEADS):
            ref[0, hd, 0] = heads[hd].T.astype(ref.dtype)

    def split(val):
        return [val[:, hd * HEAD_W:(hd + 1) * HEAD_W] for hd in range(N_HEADS)]

    w = W_BRANCH
    qa = [_rope128(v, cos, sin, lane) * (HD_A ** -0.5 * LOG2_E) for v in split(proj(0, w))]
    ka = [_rope128(v, cos, sin, lane) for v in split(proj(w, w))]
    va = split(proj(2 * w, w))
    put_cache(ka_c, ka)
    put_cache(va_c, va)
    put(kab_ref, jnp.concatenate(ka, axis=1))
    if feature_major:
        put_feature_major(qa_ref, qa)
        put_feature_major(vab_ref, va)
    else:
        put(qa_ref, jnp.concatenate(qa, axis=1))
        put(vab_ref, jnp.concatenate(va, axis=1))
    put(qb_ref, proj(3 * w, w) * (HD_B ** -0.5))
    kb = proj(4 * w, w)
    put_cache(kb_c, split(kb))
    put(kbb_ref, kb)
    vb = proj(5 * w, w)
    put_cache(vb_c, split(vb))
    put(vbb_ref, vb)
    put(ga_ref, jax.nn.sigmoid(proj(6 * w, d)))
    put(gb_ref, jax.nn.sigmoid(proj(6 * w + d, d)))


def _inproj(x, gain, sc, sh, cos, sin, w, bb, tt, layer, depth, caches, feature_major):
    b, t, d = x.shape
    grid = (b // bb, t // tt)
    xmap = lambda i, j: (i, j, 0)
    modspec = pl.BlockSpec((bb, 1, d), lambda i, j: (i, 0, 0))
    tabspec = pl.BlockSpec((tt, LANES), lambda i, j: (j, 0))
    n_alias = 0 if caches is None else 4
    act = w.dtype

    def out(width, dtype):
        return (pl.BlockSpec((bb, tt, width), xmap), jax.ShapeDtypeStruct((b, t, width), dtype))

    def out_t():
        return (pl.BlockSpec((1, N_HEADS, 1, HEAD_W, tt), lambda i, j: (i, 0, j, 0, 0)),
                jax.ShapeDtypeStruct((b, N_HEADS, t // tt, HEAD_W, tt), BF16))

    cache = (pl.BlockSpec((1, bb, tt * N_HEADS, HEAD_W), lambda i, j: (layer, i, j, 0)),
             jax.ShapeDtypeStruct((depth, b, t * N_HEADS, HEAD_W), F32))
    outs = [cache, cache, cache, cache,
            out_t() if feature_major else out(W_BRANCH, act),
            out(W_BRANCH, act),
            out(W_BRANCH, BF16),
            out_t() if feature_major else out(W_BRANCH, BF16),
            out(W_BRANCH, BF16), out(W_BRANCH, BF16),
            out(d, act), out(d, act)]
    in_specs = [
        pl.BlockSpec((bb, tt, d), xmap),
        pl.BlockSpec((1, 1, d), lambda i, j: (0, 0, 0)),
        modspec, modspec, tabspec, tabspec,
        pl.BlockSpec((None,) + w.shape[1:], lambda i, j: (layer, 0, 0)),
    ] + [pl.BlockSpec(memory_space=pl.ANY)] * n_alias
    args = (x, gain.reshape(1, 1, d), sc, sh, cos, sin, w) + (() if caches is None else tuple(caches))
    return pl.pallas_call(
        functools.partial(_inproj_kernel, n_alias=n_alias, feature_major=feature_major),
        grid=grid,
        in_specs=in_specs,
        out_specs=[o[0] for o in outs],
        out_shape=[o[1] for o in outs],
        input_output_aliases={7 + k: k for k in range(n_alias)},
        compiler_params=_cparams(("arbitrary", "arbitrary")),
        name="inproj",
    )(*args)


def _diff_lambda(lw_ref, lam_init):
    lw = lw_ref[...]
    return (jnp.exp(jnp.sum(lw[0:1] * lw[1:2], axis=-1, keepdims=True))
            - jnp.exp(jnp.sum(lw[2:3] * lw[3:4], axis=-1, keepdims=True)) + lam_init)


def _diff_prompt_kernel(lw_ref, sub_ref, qt_ref, k_ref, vt_ref, o_ref, qz_s, s0_s, s1_s, m_s, l_s, acc_s,
                        *, tq, lam_init):
    i = pl.program_id(2)
    cols = 2 * tq
    qt = qt_ref[0, 0, 0]
    feat = lax.broadcasted_iota(jnp.int32, (HEAD_W, tq), 0)
    zero = jnp.zeros_like(qt)
    qz_s[...] = jnp.concatenate([jnp.where(feat < HD_A, qt, zero), jnp.where(feat >= HD_A, qt, zero)], axis=1)
    m_s[...] = jnp.full((1, cols), NEG_INF, F32)
    l_s[...] = jnp.zeros((1, cols), F32)
    acc_s[...] = jnp.zeros((HEAD_W, cols), F32)

    def scores(j, s_ref):
        start = pl.multiple_of(j * tq, tq)
        s_ref[...] = jnp.dot(k_ref[0, pl.ds(start, tq), :], qz_s[...], preferred_element_type=F32)

    def absorb(j, s_ref, masked):
        s = s_ref[...]
        if masked:
            key = lax.broadcasted_iota(jnp.int32, (tq, cols), 0)
            qry = lax.broadcasted_iota(jnp.int32, (tq, cols), 1)
            qry = jnp.where(qry >= tq, qry - tq, qry)
            s = jnp.where((key >> 6) <= (qry >> 6), s, NEG_INF)
        m_prev = m_s[...]
        m_new = jnp.maximum(m_prev, jnp.max(s, axis=0, keepdims=True))
        alpha = jnp.exp2(m_prev - m_new)
        p = jnp.exp2(s - m_new)
        l_s[...] = alpha * l_s[...] + jnp.sum(p, axis=0, keepdims=True)
        acc_s[...] = alpha * acc_s[...] + jnp.dot(vt_ref[0, 0, j], p.astype(BF16), preferred_element_type=F32)
        m_s[...] = m_new

    scores(0, s0_s)

    def pair(pi, carry):
        j = 2 * pi
        scores(j + 1, s1_s)
        absorb(j, s0_s, False)
        scores(j + 2, s0_s)
        absorb(j + 1, s1_s, False)
        return carry

    lax.fori_loop(0, i // 2, pair, 0)

    @pl.when((i & 1) == 0)
    def _():
        absorb(i, s0_s, True)

    @pl.when((i & 1) == 1)
    def _():
        scores(i, s1_s)
        absorb(i - 1, s0_s, False)
        absorb(i, s1_s, True)

    lam = _diff_lambda(lw_ref, lam_init)
    o = acc_s[...] / l_s[...]
    o = o[:, :tq] - lam * o[:, tq:]
    r = lax.rsqrt(jnp.mean(o * o, axis=0, keepdims=True) + EPS)
    o = (o * r) * sub_ref[...] * (1.0 - lam_init)
    o_ref[0] = o.T.astype(o_ref.dtype)


def _diff_prompt(lw, sub_col, qt, k, vt, *, tq, lam_init):
    b, t, _ = k.shape
    nt = t // tq
    grid = (b, N_HEADS, nt)
    return pl.pallas_call(
        functools.partial(_diff_prompt_kernel, tq=tq, lam_init=lam_init),
        grid=grid,
        in_specs=[
            pl.BlockSpec(lw.shape, lambda bi, h, i: (0, 0)),
            pl.BlockSpec((HEAD_W, 1), lambda bi, h, i: (0, 0)),
            pl.BlockSpec((1, 1, 1, HEAD_W, tq), lambda bi, h, i: (bi, h, i, 0, 0)),
            pl.BlockSpec((1, t, HEAD_W), lambda bi, h, i: (bi, 0, h)),
            pl.BlockSpec((1, 1, nt, HEAD_W, tq), lambda bi, h, i: (bi, h, 0, 0, 0)),
        ],
        out_specs=pl.BlockSpec((1, tq, HEAD_W), lambda bi, h, i: (bi, i, h)),
        out_shape=jax.ShapeDtypeStruct((b, t, W_BRANCH), BF16),
        scratch_shapes=[
            pltpu.VMEM((HEAD_W, 2 * tq), BF16),
            pltpu.VMEM((tq, 2 * tq), F32),
            pltpu.VMEM((tq, 2 * tq), F32),
            pltpu.VMEM((1, 2 * tq), F32),
            pltpu.VMEM((1, 2 * tq), F32),
            pltpu.VMEM((HEAD_W, 2 * tq), F32),
        ],
        compiler_params=_cparams(("arbitrary", "arbitrary", "arbitrary")),
        name="diff_prompt",
    )(lw, sub_col, qt, k, vt)


def _diff_sample_kernel(lw_ref, sub_ref, q_ref, kn_ref, vn_ref, kp_ref, vp_ref, o_ref,
                        qz_s, m_s, l_s, acc_s, *, tq, lam_init):
    kt = pl.program_id(1)
    rows = N_HEADS * 2 * tq
    cdt = qz_s.dtype

    def fold(k_ref, v_ref, new):
        n = k_ref.shape[2] // N_HEADS
        heads = range(N_HEADS)
        mine = [pl.ds(hd * 2 * tq, 2 * tq) for hd in heads]
        scores = []
        for hd in heads:
            k = k_ref[0, 0, pl.ds(hd, n, stride=N_HEADS), :].astype(cdt)
            s = _mm(qz_s[mine[hd], :], k, NT_DIMS)
            if new:
                row = lax.broadcasted_iota(jnp.int32, s.shape, 0)
                col = lax.broadcasted_iota(jnp.int32, s.shape, 1)
                s = jnp.where((col >> 6) <= ((row & (tq - 1)) >> 6), s, NEG_INF)
            scores.append(s)
        m_prev = [m_s[mine[hd], :] for hd in heads]
        m_new = [jnp.maximum(m_prev[hd], jnp.max(scores[hd], axis=-1, keepdims=True)) for hd in heads]
        alpha = [jnp.exp2(m_prev[hd] - m_new[hd]) for hd in heads]
        probs = [jnp.exp2(scores[hd] - m_new[hd][:, :1]) for hd in heads]
        for hd in heads:
            v = v_ref[0, 0, pl.ds(hd, n, stride=N_HEADS), :].astype(cdt)
            l_s[mine[hd], :] = alpha[hd] * l_s[mine[hd], :] + jnp.sum(probs[hd], axis=-1, keepdims=True)
            acc_s[mine[hd], :] = alpha[hd] * acc_s[mine[hd], :] + _mm(probs[hd].astype(cdt), v)
            m_s[mine[hd], :] = m_new[hd]

    @pl.when(kt == 0)
    def _():
        q = q_ref[0]
        lane = lax.broadcasted_iota(jnp.int32, (tq, HEAD_W), 1)
        parts = []
        for hd in range(N_HEADS):
            qh = q[:, hd * HEAD_W:(hd + 1) * HEAD_W]
            zero = jnp.zeros_like(qh)
            parts += [jnp.where(lane < HD_A, qh, zero), jnp.where(lane >= HD_A, qh, zero)]
        qz_s[...] = jnp.concatenate(parts, axis=0)
        m_s[...] = jnp.full((rows, LANES), NEG_INF, F32)
        l_s[...] = jnp.zeros((rows, LANES), F32)
        acc_s[...] = jnp.zeros((rows, HEAD_W), F32)
        fold(kn_ref, vn_ref, True)

    fold(kp_ref, vp_ref, False)

    @pl.when(kt == pl.num_programs(1) - 1)
    def _():
        lam = _diff_lambda(lw_ref, lam_init)
        o = acc_s[...] / l_s[...]
        outs = []
        for hd in range(N_HEADS):
            base = hd * 2 * tq
            oh = o[base:base + tq] - lam * o[base + tq:base + 2 * tq]
            r = lax.rsqrt(jnp.mean(oh * oh, axis=-1, keepdims=True) + EPS)
            outs.append((oh * r) * sub_ref[...] * (1.0 - lam_init))
        o_ref[0] = jnp.concatenate(outs, axis=1).astype(o_ref.dtype)


def _diff_sample(lw, sub_row, q, k_new, v_new, k_past, v_past, layer, *, key_rows, lam_init):
    b, tq, _ = q.shape
    past_rows = k_past.shape[2]
    grid = (b, past_rows // key_rows)
    new_spec = pl.BlockSpec((1, 1, tq * N_HEADS, HEAD_W), lambda bi, kt: (layer, bi, 0, 0))
    past_spec = pl.BlockSpec((1, 1, key_rows, HEAD_W), lambda bi, kt: (layer, bi, kt, 0))
    rows = N_HEADS * 2 * tq
    return pl.pallas_call(
        functools.partial(_diff_sample_kernel, tq=tq, lam_init=lam_init),
        grid=grid,
        in_specs=[
            pl.BlockSpec(lw.shape, lambda bi, kt: (0, 0)),
            pl.BlockSpec((1, HEAD_W), lambda bi, kt: (0, 0)),
            pl.BlockSpec((1, tq, W_BRANCH), lambda bi, kt: (bi, 0, 0)),
            new_spec, new_spec, past_spec, past_spec,
        ],
        out_specs=pl.BlockSpec((1, tq, W_BRANCH), lambda bi, kt: (bi, 0, 0)),
        out_shape=jax.ShapeDtypeStruct((b, tq, W_BRANCH), q.dtype),
        scratch_shapes=[
            pltpu.VMEM((rows, HEAD_W), q.dtype),
            pltpu.VMEM((rows, LANES), F32),
            pltpu.VMEM((rows, LANES), F32),
            pltpu.VMEM((rows, HEAD_W), F32),
        ],
        compiler_params=_cparams(("arbitrary", "arbitrary")),
        name="diff_sample",
    )(lw, sub_row, q, k_new, v_new, k_past, v_past)


def _neg_softplus(z):
    return -(jnp.maximum(z, 0.0) + jnp.log(1.0 + jnp.exp(-jnp.abs(z))))


def _suffix_sums(x, upper):
    hi, lo = _split_bf16(x)
    return (jnp.dot(hi, upper, preferred_element_type=F32)
            + jnp.dot(lo, upper, preferred_element_type=F32))


def _upper(n, dtype=BF16):
    j = lax.broadcasted_iota(jnp.int32, (n, n), 0)
    s = lax.broadcasted_iota(jnp.int32, (n, n), 1)
    return jnp.where(j >= s, 1.0, 0.0).astype(dtype)


def _stick_prompt_kernel(q_ref, kd_ref, vd_ref, kq_ref, vq_ref, kp_ref, vp_ref, o_ref, up_s, c_s, acc_s,
                         *, tq):
    i = pl.program_id(2)
    has_prev = jnp.where(i > 0, 1.0, 0.0)
    row = lax.broadcasted_iota(jnp.int32, (tq, tq), 0)
    col = lax.broadcasted_iota(jnp.int32, (tq, tq), 1)
    earlier = col < row
    q = q_ref[0]
    up_s[...] = _upper(tq)
    z = lax.dot_general(q, kd_ref[0], NT_DIMS, preferred_element_type=F32)
    zq = lax.dot_general(q, kq_ref[0], NT_DIMS, preferred_element_type=F32)
    lneg = jnp.where(earlier, _neg_softplus(z), 0.0)
    lnq = _neg_softplus(zq)
    cs = _suffix_sums(lneg, up_s[...])
    csq = _suffix_sums(lnq, up_s[...])
    c0 = jnp.sum(lneg, axis=-1, keepdims=True)
    a = jnp.where(earlier, jnp.exp(z + cs), 0.0)
    aq = jnp.exp(zq + csq + c0) * has_prev
    acc_s[...] = (jnp.dot(a.astype(BF16), vd_ref[0], preferred_element_type=F32)
                  + jnp.dot(aq.astype(BF16), vq_ref[0], preferred_element_type=F32))
    c1 = c0 + jnp.sum(lnq, axis=-1, keepdims=True) * has_prev
    c_s[...] = c1

    def cond(state):
        j, c_max = state
        return jnp.logical_and(j >= 0, c_max > STICK_DEAD_LOG)

    def body(state):
        j, _ = state
        start = pl.multiple_of(j * tq, tq)
        k = kp_ref[0, pl.ds(start, tq), :]
        v = vp_ref[0, pl.ds(start, tq), :]
        zj = lax.dot_general(q_ref[0], k, NT_DIMS, preferred_element_type=F32)
        ln = _neg_softplus(zj)
        csj = _suffix_sums(ln, up_s[...])
        c = c_s[...]
        aj = jnp.exp(zj + csj + c)
        acc_s[...] += jnp.dot(aj.astype(BF16), v, preferred_element_type=F32)
        c_new = c + jnp.sum(ln, axis=-1, keepdims=True)
        c_s[...] = c_new
        return j - 1, jnp.max(c_new)

    lax.while_loop(cond, body, (i - 2, jnp.max(c1)))
    o_ref[0] = acc_s[...].astype(o_ref.dtype)


def _stick_prompt(q, k, v, *, tq):
    b, t, _ = q.shape
    grid = (b, N_HEADS, t // tq)
    tile = pl.BlockSpec((1, tq, HEAD_W), lambda bi, h, i: (bi, i, h))
    before = pl.BlockSpec((1, tq, HEAD_W), lambda bi, h, i: (bi, jnp.maximum(i - 1, 0), h))
    whole = pl.BlockSpec((1, t, HEAD_W), lambda bi, h, i: (bi, 0, h))
    return pl.pallas_call(
        functools.partial(_stick_prompt_kernel, tq=tq),
        grid=grid,
        in_specs=[tile, tile, tile, before, before, whole, whole],
        out_specs=tile,
        out_shape=jax.ShapeDtypeStruct((b, t, W_BRANCH), BF16),
        scratch_shapes=[
            pltpu.VMEM((tq, tq), BF16),
            pltpu.VMEM((tq, 1), F32),
            pltpu.VMEM((tq, HEAD_W), F32),
        ],
        compiler_params=_cparams(("arbitrary", "arbitrary", "arbitrary")),
        name="stick_prompt",
    )(q, k, v, k, v, k, v)


STICK_RECENT_EXTRA = 2


def _stick_recent_kernel(q_ref, kd_ref, vd_ref, kq_ref, vq_ref, *rest, tq):
    older = rest[:2 * STICK_RECENT_EXTRA]
    o_ref, rest_ref, c_s, acc_s, worst_s = rest[2 * STICK_RECENT_EXTRA:]
    i = pl.program_id(1)
    has_prev = jnp.where(i > 0, 1.0, 0.0)
    row = lax.broadcasted_iota(jnp.int32, (tq, tq), 0)
    col = lax.broadcasted_iota(jnp.int32, (tq, tq), 1)
    earlier = col < row
    upper = _upper(tq)
    heads = [slice(hd * HEAD_W, (hd + 1) * HEAD_W) for hd in range(N_HEADS)]
    worst = None
    for hd, sl in enumerate(heads):
        q = q_ref[0, :, sl]
        z = lax.dot_general(q, kd_ref[0, :, sl], NT_DIMS, preferred_element_type=F32)
        zq = lax.dot_general(q, kq_ref[0, :, sl], NT_DIMS, preferred_element_type=F32)
        lneg = jnp.where(earlier, _neg_softplus(z), 0.0)
        lnq = _neg_softplus(zq)
        cs = _suffix_sums(lneg, upper)
        csq = _suffix_sums(lnq, upper)
        c0 = jnp.sum(lneg, axis=-1, keepdims=True)
        a = jnp.where(earlier, jnp.exp(z + cs), 0.0)
        aq = jnp.exp(zq + csq + c0) * has_prev
        acc_s[:, sl] = (jnp.dot(a.astype(BF16), vd_ref[0, :, sl], preferred_element_type=F32)
                        + jnp.dot(aq.astype(BF16), vq_ref[0, :, sl], preferred_element_type=F32))
        c1 = c0 + jnp.sum(lnq, axis=-1, keepdims=True) * has_prev
        c_s[hd] = c1
        c_max = jnp.max(c1)
        worst = c_max if worst is None else jnp.maximum(worst, c_max)
    worst_s[0] = worst

    for e in range(STICK_RECENT_EXTRA):
        k_ref, v_ref = older[2 * e], older[2 * e + 1]

        @pl.when(jnp.logical_and(worst_s[0] > STICK_DEAD_LOG, i >= e + 2))
        def _(k_ref=k_ref, v_ref=v_ref):
            worst_e = None
            for hd, sl in enumerate(heads):
                zj = lax.dot_general(q_ref[0, :, sl], k_ref[0, :, sl], NT_DIMS, preferred_element_type=F32)
                ln = _neg_softplus(zj)
                csj = _suffix_sums(ln, upper)
                c = c_s[hd]
                aj = jnp.exp(zj + csj + c)
                acc_s[:, sl] += jnp.dot(aj.astype(BF16), v_ref[0, :, sl], preferred_element_type=F32)
                c_new = c + jnp.sum(ln, axis=-1, keepdims=True)
                c_s[hd] = c_new
                c_max = jnp.max(c_new)
                worst_e = c_max if worst_e is None else jnp.maximum(worst_e, c_max)
            worst_s[0] = worst_e

    o_ref[0] = acc_s[...].astype(o_ref.dtype)
    left = jnp.where(i >= STICK_RECENT_EXTRA + 2, worst_s[0], 2.0 * STICK_DEAD_LOG)
    rest_ref[...] = jnp.full(rest_ref.shape, left, F32)


def _stick_recent(q, k, v, *, tq):
    b, t, _ = q.shape
    nt = t // tq
    tile = pl.BlockSpec((1, tq, W_BRANCH), lambda bi, i: (bi, i, 0))

    def back(n):
        return pl.BlockSpec((1, tq, W_BRANCH), lambda bi, i: (bi, jnp.maximum(i - n, 0), 0))

    n_older = STICK_RECENT_EXTRA
    return pl.pallas_call(
        functools.partial(_stick_recent_kernel, tq=tq),
        grid=(b, nt),
        in_specs=[tile, tile, tile, back(1), back(1)] + [back(2 + e // 2) for e in range(2 * n_older)],
        out_specs=[tile, pl.BlockSpec((1, 1, 8, LANES), lambda bi, i: (bi, i, 0, 0))],
        out_shape=[jax.ShapeDtypeStruct((b, t, W_BRANCH), BF16),
                   jax.ShapeDtypeStruct((b, nt, 8, LANES), F32)],
        scratch_shapes=[
            pltpu.VMEM((N_HEADS, tq, 1), F32),
            pltpu.VMEM((tq, W_BRANCH), F32),
            pltpu.SMEM((1,), F32),
        ],
        compiler_params=_cparams(("arbitrary", "arbitrary")),
        name="stick_recent",
    )(q, k, v, k, v, *([k, v] * n_older))


def _stick_sample_kernel(q_ref, kn_ref, vn_ref, kp_ref, vp_ref, o_ref, rest_ref,
                         q_s, up_s, c_s, acc_s, live_s, *, tq, sub_rows):
    kt = pl.program_id(1)
    rows = N_HEADS * tq

    def fold(k, v, new):
        n = k.shape[0]
        z = _mm(q_s[...], k.astype(q_s.dtype), NT_DIMS)
        row = lax.broadcasted_iota(jnp.int32, (rows, n), 0)
        col = lax.broadcasted_iota(jnp.int32, (rows, n), 1)
        keep = (row >> _log2(tq)) == (col & (N_HEADS - 1))
        if new:
            keep = jnp.logical_and(keep, (col >> 2) < (row & (tq - 1)))
        ln = jnp.where(keep, _neg_softplus(z), 0.0)
        cs = _suffix_sums(ln, _upper(n, up_s.dtype) if new else up_s[...])
        c = c_s[...]
        a = jnp.where(keep, jnp.exp(z + cs + c), 0.0)
        acc_s[...] += _mm(a.astype(q_s.dtype), v.astype(q_s.dtype))
        c_new = c + jnp.sum(ln, axis=-1, keepdims=True)
        c_s[...] = c_new
        live_s[0] = (jnp.max(c_new) > STICK_DEAD_LOG).astype(jnp.int32)

    @pl.when(kt == 0)
    def _():
        q = q_ref[0]
        q_s[...] = jnp.concatenate([q[:, hd * HEAD_W:(hd + 1) * HEAD_W] for hd in range(N_HEADS)], axis=0)
        up_s[...] = _upper(sub_rows, up_s.dtype)
        c_s[...] = jnp.zeros((rows, 1), F32)
        acc_s[...] = jnp.zeros((rows, HEAD_W), F32)
        fold(kn_ref[0, 0], vn_ref[0, 0], True)

    n_sub = kp_ref.shape[2] // sub_rows
    for sb in reversed(range(n_sub)):
        @pl.when(live_s[0] > 0)
        def _(sb=sb):
            sl = slice(sb * sub_rows, (sb + 1) * sub_rows)
            fold(kp_ref[0, 0, sl, :], vp_ref[0, 0, sl, :], False)

    @pl.when(kt == pl.num_programs(1) - 1)
    def _():
        acc = acc_s[...]
        o_ref[0] = jnp.concatenate([acc[hd * tq:(hd + 1) * tq] for hd in range(N_HEADS)],
                                   axis=1).astype(o_ref.dtype)
        rest_ref[...] = jnp.full(rest_ref.shape, jnp.max(c_s[...]), F32)


def _stick_sample(q, k_new, v_new, k_past, v_past, layer, *, key_rows, sub_rows, newest_only):
    b, tq, _ = q.shape
    n_blocks = k_past.shape[2] // key_rows
    grid = (b, 1 if newest_only else n_blocks)
    new_spec = pl.BlockSpec((1, 1, tq * N_HEADS, HEAD_W), lambda bi, kt: (layer, bi, 0, 0))
    past_spec = pl.BlockSpec((1, 1, key_rows, HEAD_W), lambda bi, kt: (layer, bi, n_blocks - 1 - kt, 0))
    rows = N_HEADS * tq
    return pl.pallas_call(
        functools.partial(_stick_sample_kernel, tq=tq, sub_rows=sub_rows),
        grid=grid,
        in_specs=[pl.BlockSpec((1, tq, W_BRANCH), lambda bi, kt: (bi, 0, 0)),
                  new_spec, new_spec, past_spec, past_spec],
        out_specs=[pl.BlockSpec((1, tq, W_BRANCH), lambda bi, kt: (bi, 0, 0)),
                   pl.BlockSpec((1, 8, LANES), lambda bi, kt: (bi, 0, 0))],
        out_shape=[jax.ShapeDtypeStruct((b, tq, W_BRANCH), q.dtype),
                   jax.ShapeDtypeStruct((b, 8, LANES), F32)],
        scratch_shapes=[
            pltpu.VMEM((rows, HEAD_W), q.dtype),
            pltpu.VMEM((sub_rows, sub_rows), BF16),
            pltpu.VMEM((rows, 1), F32),
            pltpu.VMEM((rows, HEAD_W), F32),
            pltpu.SMEM((1,), jnp.int32),
        ],
        compiler_params=_cparams(("arbitrary", "arbitrary")),
        name="stick_sample",
    )(q, k_new, v_new, k_past, v_past)


def _router_gates_t(logits_t, bias_col):
    n = EXPERTS_PER_GROUP
    aff = jax.nn.sigmoid(logits_t)
    sel = aff + bias_col
    pos = [sel[n * k:n * (k + 1)] for k in range(n)]
    in_top2 = []
    for j in range(n):
        rank = jnp.zeros(pos[j].shape, jnp.int32)
        for i in range(n):
            if i != j:
                ahead = (pos[i] >= pos[j]) if i < j else (pos[i] > pos[j])
                rank = rank + jnp.where(ahead, 1, 0)
        in_top2.append(rank < 2)
    score = sum(jnp.where(in_top2[j], pos[j], 0.0) for j in range(n))
    grp = lax.broadcasted_iota(jnp.int32, score.shape, 0)
    beaten = jnp.zeros(score.shape, jnp.int32)
    for g2 in range(N_GROUPS):
        other = score[g2:g2 + 1]
        ahead = jnp.logical_or(other > score, jnp.logical_and(other == score, g2 < grp))
        beaten = beaten + jnp.where(ahead, 1, 0)
    chosen = beaten == 0
    w = [jnp.where(jnp.logical_and(in_top2[j], chosen), aff[n * j:n * (j + 1)], 0.0) for j in range(n)]
    total = jnp.sum(sum(w), axis=0, keepdims=True)
    return jnp.concatenate([wj / total for wj in w], axis=0)


def _mixout_kernel(x_ref, oa_ref, ob_ref, ga_ref, gb_ref, g1_ref, sc_ref, sh_ref, gain_ref,
                   wa_ref, wb_ref, wo_ref, wr_ref, br_ref, x1_ref, h2_ref, gates_ref):
    bb, tt, d = x_ref.shape
    rows = bb * tt
    ya = _mm(oa_ref[...].reshape(rows, W_BRANCH), wa_ref[...])
    yb = _mm(ob_ref[...].reshape(rows, W_BRANCH), wb_ref[...])
    y = (ga_ref[...].reshape(rows, d).astype(F32) * ya + gb_ref[...].reshape(rows, d).astype(F32) * yb)
    mix = _mm(y.astype(wo_ref.dtype), wo_ref[...])
    x1 = x_ref[...] + g1_ref[...] * mix.reshape(bb, tt, d)
    x1_ref[...] = x1
    r = lax.rsqrt(jnp.mean(x1 * x1, axis=-1, keepdims=True) + EPS)
    h2 = ((x1 * r) * gain_ref[...] * (1.0 + sc_ref[...]) + sh_ref[...]).reshape(rows, d)
    h2_ref[...] = h2.reshape(bb, tt, d).astype(h2_ref.dtype)
    logits_t = _mm(wr_ref[...], h2.astype(wr_ref.dtype), NT_DIMS)
    gates_t = _router_gates_t(logits_t, br_ref[...])
    pad = jnp.zeros((LANES - N_EXPERTS, rows), F32)
    gates_ref[...] = jnp.concatenate([gates_t, pad], axis=0).T.reshape(bb, tt, LANES)


def _mixout(x, oa, ob, ga, gb, g1, sc2, sh2, gain, wa, wb, wo, wr, br, bb, tt, layer):
    b, t, d = x.shape
    grid = (b // bb, t // tt)
    xmap = lambda i, j: (i, j, 0)
    modspec = pl.BlockSpec((bb, 1, d), lambda i, j: (i, 0, 0))

    def full(a):
        return pl.BlockSpec(a.shape, lambda i, j: (0,) * a.ndim)

    def of_layer(a):
        return pl.BlockSpec((None,) + a.shape[1:], lambda i, j: (layer,) + (0,) * (a.ndim - 1))

    return pl.pallas_call(
        _mixout_kernel,
        grid=grid,
        in_specs=[
            pl.BlockSpec((bb, tt, d), xmap),
            pl.BlockSpec((bb, tt, W_BRANCH), xmap), pl.BlockSpec((bb, tt, W_BRANCH), xmap),
            pl.BlockSpec((bb, tt, d), xmap), pl.BlockSpec((bb, tt, d), xmap),
            modspec, modspec, modspec,
            pl.BlockSpec((1, 1, d), lambda i, j: (0, 0, 0)),
            of_layer(wa), of_layer(wb), of_layer(wo), full(wr), full(br),
        ],
        out_specs=[pl.BlockSpec((bb, tt, d), xmap), pl.BlockSpec((bb, tt, d), xmap),
                   pl.BlockSpec((bb, tt, LANES), xmap)],
        out_shape=[jax.ShapeDtypeStruct((b, t, d), F32), jax.ShapeDtypeStruct((b, t, d), BF16),
                   jax.ShapeDtypeStruct((b, t, LANES), F32)],
        compiler_params=_cparams(("arbitrary", "arbitrary")),
        name="mixout",
    )(x, oa, ob, ga, gb, g1, sc2, sh2, gain.reshape(1, 1, d), wa, wb, wo, wr, br)


def _moe_kernel(h_ref, gates_ref, x1_ref, g2_ref, gain_ref, wg_ref, wu_ref, wd_ref, o_ref, acc_s,
                *, final_norm):
    bb, tt, d = h_ref.shape
    rows = bb * tt
    grp = pl.program_id(2)
    h = h_ref[...].reshape(rows, d)
    gates = gates_ref[...].reshape(rows, LANES)
    lane = lax.broadcasted_iota(jnp.int32, (rows, LANES), 1)

    @pl.when(grp == 0)
    def _():
        acc_s[...] = jnp.zeros(acc_s.shape, F32)

    hidden = []
    for k in range(EXPERTS_PER_GROUP):
        g = jnp.dot(h, wg_ref[k], preferred_element_type=F32)
        u = jnp.dot(h, wu_ref[k], preferred_element_type=F32)
        gate = jnp.sum(jnp.where(lane == EXPERTS_PER_GROUP * k + grp, gates, 0.0), axis=-1, keepdims=True)
        hidden.append(((g * jax.nn.sigmoid(g)) * u * gate).astype(BF16))
    acc_s[...] += jnp.dot(jnp.concatenate(hidden, axis=1), wd_ref[...], preferred_element_type=F32)

    @pl.when(grp == N_GROUPS - 1)
    def _():
        x2 = x1_ref[...] + g2_ref[...] * acc_s[...].reshape(bb, tt, d)
        if final_norm:
            r = lax.rsqrt(jnp.mean(x2 * x2, axis=-1, keepdims=True) + EPS)
            x2 = (x2 * r) * gain_ref[...]
        o_ref[...] = x2


def _moe(h2, gates, x1, g2, gain, wg, wu, wd, bb, tt, layer, final_norm):
    b, t, d = x1.shape
    grid = (b // bb, t // tt, N_GROUPS)
    xmap = lambda i, j, e: (i, j, 0)
    n = EXPERTS_PER_GROUP
    return pl.pallas_call(
        functools.partial(_moe_kernel, final_norm=final_norm),
        grid=grid,
        in_specs=[
            pl.BlockSpec((bb, tt, d), xmap),
            pl.BlockSpec((bb, tt, LANES), xmap),
            pl.BlockSpec((bb, tt, d), xmap),
            pl.BlockSpec((bb, 1, d), lambda i, j, e: (i, 0, 0)),
            pl.BlockSpec((1, 1, d), lambda i, j, e: (0, 0, 0)),
            pl.BlockSpec((None, n, d, D_EXPERT), lambda i, j, e: (layer, e, 0, 0)),
            pl.BlockSpec((None, n, d, D_EXPERT), lambda i, j, e: (layer, e, 0, 0)),
            pl.BlockSpec((None, None, n * D_EXPERT, d), lambda i, j, e: (layer, e, 0, 0)),
        ],
        out_specs=pl.BlockSpec((bb, tt, d), xmap),
        out_shape=jax.ShapeDtypeStruct((b, t, d), F32),
        scratch_shapes=[pltpu.VMEM((bb * tt, d), F32)],
        compiler_params=_cparams(("arbitrary", "arbitrary", "arbitrary")),
        name="moe",
    )(h2, gates, x1, g2, gain.reshape(1, 1, d), wg, wu, wd)


def _rope_tables(pos):
    half = HD_A // 2
    inv = ROPE_THETA ** (-jnp.arange(half, dtype=F32) / half)
    lane = jnp.arange(LANES)
    ang = pos.astype(F32)[:, None] * inv[lane % half][None, :]
    sign = jnp.where(lane % HD_A < half, -1.0, 1.0).astype(F32)
    return jnp.cos(ang), jnp.sin(ang) * sign[None, :]


def _trunk(x, mod, pos, past, p, *, row_block, moe_rows, attn_tile):
    b, t, d = x.shape
    bb, tt = row_block
    depth = p["norm_mix"].shape[0]
    prompt = past is None
    cos, sin = _rope_tables(pos)
    caches = None
    for l in range(depth):
        wl = p["f32"] if (not prompt and l == 0) else p["bf16"]
        sh1, sc1, g1, sh2, sc2, g2 = [mod[l, :, i][:, None, :] for i in range(6)]
        outs = _inproj(x, p["norm_mix"][l], sc1, sh1, cos, sin, wl["w_in"], bb, tt, l, depth,
                       caches, feature_major=prompt)
        caches = outs[:4]
        qa, qb, kab, vab, kbb, vbb, ga, gb = outs[4:]
        lam_init = 0.8 - 0.6 * math.exp(-0.3 * l)
        lw, sub = p["a_lambda"][l], p["a_subln"][l]
        if prompt:
            oa = _diff_prompt(lw, sub[:, None], qa, kab, vab, tq=attn_tile, lam_init=lam_init)
            stick_tile = min(t, 256)
            ob, rest = _stick_recent(qb, kbb, vbb, tq=stick_tile)
            ob = lax.cond(jnp.max(rest) > STICK_DEAD_LOG,
                          lambda: _stick_prompt(qb, kbb, vbb, tq=stick_tile), lambda: ob)
        else:
            oa = _diff_sample(lw, sub[None, :], qa, caches[0], caches[1], past[0], past[1], l,
                              key_rows=attn_tile, lam_init=lam_init)
            stick = functools.partial(_stick_sample, qb, caches[2], caches[3], past[2], past[3], l,
                                      key_rows=attn_tile, sub_rows=min(attn_tile, 512))
            ob, rest = stick(newest_only=True)
            if past[2].shape[2] > attn_tile:
                ob = lax.cond(jnp.max(rest) > STICK_DEAD_LOG,
                              lambda: stick(newest_only=False)[0], lambda: ob)
        x1, h2, gates = _mixout(x, oa, ob, ga, gb, g1, sc2, sh2, p["norm_ffn"][l],
                                wl["w_proj_a"], wl["w_proj_b"], wl["w_out"],
                                wl["w_router"], p["b_router"], bb, tt, l)
        x = _moe(h2, gates, x1, g2, p["norm_final"], p["w_e_gate"], p["w_e_up"],
                 p["w_e_down"], bb, moe_rows, l, final_norm=(l == depth - 1))
    return (x,) + tuple(c.reshape(depth, b, t, N_HEADS, HEAD_W) for c in caches)


def kernel(x_prompt, x_sample, cache_a_k, cache_a_v, cache_b_k, cache_b_v, c_prompt, c_sample,
           w_in, w_proj_a, w_proj_b, w_out, a_lambda, a_subln, w_ada, b_ada,
           norm_mix, norm_ffn, norm_final, w_router, b_router, w_e_gate, w_e_up, w_e_down):
    d = x_prompt.shape[-1]
    bp, tp = x_prompt.shape[:2]
    bs, ts = x_sample.shape[:2]
    depth = w_in.shape[0]
    past_len = cache_a_k.shape[2]
    assert d == D_MODEL and past_len % CHUNK == 0 and ts <= CHUNK

    wr = w_router.T.reshape(N_GROUPS, EXPERTS_PER_GROUP, d).transpose(1, 0, 2).reshape(N_EXPERTS, d)
    br = b_router.reshape(N_GROUPS, EXPERTS_PER_GROUP).T.reshape(N_EXPERTS, 1)
    mixer_f32 = dict(w_in=w_in, w_proj_a=w_proj_a, w_proj_b=w_proj_b, w_out=w_out, w_router=wr)
    p = dict(
        f32=mixer_f32, bf16={k: v.astype(BF16) for k, v in mixer_f32.items()},
        a_lambda=a_lambda, a_subln=a_subln,
        norm_mix=norm_mix, norm_ffn=norm_ffn, norm_final=norm_final, b_router=br,
        w_e_gate=w_e_gate.astype(BF16), w_e_up=w_e_up.astype(BF16),
        w_e_down=w_e_down.astype(BF16).reshape(depth, N_GROUPS, EXPERTS_PER_GROUP * D_EXPERT, d),
    )

    n_c = bp + bs
    rows = -(-n_c // 8) * 8
    c_all = jnp.zeros((rows, d), F32).at[:bp].set(c_prompt).at[bp:n_c].set(c_sample)
    mod = _ada_mod(c_all, w_ada, b_ada).reshape(depth, rows, 6, d)

    pos_p = jnp.arange(tp, dtype=jnp.int32)
    pos_s = past_len + jnp.arange(ts, dtype=jnp.int32)
    tile_p = min(tp, 512)
    out_p = _trunk(x_prompt, mod[:, :bp], pos_p, None, p,
                   row_block=(1, tile_p), moe_rows=min(tp, 1024), attn_tile=tile_p)
    past = tuple(c.reshape(depth, bs, past_len * N_HEADS, HEAD_W)
                 for c in (cache_a_k, cache_a_v, cache_b_k, cache_b_v))
    out_s = _trunk(x_sample, mod[:, bp:n_c], pos_s, past, p,
                   row_block=(bs, ts), moe_rows=ts, attn_tile=min(past_len * N_HEADS, 2048))
    return (out_p[0], out_s[0]) + out_p[1:] + out_s[1:]
```

```python
import functools
import math

import jax
import jax.numpy as jnp
from jax import lax
from jax.experimental import pallas as pl
from jax.experimental.pallas import tpu as pltpu

D_MODEL = 1024
CHUNK = 64
N_HEADS = 4
HEAD_W = 128
HD_A = 64
HD_B = 128
W_BRANCH = N_HEADS * HEAD_W
N_EXPERTS = 16
EXPERTS_PER_GROUP = 4
N_GROUPS = N_EXPERTS // EXPERTS_PER_GROUP
D_EXPERT = 256
ROPE_THETA = 10000.0
EPS = 1e-6
NEG_INF = -1e30
STICK_DEAD_LOG = -104.0
LOG2_E = 1.4426950408889634
LANES = 128
VMEM_LIMIT = 56 * 1024 * 1024

F32 = jnp.float32
BF16 = jnp.bfloat16
NT_DIMS = (((1,), (1,)), ((), ()))


def _log2(n):
    assert n > 0 and n & (n - 1) == 0, n
    return n.bit_length() - 1


def _split_bf16(x):
    hi = x.astype(BF16)
    return hi, (x - hi.astype(F32)).astype(BF16)


def _mm(a, b, dims=None):
    assert a.dtype == b.dtype, (a.dtype, b.dtype)

    def one(x, y):
        if dims is None:
            return jnp.dot(x, y, preferred_element_type=F32)
        return lax.dot_general(x, y, dims, preferred_element_type=F32)

    if a.dtype != F32:
        return one(a, b)
    a_hi, a_lo = _split_bf16(a)
    b_hi, b_lo = _split_bf16(b)
    return one(a_hi, b_hi) + (one(a_hi, b_lo) + one(a_lo, b_hi))


def _cparams(sem):
    return pltpu.CompilerParams(dimension_semantics=sem, vmem_limit_bytes=VMEM_LIMIT)


def _ada_kernel(c_ref, w_ref, b_ref, o_ref):
    c = c_ref[...]
    a = c * jax.nn.sigmoid(c)

    @pl.when(pl.program_id(0) == 0)
    def _():
        o_ref[0] = _mm(a, w_ref[0]) + b_ref[0]

    @pl.when(pl.program_id(0) > 0)
    def _():
        o_ref[0] = _mm(a.astype(BF16), w_ref[0].astype(BF16)) + b_ref[0]


def _ada_mod(c_all, w_ada, b_ada, tn=1536):
    depth, d, n6 = w_ada.shape
    rows = c_all.shape[0]
    return pl.pallas_call(
        _ada_kernel,
        grid=(depth, n6 // tn),
        in_specs=[
            pl.BlockSpec((rows, d), lambda l, j: (0, 0)),
            pl.BlockSpec((1, d, tn), lambda l, j: (l, 0, j)),
            pl.BlockSpec((1, 1, tn), lambda l, j: (l, 0, j)),
        ],
        out_specs=pl.BlockSpec((1, rows, tn), lambda l, j: (l, 0, j)),
        out_shape=jax.ShapeDtypeStruct((depth, rows, n6), F32),
        compiler_params=_cparams(("arbitrary", "arbitrary")),
        name="ada_mod",
    )(c_all, w_ada, b_ada.reshape(depth, 1, n6))


def _rope128(x, cos, sin_signed, lane):
    fwd = pltpu.roll(x, LANES - HD_A // 2, 1)
    bwd = pltpu.roll(x, HD_A // 2, 1)
    partner = jnp.where((lane & (HD_A - 1)) < HD_A // 2, fwd, bwd)
    return x * cos + partner * sin_signed


def _inproj_kernel(*refs, n_alias, feature_major):
    (x_ref, g_ref, sc_ref, sh_ref, cos_ref, sin_ref, w_ref) = refs[:7]
    outs = refs[7 + n_alias:]
    (ka_c, va_c, kb_c, vb_c, qa_ref, qb_ref, kab_ref, vab_ref, kbb_ref, vbb_ref, ga_ref, gb_ref) = outs
    bb, tt, d = x_ref.shape
    rows = bb * tt
    x = x_ref[...]
    r = lax.rsqrt(jnp.mean(x * x, axis=-1, keepdims=True) + EPS)
    h = (x * r) * g_ref[...] * (1.0 + sc_ref[...]) + sh_ref[...]
    hb = h.reshape(rows, d).astype(w_ref.dtype)
    c32, s32 = cos_ref[...], sin_ref[...]
    cos = jnp.concatenate([c32] * (LANES // c32.shape[1]), axis=1)
    sin = jnp.concatenate([-s32, s32] * (LANES // (2 * s32.shape[1])), axis=1)
    if bb > 1:
        cos = jnp.concatenate([cos] * bb, axis=0)
        sin = jnp.concatenate([sin] * bb, axis=0)
    lane = lax.broadcasted_iota(jnp.int32, (rows, LANES), 1)

    def proj(c0, width):
        return _mm(hb, w_ref[:, c0:c0 + width])

    def put(ref, val):
        ref[...] = val.reshape(ref.shape).astype(ref.dtype)

    def put_cache(ref, heads):
        for b in range(bb):
            for hd in range(N_HEADS):
                ref[0, b, pl.ds(hd, tt, stride=N_HEADS), :] = heads[hd][b * tt:(b + 1) * tt]

    def put_feature_major(ref, heads):
        for hd in range(N_HEADS):
            ref[0, hd, 0] = heads[hd].T.astype(ref.dtype)

    def split(val):
        return [val[:, hd * HEAD_W:(hd + 1) * HEAD_W] for hd in range(N_HEADS)]

    w = W_BRANCH
    qa = [_rope128(v, cos, sin, lane) * (HD_A ** -0.5 * LOG2_E) for v in split(proj(0, w))]
    ka = [_rope128(v, cos, sin, lane) for v in split(proj(w, w))]
    va = split(proj(2 * w, w))
    put_cache(ka_c, ka)
    put_cache(va_c, va)
    put(kab_ref, jnp.concatenate(ka, axis=1))
    if feature_major:
        put_feature_major(qa_ref, qa)
        put_feature_major(vab_ref, va)
    else:
        put(qa_ref, jnp.concatenate(qa, axis=1))
        put(vab_ref, jnp.concatenate(va, axis=1))
    put(qb_ref, proj(3 * w, w) * (HD_B ** -0.5))
    kb = proj(4 * w, w)
    put_cache(kb_c, split(kb))
    put(kbb_ref, kb)
    vb = proj(5 * w, w)
    put_cache(vb_c, split(vb))
    put(vbb_ref, vb)
    put(ga_ref, jax.nn.sigmoid(proj(6 * w, d)))
    put(gb_ref, jax.nn.sigmoid(proj(6 * w + d, d)))


def _inproj(x, gain, sc, sh, cos, sin, w, bb, tt, layer, depth, caches, feature_major):
    b, t, d = x.shape
    grid = (b // bb, t // tt)
    xmap = lambda i, j: (i, j, 0)
    modspec = pl.BlockSpec((bb, 1, d), lambda i, j: (i, 0, 0))
    tabspec = pl.BlockSpec((tt, cos.shape[1]), lambda i, j: (j, 0))
    n_alias = 0 if caches is None else 4
    act = w.dtype

    def out(width, dtype):
        return (pl.BlockSpec((bb, tt, width), xmap), jax.ShapeDtypeStruct((b, t, width), dtype))

    def out_t():
        return (pl.BlockSpec((1, N_HEADS, 1, HEAD_W, tt), lambda i, j: (i, 0, j, 0, 0)),
                jax.ShapeDtypeStruct((b, N_HEADS, t // tt, HEAD_W, tt), BF16))

    cache = (pl.BlockSpec((1, bb, tt * N_HEADS, HEAD_W), lambda i, j: (layer, i, j, 0)),
             jax.ShapeDtypeStruct((depth, b, t * N_HEADS, HEAD_W), F32))
    outs = [cache, cache, cache, cache,
            out_t() if feature_major else out(W_BRANCH, act),
            out(W_BRANCH, act),
            out(W_BRANCH, BF16),
            out_t() if feature_major else out(W_BRANCH, BF16),
            out(W_BRANCH, BF16), out(W_BRANCH, BF16),
            out(d, act), out(d, act)]
    in_specs = [
        pl.BlockSpec((bb, tt, d), xmap),
        pl.BlockSpec((1, 1, d), lambda i, j: (0, 0, 0)),
        modspec, modspec, tabspec, tabspec,
        pl.BlockSpec((None,) + w.shape[1:], lambda i, j: (layer, 0, 0)),
    ] + [pl.BlockSpec(memory_space=pl.ANY)] * n_alias
    args = (x, gain.reshape(1, 1, d), sc, sh, cos, sin, w) + (() if caches is None else tuple(caches))
    return pl.pallas_call(
        functools.partial(_inproj_kernel, n_alias=n_alias, feature_major=feature_major),
        grid=grid,
        in_specs=in_specs,
        out_specs=[o[0] for o in outs],
        out_shape=[o[1] for o in outs],
        input_output_aliases={7 + k: k for k in range(n_alias)},
        compiler_params=_cparams(("arbitrary", "arbitrary")),
        name="inproj",
    )(*args)


def _diff_lambda(lw_ref, lam_init):
    lw = lw_ref[...]
    return (jnp.exp(jnp.sum(lw[0:1] * lw[1:2], axis=-1, keepdims=True))
            - jnp.exp(jnp.sum(lw[2:3] * lw[3:4], axis=-1, keepdims=True)) + lam_init)


def _diff_prompt_kernel(lw_ref, sub_ref, qt_ref, k_ref, vt_ref, o_ref, qz_s, s0_s, s1_s, m_s, l_s, acc_s,
                        *, tq, lam_init):
    i = pl.program_id(2)
    cols = 2 * tq
    qt = qt_ref[0, 0, 0]
    feat = lax.broadcasted_iota(jnp.int32, (HEAD_W, tq), 0)
    zero = jnp.zeros_like(qt)
    qz_s[...] = jnp.concatenate([jnp.where(feat < HD_A, qt, zero), jnp.where(feat >= HD_A, qt, zero)], axis=1)
    m_s[...] = jnp.full((1, cols), NEG_INF, F32)
    l_s[...] = jnp.zeros((1, cols), F32)
    acc_s[...] = jnp.zeros((HEAD_W, cols), F32)

    def scores(j, s_ref):
        start = pl.multiple_of(j * tq, tq)
        s_ref[...] = jnp.dot(k_ref[0, pl.ds(start, tq), :], qz_s[...], preferred_element_type=F32)

    def absorb(j, s_ref, masked):
        s = s_ref[...]
        if masked:
            key = lax.broadcasted_iota(jnp.int32, (tq, cols), 0)
            qry = lax.broadcasted_iota(jnp.int32, (tq, cols), 1)
            qry = jnp.where(qry >= tq, qry - tq, qry)
            s = jnp.where((key >> 6) <= (qry >> 6), s, NEG_INF)
        m_prev = m_s[...]
        m_new = jnp.maximum(m_prev, jnp.max(s, axis=0, keepdims=True))
        alpha = jnp.exp2(m_prev - m_new)
        p = jnp.exp2(s - m_new)
        l_s[...] = alpha * l_s[...] + jnp.sum(p, axis=0, keepdims=True)
        acc_s[...] = alpha * acc_s[...] + jnp.dot(vt_ref[0, 0, j], p.astype(BF16), preferred_element_type=F32)
        m_s[...] = m_new

    scores(0, s0_s)

    def pair(pi, carry):
        j = 2 * pi
        scores(j + 1, s1_s)
        absorb(j, s0_s, False)
        scores(j + 2, s0_s)
        absorb(j + 1, s1_s, False)
        return carry

    lax.fori_loop(0, i // 2, pair, 0)

    @pl.when((i & 1) == 0)
    def _():
        absorb(i, s0_s, True)

    @pl.when((i & 1) == 1)
    def _():
        scores(i, s1_s)
        absorb(i - 1, s0_s, False)
        absorb(i, s1_s, True)

    lam = _diff_lambda(lw_ref, lam_init)
    o = acc_s[...] / l_s[...]
    o = o[:, :tq] - lam * o[:, tq:]
    r = lax.rsqrt(jnp.mean(o * o, axis=0, keepdims=True) + EPS)
    o = (o * r) * sub_ref[...] * (1.0 - lam_init)
    o_ref[0] = o.T.astype(o_ref.dtype)


def _diff_prompt(lw, sub_col, qt, k, vt, *, tq, lam_init):
    b, t, _ = k.shape
    nt = t // tq
    grid = (b, N_HEADS, nt)
    return pl.pallas_call(
        functools.partial(_diff_prompt_kernel, tq=tq, lam_init=lam_init),
        grid=grid,
        in_specs=[
            pl.BlockSpec(lw.shape, lambda bi, h, i: (0, 0)),
            pl.BlockSpec((HEAD_W, 1), lambda bi, h, i: (0, 0)),
            pl.BlockSpec((1, 1, 1, HEAD_W, tq), lambda bi, h, i: (bi, h, i, 0, 0)),
            pl.BlockSpec((1, t, HEAD_W), lambda bi, h, i: (bi, 0, h)),
            pl.BlockSpec((1, 1, nt, HEAD_W, tq), lambda bi, h, i: (bi, h, 0, 0, 0)),
        ],
        out_specs=pl.BlockSpec((1, tq, HEAD_W), lambda bi, h, i: (bi, i, h)),
        out_shape=jax.ShapeDtypeStruct((b, t, W_BRANCH), BF16),
        scratch_shapes=[
            pltpu.VMEM((HEAD_W, 2 * tq), BF16),
            pltpu.VMEM((tq, 2 * tq), F32),
            pltpu.VMEM((tq, 2 * tq), F32),
            pltpu.VMEM((1, 2 * tq), F32),
            pltpu.VMEM((1, 2 * tq), F32),
            pltpu.VMEM((HEAD_W, 2 * tq), F32),
        ],
        compiler_params=_cparams(("arbitrary", "arbitrary", "arbitrary")),
        name="diff_prompt",
    )(lw, sub_col, qt, k, vt)


def _diff_sample_kernel(lw_ref, sub_ref, q_ref, kn_ref, vn_ref, kp_ref, vp_ref, o_ref,
                        qz_s, m_s, l_s, acc_s, *, tq, lam_init):
    kt = pl.program_id(1)
    rows = N_HEADS * 2 * tq
    cdt = qz_s.dtype

    def fold(k_ref, v_ref, new):
        n = k_ref.shape[2] // N_HEADS
        heads = range(N_HEADS)
        mine = [pl.ds(hd * 2 * tq, 2 * tq) for hd in heads]
        scores = []
        for hd in heads:
            k = k_ref[0, 0, pl.ds(hd, n, stride=N_HEADS), :].astype(cdt)
            s = _mm(qz_s[mine[hd], :], k, NT_DIMS)
            if new:
                row = lax.broadcasted_iota(jnp.int32, s.shape, 0)
                col = lax.broadcasted_iota(jnp.int32, s.shape, 1)
                s = jnp.where((col >> 6) <= ((row & (tq - 1)) >> 6), s, NEG_INF)
            scores.append(s)
        m_prev = [m_s[mine[hd], :] for hd in heads]
        m_new = [jnp.maximum(m_prev[hd], jnp.max(scores[hd], axis=-1, keepdims=True)) for hd in heads]
        alpha = [jnp.exp2(m_prev[hd] - m_new[hd]) for hd in heads]
        probs = [jnp.exp2(scores[hd] - m_new[hd][:, :1]) for hd in heads]
        for hd in heads:
            v = v_ref[0, 0, pl.ds(hd, n, stride=N_HEADS), :].astype(cdt)
            l_s[mine[hd], :] = alpha[hd] * l_s[mine[hd], :] + jnp.sum(probs[hd], axis=-1, keepdims=True)
            acc_s[mine[hd], :] = alpha[hd] * acc_s[mine[hd], :] + _mm(probs[hd].astype(cdt), v)
            m_s[mine[hd], :] = m_new[hd]

    @pl.when(kt == 0)
    def _():
        q = q_ref[0]
        lane = lax.broadcasted_iota(jnp.int32, (tq, HEAD_W), 1)
        parts = []
        for hd in range(N_HEADS):
            qh = q[:, hd * HEAD_W:(hd + 1) * HEAD_W]
            zero = jnp.zeros_like(qh)
            parts += [jnp.where(lane < HD_A, qh, zero), jnp.where(lane >= HD_A, qh, zero)]
        qz_s[...] = jnp.concatenate(parts, axis=0)
        m_s[...] = jnp.full((rows, LANES), NEG_INF, F32)
        l_s[...] = jnp.zeros((rows, LANES), F32)
        acc_s[...] = jnp.zeros((rows, HEAD_W), F32)
        fold(kn_ref, vn_ref, True)

    fold(kp_ref, vp_ref, False)

    @pl.when(kt == pl.num_programs(1) - 1)
    def _():
        lam = _diff_lambda(lw_ref, lam_init)
        o = acc_s[...] / l_s[...]
        outs = []
        for hd in range(N_HEADS):
            base = hd * 2 * tq
            oh = o[base:base + tq] - lam * o[base + tq:base + 2 * tq]
            r = lax.rsqrt(jnp.mean(oh * oh, axis=-1, keepdims=True) + EPS)
            outs.append((oh * r) * sub_ref[...] * (1.0 - lam_init))
        o_ref[0] = jnp.concatenate(outs, axis=1).astype(o_ref.dtype)


def _diff_sample(lw, sub_row, q, k_new, v_new, k_past, v_past, layer, *, key_rows, lam_init):
    b, tq, _ = q.shape
    past_rows = k_past.shape[2]
    grid = (b, past_rows // key_rows)
    new_spec = pl.BlockSpec((1, 1, tq * N_HEADS, HEAD_W), lambda bi, kt: (layer, bi, 0, 0))
    past_spec = pl.BlockSpec((1, 1, key_rows, HEAD_W), lambda bi, kt: (layer, bi, kt, 0))
    rows = N_HEADS * 2 * tq
    return pl.pallas_call(
        functools.partial(_diff_sample_kernel, tq=tq, lam_init=lam_init),
        grid=grid,
        in_specs=[
            pl.BlockSpec(lw.shape, lambda bi, kt: (0, 0)),
            pl.BlockSpec((1, HEAD_W), lambda bi, kt: (0, 0)),
            pl.BlockSpec((1, tq, W_BRANCH), lambda bi, kt: (bi, 0, 0)),
            new_spec, new_spec, past_spec, past_spec,
        ],
        out_specs=pl.BlockSpec((1, tq, W_BRANCH), lambda bi, kt: (bi, 0, 0)),
        out_shape=jax.ShapeDtypeStruct((b, tq, W_BRANCH), q.dtype),
        scratch_shapes=[
            pltpu.VMEM((rows, HEAD_W), q.dtype),
            pltpu.VMEM((rows, LANES), F32),
            pltpu.VMEM((rows, LANES), F32),
            pltpu.VMEM((rows, HEAD_W), F32),
        ],
        compiler_params=_cparams(("arbitrary", "arbitrary")),
        name="diff_sample",
    )(lw, sub_row, q, k_new, v_new, k_past, v_past)


def _neg_softplus(z):
    return -(jnp.maximum(z, 0.0) + jnp.log(1.0 + jnp.exp(-jnp.abs(z))))


def _suffix_sums(x, upper):
    hi, lo = _split_bf16(x)
    return (jnp.dot(hi, upper, preferred_element_type=F32)
            + jnp.dot(lo, upper, preferred_element_type=F32))


def _upper(n, dtype=BF16):
    j = lax.broadcasted_iota(jnp.int32, (n, n), 0)
    s = lax.broadcasted_iota(jnp.int32, (n, n), 1)
    return jnp.where(j >= s, 1.0, 0.0).astype(dtype)


def _stick_prompt_kernel(q_ref, kd_ref, vd_ref, kq_ref, vq_ref, kp_ref, vp_ref, o_ref, up_s, c_s, acc_s,
                         *, tq):
    i = pl.program_id(2)
    has_prev = jnp.where(i > 0, 1.0, 0.0)
    row = lax.broadcasted_iota(jnp.int32, (tq, tq), 0)
    col = lax.broadcasted_iota(jnp.int32, (tq, tq), 1)
    earlier = col < row
    q = q_ref[0]
    up_s[...] = _upper(tq)
    z = lax.dot_general(q, kd_ref[0], NT_DIMS, preferred_element_type=F32)
    zq = lax.dot_general(q, kq_ref[0], NT_DIMS, preferred_element_type=F32)
    lneg = jnp.where(earlier, _neg_softplus(z), 0.0)
    lnq = _neg_softplus(zq)
    cs = _suffix_sums(lneg, up_s[...])
    csq = _suffix_sums(lnq, up_s[...])
    c0 = jnp.sum(lneg, axis=-1, keepdims=True)
    a = jnp.where(earlier, jnp.exp(z + cs), 0.0)
    aq = jnp.exp(zq + csq + c0) * has_prev
    acc_s[...] = (jnp.dot(a.astype(BF16), vd_ref[0], preferred_element_type=F32)
                  + jnp.dot(aq.astype(BF16), vq_ref[0], preferred_element_type=F32))
    c1 = c0 + jnp.sum(lnq, axis=-1, keepdims=True) * has_prev
    c_s[...] = c1

    def cond(state):
        j, c_max = state
        return jnp.logical_and(j >= 0, c_max > STICK_DEAD_LOG)

    def body(state):
        j, _ = state
        start = pl.multiple_of(j * tq, tq)
        k = kp_ref[0, pl.ds(start, tq), :]
        v = vp_ref[0, pl.ds(start, tq), :]
        zj = lax.dot_general(q_ref[0], k, NT_DIMS, preferred_element_type=F32)
        ln = _neg_softplus(zj)
        csj = _suffix_sums(ln, up_s[...])
        c = c_s[...]
        aj = jnp.exp(zj + csj + c)
        acc_s[...] += jnp.dot(aj.astype(BF16), v, preferred_element_type=F32)
        c_new = c + jnp.sum(ln, axis=-1, keepdims=True)
        c_s[...] = c_new
        return j - 1, jnp.max(c_new)

    lax.while_loop(cond, body, (i - 2, jnp.max(c1)))
    o_ref[0] = acc_s[...].astype(o_ref.dtype)


def _stick_prompt(q, k, v, *, tq):
    b, t, _ = q.shape
    grid = (b, N_HEADS, t // tq)
    tile = pl.BlockSpec((1, tq, HEAD_W), lambda bi, h, i: (bi, i, h))
    before = pl.BlockSpec((1, tq, HEAD_W), lambda bi, h, i: (bi, jnp.maximum(i - 1, 0), h))
    whole = pl.BlockSpec((1, t, HEAD_W), lambda bi, h, i: (bi, 0, h))
    return pl.pallas_call(
        functools.partial(_stick_prompt_kernel, tq=tq),
        grid=grid,
        in_specs=[tile, tile, tile, before, before, whole, whole],
        out_specs=tile,
        out_shape=jax.ShapeDtypeStruct((b, t, W_BRANCH), BF16),
        scratch_shapes=[
            pltpu.VMEM((tq, tq), BF16),
            pltpu.VMEM((tq, 1), F32),
            pltpu.VMEM((tq, HEAD_W), F32),
        ],
        compiler_params=_cparams(("arbitrary", "arbitrary", "arbitrary")),
        name="stick_prompt",
    )(q, k, v, k, v, k, v)


STICK_RECENT_EXTRA = 2


def _stick_recent_kernel(q_ref, kd_ref, vd_ref, kq_ref, vq_ref, *rest, tq):
    older = rest[:2 * STICK_RECENT_EXTRA]
    o_ref, rest_ref, c_s, acc_s, worst_s = rest[2 * STICK_RECENT_EXTRA:]
    i = pl.program_id(1)
    has_prev = jnp.where(i > 0, 1.0, 0.0)
    row = lax.broadcasted_iota(jnp.int32, (tq, tq), 0)
    col = lax.broadcasted_iota(jnp.int32, (tq, tq), 1)
    earlier = col < row
    upper = _upper(tq)
    heads = [slice(hd * HEAD_W, (hd + 1) * HEAD_W) for hd in range(N_HEADS)]
    worst = None
    for hd, sl in enumerate(heads):
        q = q_ref[0, :, sl]
        z = lax.dot_general(q, kd_ref[0, :, sl], NT_DIMS, preferred_element_type=F32)
        zq = lax.dot_general(q, kq_ref[0, :, sl], NT_DIMS, preferred_element_type=F32)
        lneg = jnp.where(earlier, _neg_softplus(z), 0.0)
        lnq = _neg_softplus(zq)
        cs = _suffix_sums(lneg, upper)
        csq = _suffix_sums(lnq, upper)
        c0 = jnp.sum(lneg, axis=-1, keepdims=True)
        a = jnp.where(earlier, jnp.exp(z + cs), 0.0)
        aq = jnp.exp(zq + csq + c0) * has_prev
        acc_s[:, sl] = (jnp.dot(a.astype(BF16), vd_ref[0, :, sl], preferred_element_type=F32)
                        + jnp.dot(aq.astype(BF16), vq_ref[0, :, sl], preferred_element_type=F32))
        c1 = c0 + jnp.sum(lnq, axis=-1, keepdims=True) * has_prev
        c_s[hd] = c1
        c_max = jnp.max(c1)
        worst = c_max if worst is None else jnp.maximum(worst, c_max)
    worst_s[0] = worst

    for e in range(STICK_RECENT_EXTRA):
        k_ref, v_ref = older[2 * e], older[2 * e + 1]

        @pl.when(jnp.logical_and(worst_s[0] > STICK_DEAD_LOG, i >= e + 2))
        def _(k_ref=k_ref, v_ref=v_ref):
            worst_e = None
            for hd, sl in enumerate(heads):
                zj = lax.dot_general(q_ref[0, :, sl], k_ref[0, :, sl], NT_DIMS, preferred_element_type=F32)
                ln = _neg_softplus(zj)
                csj = _suffix_sums(ln, upper)
                c = c_s[hd]
                aj = jnp.exp(zj + csj + c)
                acc_s[:, sl] += jnp.dot(aj.astype(BF16), v_ref[0, :, sl], preferred_element_type=F32)
                c_new = c + jnp.sum(ln, axis=-1, keepdims=True)
                c_s[hd] = c_new
                c_max = jnp.max(c_new)
                worst_e = c_max if worst_e is None else jnp.maximum(worst_e, c_max)
            worst_s[0] = worst_e

    o_ref[0] = acc_s[...].astype(o_ref.dtype)
    left = jnp.where(i >= STICK_RECENT_EXTRA + 2, worst_s[0], 2.0 * STICK_DEAD_LOG)
    rest_ref[...] = jnp.full(rest_ref.shape, left, F32)


def _stick_recent(q, k, v, *, tq):
    b, t, _ = q.shape
    nt = t // tq
    tile = pl.BlockSpec((1, tq, W_BRANCH), lambda bi, i: (bi, i, 0))

    def back(n):
        return pl.BlockSpec((1, tq, W_BRANCH), lambda bi, i: (bi, jnp.maximum(i - n, 0), 0))

    n_older = STICK_RECENT_EXTRA
    return pl.pallas_call(
        functools.partial(_stick_recent_kernel, tq=tq),
        grid=(b, nt),
        in_specs=[tile, tile, tile, back(1), back(1)] + [back(2 + e // 2) for e in range(2 * n_older)],
        out_specs=[tile, pl.BlockSpec((1, 1, 8, LANES), lambda bi, i: (bi, i, 0, 0))],
        out_shape=[jax.ShapeDtypeStruct((b, t, W_BRANCH), BF16),
                   jax.ShapeDtypeStruct((b, nt, 8, LANES), F32)],
        scratch_shapes=[
            pltpu.VMEM((N_HEADS, tq, 1), F32),
            pltpu.VMEM((tq, W_BRANCH), F32),
            pltpu.SMEM((1,), F32),
        ],
        compiler_params=_cparams(("arbitrary", "arbitrary")),
        name="stick_recent",
    )(q, k, v, k, v, *([k, v] * n_older))


def _stick_sample_kernel(q_ref, kn_ref, vn_ref, kp_ref, vp_ref, o_ref, rest_ref,
                         q_s, up_s, c_s, acc_s, live_s, *, tq, sub_rows):
    kt = pl.program_id(1)
    rows = N_HEADS * tq

    def fold(k, v, new):
        n = k.shape[0]
        z = _mm(q_s[...], k.astype(q_s.dtype), NT_DIMS)
        row = lax.broadcasted_iota(jnp.int32, (rows, n), 0)
        col = lax.broadcasted_iota(jnp.int32, (rows, n), 1)
        keep = (row >> _log2(tq)) == (col & (N_HEADS - 1))
        if new:
            keep = jnp.logical_and(keep, (col >> 2) < (row & (tq - 1)))
        ln = jnp.where(keep, _neg_softplus(z), 0.0)
        cs = _suffix_sums(ln, _upper(n, up_s.dtype) if new else up_s[...])
        c = c_s[...]
        a = jnp.where(keep, jnp.exp(z + cs + c), 0.0)
        acc_s[...] += _mm(a.astype(q_s.dtype), v.astype(q_s.dtype))
        c_new = c + jnp.sum(ln, axis=-1, keepdims=True)
        c_s[...] = c_new
        live_s[0] = (jnp.max(c_new) > STICK_DEAD_LOG).astype(jnp.int32)

    @pl.when(kt == 0)
    def _():
        q = q_ref[0]
        q_s[...] = jnp.concatenate([q[:, hd * HEAD_W:(hd + 1) * HEAD_W] for hd in range(N_HEADS)], axis=0)
        up_s[...] = _upper(sub_rows, up_s.dtype)
        c_s[...] = jnp.zeros((rows, 1), F32)
        acc_s[...] = jnp.zeros((rows, HEAD_W), F32)
        fold(kn_ref[0, 0], vn_ref[0, 0], True)

    n_sub = kp_ref.shape[2] // sub_rows
    for sb in reversed(range(n_sub)):
        @pl.when(live_s[0] > 0)
        def _(sb=sb):
            sl = slice(sb * sub_rows, (sb + 1) * sub_rows)
            fold(kp_ref[0, 0, sl, :], vp_ref[0, 0, sl, :], False)

    @pl.when(kt == pl.num_programs(1) - 1)
    def _():
        acc = acc_s[...]
        o_ref[0] = jnp.concatenate([acc[hd * tq:(hd + 1) * tq] for hd in range(N_HEADS)],
                                   axis=1).astype(o_ref.dtype)
        rest_ref[...] = jnp.full(rest_ref.shape, jnp.max(c_s[...]), F32)


def _stick_sample(q, k_new, v_new, k_past, v_past, layer, *, key_rows, sub_rows, newest_only):
    b, tq, _ = q.shape
    n_blocks = k_past.shape[2] // key_rows
    grid = (b, 1 if newest_only else n_blocks)
    new_spec = pl.BlockSpec((1, 1, tq * N_HEADS, HEAD_W), lambda bi, kt: (layer, bi, 0, 0))
    past_spec = pl.BlockSpec((1, 1, key_rows, HEAD_W), lambda bi, kt: (layer, bi, n_blocks - 1 - kt, 0))
    rows = N_HEADS * tq
    return pl.pallas_call(
        functools.partial(_stick_sample_kernel, tq=tq, sub_rows=sub_rows),
        grid=grid,
        in_specs=[pl.BlockSpec((1, tq, W_BRANCH), lambda bi, kt: (bi, 0, 0)),
                  new_spec, new_spec, past_spec, past_spec],
        out_specs=[pl.BlockSpec((1, tq, W_BRANCH), lambda bi, kt: (bi, 0, 0)),
                   pl.BlockSpec((1, 8, LANES), lambda bi, kt: (bi, 0, 0))],
        out_shape=[jax.ShapeDtypeStruct((b, tq, W_BRANCH), q.dtype),
                   jax.ShapeDtypeStruct((b, 8, LANES), F32)],
        scratch_shapes=[
            pltpu.VMEM((rows, HEAD_W), q.dtype),
            pltpu.VMEM((sub_rows, sub_rows), BF16),
            pltpu.VMEM((rows, 1), F32),
            pltpu.VMEM((rows, HEAD_W), F32),
            pltpu.SMEM((1,), jnp.int32),
        ],
        compiler_params=_cparams(("arbitrary", "arbitrary")),
        name="stick_sample",
    )(q, k_new, v_new, k_past, v_past)


def _router_gates_t(logits_t, bias_col):
    n = EXPERTS_PER_GROUP
    aff = jax.nn.sigmoid(logits_t)
    sel = aff + bias_col
    pos = [sel[n * k:n * (k + 1)] for k in range(n)]
    in_top2 = []
    for j in range(n):
        rank = jnp.zeros(pos[j].shape, jnp.int32)
        for i in range(n):
            if i != j:
                ahead = (pos[i] >= pos[j]) if i < j else (pos[i] > pos[j])
                rank = rank + jnp.where(ahead, 1, 0)
        in_top2.append(rank < 2)
    score = sum(jnp.where(in_top2[j], pos[j], 0.0) for j in range(n))
    grp = lax.broadcasted_iota(jnp.int32, score.shape, 0)
    beaten = jnp.zeros(score.shape, jnp.int32)
    for g2 in range(N_GROUPS):
        other = score[g2:g2 + 1]
        ahead = jnp.logical_or(other > score, jnp.logical_and(other == score, g2 < grp))
        beaten = beaten + jnp.where(ahead, 1, 0)
    chosen = beaten == 0
    w = [jnp.where(jnp.logical_and(in_top2[j], chosen), aff[n * j:n * (j + 1)], 0.0) for j in range(n)]
    total = jnp.sum(sum(w), axis=0, keepdims=True)
    return jnp.concatenate([wj / total for wj in w], axis=0)


def _mixout_kernel(x_ref, oa_ref, ob_ref, ga_ref, gb_ref, g1_ref, sc_ref, sh_ref, gain_ref,
                   wa_ref, wb_ref, wo_ref, wr_ref, br_ref, x1_ref, h2_ref, gates_ref):
    bb, tt, d = x_ref.shape
    rows = bb * tt
    ya = _mm(oa_ref[...].reshape(rows, W_BRANCH), wa_ref[...])
    yb = _mm(ob_ref[...].reshape(rows, W_BRANCH), wb_ref[...])
    y = (ga_ref[...].reshape(rows, d).astype(F32) * ya + gb_ref[...].reshape(rows, d).astype(F32) * yb)
    mix = _mm(y.astype(wo_ref.dtype), wo_ref[...])
    x1 = x_ref[...] + g1_ref[...] * mix.reshape(bb, tt, d)
    x1_ref[...] = x1
    r = lax.rsqrt(jnp.mean(x1 * x1, axis=-1, keepdims=True) + EPS)
    h2 = ((x1 * r) * gain_ref[...] * (1.0 + sc_ref[...]) + sh_ref[...]).reshape(rows, d)
    h2_ref[...] = h2.reshape(bb, tt, d).astype(h2_ref.dtype)
    logits_t = _mm(wr_ref[...], h2.astype(wr_ref.dtype), NT_DIMS)
    gates_t = _router_gates_t(logits_t, br_ref[...])
    pad = jnp.zeros((LANES - N_EXPERTS, rows), F32)
    gates_ref[...] = jnp.concatenate([gates_t, pad], axis=0).T.reshape(bb, tt, LANES)


def _mixout(x, oa, ob, ga, gb, g1, sc2, sh2, gain, wa, wb, wo, wr, br, bb, tt, layer):
    b, t, d = x.shape
    grid = (b // bb, t // tt)
    xmap = lambda i, j: (i, j, 0)
    modspec = pl.BlockSpec((bb, 1, d), lambda i, j: (i, 0, 0))

    def full(a):
        return pl.BlockSpec(a.shape, lambda i, j: (0,) * a.ndim)

    def of_layer(a):
        return pl.BlockSpec((None,) + a.shape[1:], lambda i, j: (layer,) + (0,) * (a.ndim - 1))

    return pl.pallas_call(
        _mixout_kernel,
        grid=grid,
        in_specs=[
            pl.BlockSpec((bb, tt, d), xmap),
            pl.BlockSpec((bb, tt, W_BRANCH), xmap), pl.BlockSpec((bb, tt, W_BRANCH), xmap),
            pl.BlockSpec((bb, tt, d), xmap), pl.BlockSpec((bb, tt, d), xmap),
            modspec, modspec, modspec,
            pl.BlockSpec((1, 1, d), lambda i, j: (0, 0, 0)),
            of_layer(wa), of_layer(wb), of_layer(wo), full(wr), full(br),
        ],
        out_specs=[pl.BlockSpec((bb, tt, d), xmap), pl.BlockSpec((bb, tt, d), xmap),
                   pl.BlockSpec((bb, tt, LANES), xmap)],
        out_shape=[jax.ShapeDtypeStruct((b, t, d), F32), jax.ShapeDtypeStruct((b, t, d), BF16),
                   jax.ShapeDtypeStruct((b, t, LANES), F32)],
        compiler_params=_cparams(("arbitrary", "arbitrary")),
        name="mixout",
    )(x, oa, ob, ga, gb, g1, sc2, sh2, gain.reshape(1, 1, d), wa, wb, wo, wr, br)


def _moe_kernel(h_ref, gates_ref, x1_ref, g2_ref, gain_ref, wg_ref, wu_ref, wd_ref, o_ref, acc_s,
                *, final_norm):
    bb, tt, d = h_ref.shape
    rows = bb * tt
    grp = pl.program_id(2)
    h = h_ref[...].reshape(rows, d)
    gates = gates_ref[...].reshape(rows, LANES)
    lane = lax.broadcasted_iota(jnp.int32, (rows, LANES), 1)

    @pl.when(grp == 0)
    def _():
        acc_s[...] = jnp.zeros(acc_s.shape, F32)

    hidden = []
    for k in range(EXPERTS_PER_GROUP):
        g = jnp.dot(h, wg_ref[k], preferred_element_type=F32)
        u = jnp.dot(h, wu_ref[k], preferred_element_type=F32)
        gate = jnp.sum(jnp.where(lane == EXPERTS_PER_GROUP * k + grp, gates, 0.0), axis=-1, keepdims=True)
        hidden.append(((g * jax.nn.sigmoid(g)) * u * gate).astype(BF16))
    acc_s[...] += jnp.dot(jnp.concatenate(hidden, axis=1), wd_ref[...], preferred_element_type=F32)

    @pl.when(grp == N_GROUPS - 1)
    def _():
        x2 = x1_ref[...] + g2_ref[...] * acc_s[...].reshape(bb, tt, d)
        if final_norm:
            r = lax.rsqrt(jnp.mean(x2 * x2, axis=-1, keepdims=True) + EPS)
            x2 = (x2 * r) * gain_ref[...]
        o_ref[...] = x2


def _moe(h2, gates, x1, g2, gain, wg, wu, wd, bb, tt, layer, final_norm):
    b, t, d = x1.shape
    grid = (b // bb, t // tt, N_GROUPS)
    xmap = lambda i, j, e: (i, j, 0)
    n = EXPERTS_PER_GROUP
    return pl.pallas_call(
        functools.partial(_moe_kernel, final_norm=final_norm),
        grid=grid,
        in_specs=[
            pl.BlockSpec((bb, tt, d), xmap),
            pl.BlockSpec((bb, tt, LANES), xmap),
            pl.BlockSpec((bb, tt, d), xmap),
            pl.BlockSpec((bb, 1, d), lambda i, j, e: (i, 0, 0)),
            pl.BlockSpec((1, 1, d), lambda i, j, e: (0, 0, 0)),
            pl.BlockSpec((None, n, d, D_EXPERT), lambda i, j, e: (layer, e, 0, 0)),
            pl.BlockSpec((None, n, d, D_EXPERT), lambda i, j, e: (layer, e, 0, 0)),
            pl.BlockSpec((None, None, n * D_EXPERT, d), lambda i, j, e: (layer, e, 0, 0)),
        ],
        out_specs=pl.BlockSpec((bb, tt, d), xmap),
        out_shape=jax.ShapeDtypeStruct((b, t, d), F32),
        scratch_shapes=[pltpu.VMEM((bb * tt, d), F32)],
        compiler_params=_cparams(("arbitrary", "arbitrary", "arbitrary")),
        name="moe",
    )(h2, gates, x1, g2, gain.reshape(1, 1, d), wg, wu, wd)


def _rope_tables(pos):
    half = HD_A // 2
    inv = ROPE_THETA ** (-jnp.arange(half, dtype=F32) / half)
    ang = pos.astype(F32)[:, None] * inv[None, :]
    return jnp.cos(ang), jnp.sin(ang)


def _trunk(x, mod, pos, past, p, *, row_block, moe_rows, attn_tile):
    b, t, d = x.shape
    bb, tt = row_block
    depth = p["norm_mix"].shape[0]
    prompt = past is None
    cos, sin = _rope_tables(pos)
    caches = None
    for l in range(depth):
        wl = p["f32"] if (not prompt and l == 0) else p["bf16"]
        sh1, sc1, g1, sh2, sc2, g2 = [mod[l, :, i][:, None, :] for i in range(6)]
        outs = _inproj(x, p["norm_mix"][l], sc1, sh1, cos, sin, wl["w_in"], bb, tt, l, depth,
                       caches, feature_major=prompt)
        caches = outs[:4]
        qa, qb, kab, vab, kbb, vbb, ga, gb = outs[4:]
        lam_init = 0.8 - 0.6 * math.exp(-0.3 * l)
        lw, sub = p["a_lambda"][l], p["a_subln"][l]
        if prompt:
            oa = _diff_prompt(lw, sub[:, None], qa, kab, vab, tq=attn_tile, lam_init=lam_init)
            stick_tile = min(t, 256)
            ob, rest = _stick_recent(qb, kbb, vbb, tq=stick_tile)
            ob = lax.cond(jnp.max(rest) > STICK_DEAD_LOG,
                          lambda: _stick_prompt(qb, kbb, vbb, tq=stick_tile), lambda: ob)
        else:
            oa = _diff_sample(lw, sub[None, :], qa, caches[0], caches[1], past[0], past[1], l,
                              key_rows=min(past[0].shape[2], 4 * attn_tile), lam_init=lam_init)
            stick = functools.partial(_stick_sample, qb, caches[2], caches[3], past[2], past[3], l,
                                      key_rows=attn_tile, sub_rows=min(attn_tile, 512))
            ob, rest = stick(newest_only=True)
            if past[2].shape[2] > attn_tile:
                ob = lax.cond(jnp.max(rest) > STICK_DEAD_LOG,
                              lambda: stick(newest_only=False)[0], lambda: ob)
        x1, h2, gates = _mixout(x, oa, ob, ga, gb, g1, sc2, sh2, p["norm_ffn"][l],
                                wl["w_proj_a"], wl["w_proj_b"], wl["w_out"],
                                wl["w_router"], p["b_router"], bb, tt, l)
        x = _moe(h2, gates, x1, g2, p["norm_final"], p["w_e_gate"], p["w_e_up"],
                 p["w_e_down"], bb, moe_rows, l, final_norm=(l == depth - 1))
    return (x,) + tuple(c.reshape(depth, b, t, N_HEADS, HEAD_W) for c in caches)


def kernel(x_prompt, x_sample, cache_a_k, cache_a_v, cache_b_k, cache_b_v, c_prompt, c_sample,
           w_in, w_proj_a, w_proj_b, w_out, a_lambda, a_subln, w_ada, b_ada,
           norm_mix, norm_ffn, norm_final, w_router, b_router, w_e_gate, w_e_up, w_e_down):
    d = x_prompt.shape[-1]
    bp, tp = x_prompt.shape[:2]
    bs, ts = x_sample.shape[:2]
    depth = w_in.shape[0]
    past_len = cache_a_k.shape[2]
    assert d == D_MODEL and past_len % CHUNK == 0 and ts <= CHUNK

    wr = w_router.T.reshape(N_GROUPS, EXPERTS_PER_GROUP, d).transpose(1, 0, 2).reshape(N_EXPERTS, d)
    br = b_router.reshape(N_GROUPS, EXPERTS_PER_GROUP).T.reshape(N_EXPERTS, 1)
    mixer_f32 = dict(w_in=w_in, w_proj_a=w_proj_a, w_proj_b=w_proj_b, w_out=w_out, w_router=wr)
    p = dict(
        f32=mixer_f32, bf16={k: v.astype(BF16) for k, v in mixer_f32.items()},
        a_lambda=a_lambda, a_subln=a_subln,
        norm_mix=norm_mix, norm_ffn=norm_ffn, norm_final=norm_final, b_router=br,
        w_e_gate=w_e_gate.astype(BF16), w_e_up=w_e_up.astype(BF16),
        w_e_down=w_e_down.astype(BF16).reshape(depth, N_GROUPS, EXPERTS_PER_GROUP * D_EXPERT, d),
    )

    n_c = bp + bs
    rows = -(-n_c // 8) * 8
    c_all = jnp.zeros((rows, d), F32).at[:bp].set(c_prompt).at[bp:n_c].set(c_sample)
    mod = _ada_mod(c_all, w_ada, b_ada).reshape(depth, rows, 6, d)

    pos_p = jnp.arange(tp, dtype=jnp.int32)
    pos_s = past_len + jnp.arange(ts, dtype=jnp.int32)
    tile_p = min(tp, 512)
    out_p = _trunk(x_prompt, mod[:, :bp], pos_p, None, p,
                   row_block=(1, tile_p), moe_rows=min(tp, 1024), attn_tile=tile_p)
    past = tuple(c.reshape(depth, bs, past_len * N_HEADS, HEAD_W)
                 for c in (cache_a_k, cache_a_v, cache_b_k, cache_b_v))
    out_s = _trunk(x_sample, mod[:, bp:n_c], pos_s, past, p,
                   row_block=(bs, ts), moe_rows=ts, attn_tile=min(past_len * N_HEADS, 2048))
    return (out_p[0], out_s[0]) + out_p[1:] + out_s[1:]
```

```python
import functools
import math

import jax
import jax.numpy as jnp
from jax import lax
from jax.experimental import pallas as pl
from jax.experimental.pallas import tpu as pltpu

D_MODEL = 1024
CHUNK = 64
N_HEADS = 4
HEAD_W = 128
HD_A = 64
HD_B = 128
W_BRANCH = N_HEADS * HEAD_W
N_EXPERTS = 16
EXPERTS_PER_GROUP = 4
N_GROUPS = N_EXPERTS // EXPERTS_PER_GROUP
D_EXPERT = 256
ROPE_THETA = 10000.0
EPS = 1e-6
NEG_INF = -1e30
STICK_DEAD_LOG = -104.0
LOG2_E = 1.4426950408889634
LANES = 128
VMEM_LIMIT = 56 * 1024 * 1024

F32 = jnp.float32
BF16 = jnp.bfloat16
NT_DIMS = (((1,), (1,)), ((), ()))


def _log2(n):
    assert n > 0 and n & (n - 1) == 0, n
    return n.bit_length() - 1


def _split_bf16(x):
    hi = x.astype(BF16)
    return hi, (x - hi.astype(F32)).astype(BF16)


def _mm(a, b, dims=None):
    assert a.dtype == b.dtype, (a.dtype, b.dtype)

    def one(x, y):
        if dims is None:
            return jnp.dot(x, y, preferred_element_type=F32)
        return lax.dot_general(x, y, dims, preferred_element_type=F32)

    if a.dtype != F32:
        return one(a, b)
    a_hi, a_lo = _split_bf16(a)
    b_hi, b_lo = _split_bf16(b)
    return one(a_hi, b_hi) + (one(a_hi, b_lo) + one(a_lo, b_hi))


def _cparams(sem):
    return pltpu.CompilerParams(dimension_semantics=sem, vmem_limit_bytes=VMEM_LIMIT)


def _ada_kernel(c_ref, w_ref, b_ref, o_ref):
    c = c_ref[...]
    a = c * jax.nn.sigmoid(c)

    @pl.when(pl.program_id(0) == 0)
    def _():
        o_ref[0] = _mm(a, w_ref[0]) + b_ref[0]

    @pl.when(pl.program_id(0) > 0)
    def _():
        o_ref[0] = _mm(a.astype(BF16), w_ref[0].astype(BF16)) + b_ref[0]


def _ada_mod(c_all, w_ada, b_ada, tn=1536):
    depth, d, n6 = w_ada.shape
    rows = c_all.shape[0]
    return pl.pallas_call(
        _ada_kernel,
        grid=(depth, n6 // tn),
        in_specs=[
            pl.BlockSpec((rows, d), lambda l, j: (0, 0)),
            pl.BlockSpec((1, d, tn), lambda l, j: (l, 0, j)),
            pl.BlockSpec((1, 1, tn), lambda l, j: (l, 0, j)),
        ],
        out_specs=pl.BlockSpec((1, rows, tn), lambda l, j: (l, 0, j)),
        out_shape=jax.ShapeDtypeStruct((depth, rows, n6), F32),
        compiler_params=_cparams(("arbitrary", "arbitrary")),
        name="ada_mod",
    )(c_all, w_ada, b_ada.reshape(depth, 1, n6))


def _rope128(x, cos, sin_signed, lane):
    fwd = pltpu.roll(x, LANES - HD_A // 2, 1)
    bwd = pltpu.roll(x, HD_A // 2, 1)
    partner = jnp.where((lane & (HD_A - 1)) < HD_A // 2, fwd, bwd)
    return x * cos + partner * sin_signed


def _inproj_kernel(*refs, n_alias, feature_major):
    (x_ref, g_ref, sc_ref, sh_ref, cos_ref, sin_ref, w_ref) = refs[:7]
    outs = refs[7 + n_alias:]
    (ka_c, va_c, kb_c, vb_c, qa_ref, qb_ref, kab_ref, vab_ref, kbb_ref, vbb_ref, ga_ref, gb_ref) = outs
    bb, tt, d = x_ref.shape
    rows = bb * tt
    x = x_ref[...]
    r = lax.rsqrt(jnp.mean(x * x, axis=-1, keepdims=True) + EPS)
    h = (x * r) * g_ref[...] * (1.0 + sc_ref[...]) + sh_ref[...]
    hb = h.reshape(rows, d).astype(w_ref.dtype)
    c32, s32 = cos_ref[...], sin_ref[...]
    if feature_major:
        c32, s32 = c32.T, s32.T
    cos = jnp.concatenate([c32] * (LANES // c32.shape[1]), axis=1)
    sin = jnp.concatenate([-s32, s32] * (LANES // (2 * s32.shape[1])), axis=1)
    if bb > 1:
        cos = jnp.concatenate([cos] * bb, axis=0)
        sin = jnp.concatenate([sin] * bb, axis=0)
    lane = lax.broadcasted_iota(jnp.int32, (rows, LANES), 1)

    def proj(c0, width):
        return _mm(hb, w_ref[:, c0:c0 + width])

    def put(ref, val):
        ref[...] = val.reshape(ref.shape).astype(ref.dtype)

    def put_cache(ref, heads):
        for b in range(bb):
            for hd in range(N_HEADS):
                ref[0, b, pl.ds(hd, tt, stride=N_HEADS), :] = heads[hd][b * tt:(b + 1) * tt]

    def put_feature_major(ref, heads):
        for hd in range(N_HEADS):
            ref[0, hd, 0] = heads[hd].T.astype(ref.dtype)

    def split(val):
        return [val[:, hd * HEAD_W:(hd + 1) * HEAD_W] for hd in range(N_HEADS)]

    w = W_BRANCH
    qa = [_rope128(v, cos, sin, lane) * (HD_A ** -0.5 * LOG2_E) for v in split(proj(0, w))]
    ka = [_rope128(v, cos, sin, lane) for v in split(proj(w, w))]
    va = split(proj(2 * w, w))
    put_cache(ka_c, ka)
    put_cache(va_c, va)
    put(kab_ref, jnp.concatenate(ka, axis=1))
    if feature_major:
        put_feature_major(qa_ref, qa)
        put_feature_major(vab_ref, va)
    else:
        put(qa_ref, jnp.concatenate(qa, axis=1))
        put(vab_ref, jnp.concatenate(va, axis=1))
    put(qb_ref, proj(3 * w, w) * (HD_B ** -0.5))
    kb = proj(4 * w, w)
    put_cache(kb_c, split(kb))
    put(kbb_ref, kb)
    vb = proj(5 * w, w)
    put_cache(vb_c, split(vb))
    put(vbb_ref, vb)
    put(ga_ref, jax.nn.sigmoid(proj(6 * w, d)))
    put(gb_ref, jax.nn.sigmoid(proj(6 * w + d, d)))


def _inproj(x, gain, sc, sh, cos, sin, w, bb, tt, layer, depth, caches, feature_major):
    b, t, d = x.shape
    grid = (b // bb, t // tt)
    xmap = lambda i, j: (i, j, 0)
    modspec = pl.BlockSpec((bb, 1, d), lambda i, j: (i, 0, 0))
    if feature_major:
        tabspec = pl.BlockSpec((cos.shape[0], tt), lambda i, j: (0, j))
    else:
        tabspec = pl.BlockSpec((tt, cos.shape[1]), lambda i, j: (j, 0))
    n_alias = 0 if caches is None else 4
    act = w.dtype

    def out(width, dtype):
        return (pl.BlockSpec((bb, tt, width), xmap), jax.ShapeDtypeStruct((b, t, width), dtype))

    def out_t():
        return (pl.BlockSpec((1, N_HEADS, 1, HEAD_W, tt), lambda i, j: (i, 0, j, 0, 0)),
                jax.ShapeDtypeStruct((b, N_HEADS, t // tt, HEAD_W, tt), BF16))

    cache = (pl.BlockSpec((1, bb, tt * N_HEADS, HEAD_W), lambda i, j: (layer, i, j, 0)),
             jax.ShapeDtypeStruct((depth, b, t * N_HEADS, HEAD_W), F32))
    outs = [cache, cache, cache, cache,
            out_t() if feature_major else out(W_BRANCH, act),
            out(W_BRANCH, act),
            out(W_BRANCH, BF16),
            out_t() if feature_major else out(W_BRANCH, BF16),
            out(W_BRANCH, BF16), out(W_BRANCH, BF16),
            out(d, act), out(d, act)]
    in_specs = [
        pl.BlockSpec((bb, tt, d), xmap),
        pl.BlockSpec((1, 1, d), lambda i, j: (0, 0, 0)),
        modspec, modspec, tabspec, tabspec,
        pl.BlockSpec((None,) + w.shape[1:], lambda i, j: (layer, 0, 0)),
    ] + [pl.BlockSpec(memory_space=pl.ANY)] * n_alias
    args = (x, gain.reshape(1, 1, d), sc, sh, cos, sin, w) + (() if caches is None else tuple(caches))
    return pl.pallas_call(
        functools.partial(_inproj_kernel, n_alias=n_alias, feature_major=feature_major),
        grid=grid,
        in_specs=in_specs,
        out_specs=[o[0] for o in outs],
        out_shape=[o[1] for o in outs],
        input_output_aliases={7 + k: k for k in range(n_alias)},
        compiler_params=_cparams(("arbitrary", "arbitrary")),
        name="inproj",
    )(*args)


def _diff_lambda(lw_ref, lam_init):
    lw = lw_ref[...]
    return (jnp.exp(jnp.sum(lw[0:1] * lw[1:2], axis=-1, keepdims=True))
            - jnp.exp(jnp.sum(lw[2:3] * lw[3:4], axis=-1, keepdims=True)) + lam_init)


def _diff_prompt_kernel(lw_ref, sub_ref, qt_ref, k_ref, vt_ref, o_ref, qz_s, s0_s, s1_s, m_s, l_s, acc_s,
                        *, tq, lam_init):
    i = pl.program_id(2)
    cols = 2 * tq
    qt = qt_ref[0, 0, 0]
    feat = lax.broadcasted_iota(jnp.int32, (HEAD_W, tq), 0)
    zero = jnp.zeros_like(qt)
    qz_s[...] = jnp.concatenate([jnp.where(feat < HD_A, qt, zero), jnp.where(feat >= HD_A, qt, zero)], axis=1)
    m_s[...] = jnp.full((1, cols), NEG_INF, F32)
    l_s[...] = jnp.zeros((1, cols), F32)
    acc_s[...] = jnp.zeros((HEAD_W, cols), F32)

    def scores(j, s_ref):
        start = pl.multiple_of(j * tq, tq)
        s_ref[...] = jnp.dot(k_ref[0, pl.ds(start, tq), :], qz_s[...], preferred_element_type=F32)

    def absorb(j, s_ref, masked):
        s = s_ref[...]
        if masked:
            key = lax.broadcasted_iota(jnp.int32, (tq, cols), 0)
            qry = lax.broadcasted_iota(jnp.int32, (tq, cols), 1)
            qry = jnp.where(qry >= tq, qry - tq, qry)
            s = jnp.where((key >> 6) <= (qry >> 6), s, NEG_INF)
        m_prev = m_s[...]
        m_new = jnp.maximum(m_prev, jnp.max(s, axis=0, keepdims=True))
        alpha = jnp.exp2(m_prev - m_new)
        p = jnp.exp2(s - m_new)
        l_s[...] = alpha * l_s[...] + jnp.sum(p, axis=0, keepdims=True)
        acc_s[...] = alpha * acc_s[...] + jnp.dot(vt_ref[0, 0, j], p.astype(BF16), preferred_element_type=F32)
        m_s[...] = m_new

    scores(0, s0_s)

    def pair(pi, carry):
        j = 2 * pi
        scores(j + 1, s1_s)
        absorb(j, s0_s, False)
        scores(j + 2, s0_s)
        absorb(j + 1, s1_s, False)
        return carry

    lax.fori_loop(0, i // 2, pair, 0)

    @pl.when((i & 1) == 0)
    def _():
        absorb(i, s0_s, True)

    @pl.when((i & 1) == 1)
    def _():
        scores(i, s1_s)
        absorb(i - 1, s0_s, False)
        absorb(i, s1_s, True)

    lam = _diff_lambda(lw_ref, lam_init)
    o = acc_s[...] / l_s[...]
    o = o[:, :tq] - lam * o[:, tq:]
    r = lax.rsqrt(jnp.mean(o * o, axis=0, keepdims=True) + EPS)
    o = (o * r) * sub_ref[...] * (1.0 - lam_init)
    o_ref[0] = o.T.astype(o_ref.dtype)


def _diff_prompt(lw, sub_col, qt, k, vt, *, tq, lam_init):
    b, t, _ = k.shape
    nt = t // tq
    grid = (b, N_HEADS, nt)
    return pl.pallas_call(
        functools.partial(_diff_prompt_kernel, tq=tq, lam_init=lam_init),
        grid=grid,
        in_specs=[
            pl.BlockSpec(lw.shape, lambda bi, h, i: (0, 0)),
            pl.BlockSpec((HEAD_W, 1), lambda bi, h, i: (0, 0)),
            pl.BlockSpec((1, 1, 1, HEAD_W, tq), lambda bi, h, i: (bi, h, i, 0, 0)),
            pl.BlockSpec((1, t, HEAD_W), lambda bi, h, i: (bi, 0, h)),
            pl.BlockSpec((1, 1, nt, HEAD_W, tq), lambda bi, h, i: (bi, h, 0, 0, 0)),
        ],
        out_specs=pl.BlockSpec((1, tq, HEAD_W), lambda bi, h, i: (bi, i, h)),
        out_shape=jax.ShapeDtypeStruct((b, t, W_BRANCH), BF16),
        scratch_shapes=[
            pltpu.VMEM((HEAD_W, 2 * tq), BF16),
            pltpu.VMEM((tq, 2 * tq), F32),
            pltpu.VMEM((tq, 2 * tq), F32),
            pltpu.VMEM((1, 2 * tq), F32),
            pltpu.VMEM((1, 2 * tq), F32),
            pltpu.VMEM((HEAD_W, 2 * tq), F32),
        ],
        compiler_params=_cparams(("arbitrary", "arbitrary", "arbitrary")),
        name="diff_prompt",
    )(lw, sub_col, qt, k, vt)


def _diff_sample_kernel(lw_ref, sub_ref, q_ref, kn_ref, vn_ref, kp_ref, vp_ref, o_ref,
                        qz_s, m_s, l_s, acc_s, *, tq, lam_init):
    kt = pl.program_id(1)
    rows = N_HEADS * 2 * tq
    cdt = qz_s.dtype

    def fold(k_ref, v_ref, new):
        n = k_ref.shape[2] // N_HEADS
        heads = range(N_HEADS)
        mine = [pl.ds(hd * 2 * tq, 2 * tq) for hd in heads]
        scores = []
        for hd in heads:
            k = k_ref[0, 0, pl.ds(hd, n, stride=N_HEADS), :].astype(cdt)
            s = _mm(qz_s[mine[hd], :], k, NT_DIMS)
            if new:
                row = lax.broadcasted_iota(jnp.int32, s.shape, 0)
                col = lax.broadcasted_iota(jnp.int32, s.shape, 1)
                s = jnp.where((col >> 6) <= ((row & (tq - 1)) >> 6), s, NEG_INF)
            scores.append(s)
        m_prev = [m_s[mine[hd], :] for hd in heads]
        m_new = [jnp.maximum(m_prev[hd], jnp.max(scores[hd], axis=-1, keepdims=True)) for hd in heads]
        alpha = [jnp.exp2(m_prev[hd] - m_new[hd]) for hd in heads]
        probs = [jnp.exp2(scores[hd] - m_new[hd][:, :1]) for hd in heads]
        for hd in heads:
            v = v_ref[0, 0, pl.ds(hd, n, stride=N_HEADS), :].astype(cdt)
            l_s[mine[hd], :] = alpha[hd] * l_s[mine[hd], :] + jnp.sum(probs[hd], axis=-1, keepdims=True)
            acc_s[mine[hd], :] = alpha[hd] * acc_s[mine[hd], :] + _mm(probs[hd].astype(cdt), v)
            m_s[mine[hd], :] = m_new[hd]

    @pl.when(kt == 0)
    def _():
        q = q_ref[0]
        lane = lax.broadcasted_iota(jnp.int32, (tq, HEAD_W), 1)
        parts = []
        for hd in range(N_HEADS):
            qh = q[:, hd * HEAD_W:(hd + 1) * HEAD_W]
            zero = jnp.zeros_like(qh)
            parts += [jnp.where(lane < HD_A, qh, zero), jnp.where(lane >= HD_A, qh, zero)]
        qz_s[...] = jnp.concatenate(parts, axis=0)
        m_s[...] = jnp.full((rows, LANES), NEG_INF, F32)
        l_s[...] = jnp.zeros((rows, LANES), F32)
        acc_s[...] = jnp.zeros((rows, HEAD_W), F32)
        fold(kn_ref, vn_ref, True)

    fold(kp_ref, vp_ref, False)

    @pl.when(kt == pl.num_programs(1) - 1)
    def _():
        lam = _diff_lambda(lw_ref, lam_init)
        o = acc_s[...] / l_s[...]
        outs = []
        for hd in range(N_HEADS):
            base = hd * 2 * tq
            oh = o[base:base + tq] - lam * o[base + tq:base + 2 * tq]
            r = lax.rsqrt(jnp.mean(oh * oh, axis=-1, keepdims=True) + EPS)
            outs.append((oh * r) * sub_ref[...] * (1.0 - lam_init))
        o_ref[0] = jnp.concatenate(outs, axis=1).astype(o_ref.dtype)


def _diff_sample(lw, sub_row, q, k_new, v_new, k_past, v_past, layer, *, key_rows, lam_init):
    b, tq, _ = q.shape
    past_rows = k_past.shape[2]
    grid = (b, past_rows // key_rows)
    new_spec = pl.BlockSpec((1, 1, tq * N_HEADS, HEAD_W), lambda bi, kt: (layer, bi, 0, 0))
    past_spec = pl.BlockSpec((1, 1, key_rows, HEAD_W), lambda bi, kt: (layer, bi, kt, 0))
    rows = N_HEADS * 2 * tq
    return pl.pallas_call(
        functools.partial(_diff_sample_kernel, tq=tq, lam_init=lam_init),
        grid=grid,
        in_specs=[
            pl.BlockSpec(lw.shape, lambda bi, kt: (0, 0)),
            pl.BlockSpec((1, HEAD_W), lambda bi, kt: (0, 0)),
            pl.BlockSpec((1, tq, W_BRANCH), lambda bi, kt: (bi, 0, 0)),
            new_spec, new_spec, past_spec, past_spec,
        ],
        out_specs=pl.BlockSpec((1, tq, W_BRANCH), lambda bi, kt: (bi, 0, 0)),
        out_shape=jax.ShapeDtypeStruct((b, tq, W_BRANCH), q.dtype),
        scratch_shapes=[
            pltpu.VMEM((rows, HEAD_W), q.dtype),
            pltpu.VMEM((rows, LANES), F32),
            pltpu.VMEM((rows, LANES), F32),
            pltpu.VMEM((rows, HEAD_W), F32),
        ],
        compiler_params=_cparams(("arbitrary", "arbitrary")),
        name="diff_sample",
    )(lw, sub_row, q, k_new, v_new, k_past, v_past)


def _neg_softplus(z):
    return -(jnp.maximum(z, 0.0) + jnp.log(1.0 + jnp.exp(-jnp.abs(z))))


def _suffix_sums(x, upper):
    hi, lo = _split_bf16(x)
    return (jnp.dot(hi, upper, preferred_element_type=F32)
            + jnp.dot(lo, upper, preferred_element_type=F32))


def _upper(n, dtype=BF16):
    j = lax.broadcasted_iota(jnp.int32, (n, n), 0)
    s = lax.broadcasted_iota(jnp.int32, (n, n), 1)
    return jnp.where(j >= s, 1.0, 0.0).astype(dtype)


def _stick_prompt_kernel(q_ref, kd_ref, vd_ref, kq_ref, vq_ref, kp_ref, vp_ref, o_ref, up_s, c_s, acc_s,
                         *, tq):
    i = pl.program_id(2)
    has_prev = jnp.where(i > 0, 1.0, 0.0)
    row = lax.broadcasted_iota(jnp.int32, (tq, tq), 0)
    col = lax.broadcasted_iota(jnp.int32, (tq, tq), 1)
    earlier = col < row
    q = q_ref[0]
    up_s[...] = _upper(tq)
    z = lax.dot_general(q, kd_ref[0], NT_DIMS, preferred_element_type=F32)
    zq = lax.dot_general(q, kq_ref[0], NT_DIMS, preferred_element_type=F32)
    lneg = jnp.where(earlier, _neg_softplus(z), 0.0)
    lnq = _neg_softplus(zq)
    cs = _suffix_sums(lneg, up_s[...])
    csq = _suffix_sums(lnq, up_s[...])
    c0 = jnp.sum(lneg, axis=-1, keepdims=True)
    a = jnp.where(earlier, jnp.exp(z + cs), 0.0)
    aq = jnp.exp(zq + csq + c0) * has_prev
    acc_s[...] = (jnp.dot(a.astype(BF16), vd_ref[0], preferred_element_type=F32)
                  + jnp.dot(aq.astype(BF16), vq_ref[0], preferred_element_type=F32))
    c1 = c0 + jnp.sum(lnq, axis=-1, keepdims=True) * has_prev
    c_s[...] = c1

    def cond(state):
        j, c_max = state
        return jnp.logical_and(j >= 0, c_max > STICK_DEAD_LOG)

    def body(state):
        j, _ = state
        start = pl.multiple_of(j * tq, tq)
        k = kp_ref[0, pl.ds(start, tq), :]
        v = vp_ref[0, pl.ds(start, tq), :]
        zj = lax.dot_general(q_ref[0], k, NT_DIMS, preferred_element_type=F32)
        ln = _neg_softplus(zj)
        csj = _suffix_sums(ln, up_s[...])
        c = c_s[...]
        aj = jnp.exp(zj + csj + c)
        acc_s[...] += jnp.dot(aj.astype(BF16), v, preferred_element_type=F32)
        c_new = c + jnp.sum(ln, axis=-1, keepdims=True)
        c_s[...] = c_new
        return j - 1, jnp.max(c_new)

    lax.while_loop(cond, body, (i - 2, jnp.max(c1)))
    o_ref[0] = acc_s[...].astype(o_ref.dtype)


def _stick_prompt(q, k, v, *, tq):
    b, t, _ = q.shape
    grid = (b, N_HEADS, t // tq)
    tile = pl.BlockSpec((1, tq, HEAD_W), lambda bi, h, i: (bi, i, h))
    before = pl.BlockSpec((1, tq, HEAD_W), lambda bi, h, i: (bi, jnp.maximum(i - 1, 0), h))
    whole = pl.BlockSpec((1, t, HEAD_W), lambda bi, h, i: (bi, 0, h))
    return pl.pallas_call(
        functools.partial(_stick_prompt_kernel, tq=tq),
        grid=grid,
        in_specs=[tile, tile, tile, before, before, whole, whole],
        out_specs=tile,
        out_shape=jax.ShapeDtypeStruct((b, t, W_BRANCH), BF16),
        scratch_shapes=[
            pltpu.VMEM((tq, tq), BF16),
            pltpu.VMEM((tq, 1), F32),
            pltpu.VMEM((tq, HEAD_W), F32),
        ],
        compiler_params=_cparams(("arbitrary", "arbitrary", "arbitrary")),
        name="stick_prompt",
    )(q, k, v, k, v, k, v)


STICK_RECENT_EXTRA = 2


def _stick_recent_kernel(q_ref, kd_ref, vd_ref, kq_ref, vq_ref, *rest, tq):
    older = rest[:2 * STICK_RECENT_EXTRA]
    o_ref, rest_ref, c_s, acc_s, worst_s = rest[2 * STICK_RECENT_EXTRA:]
    i = pl.program_id(1)
    has_prev = jnp.where(i > 0, 1.0, 0.0)
    row = lax.broadcasted_iota(jnp.int32, (tq, tq), 0)
    col = lax.broadcasted_iota(jnp.int32, (tq, tq), 1)
    earlier = col < row
    upper = _upper(tq)
    heads = [slice(hd * HEAD_W, (hd + 1) * HEAD_W) for hd in range(N_HEADS)]
    worst = None
    for hd, sl in enumerate(heads):
        q = q_ref[0, :, sl]
        z = lax.dot_general(q, kd_ref[0, :, sl], NT_DIMS, preferred_element_type=F32)
        zq = lax.dot_general(q, kq_ref[0, :, sl], NT_DIMS, preferred_element_type=F32)
        lneg = jnp.where(earlier, _neg_softplus(z), 0.0)
        lnq = _neg_softplus(zq)
        cs = _suffix_sums(lneg, upper)
        csq = _suffix_sums(lnq, upper)
        c0 = jnp.sum(lneg, axis=-1, keepdims=True)
        a = jnp.where(earlier, jnp.exp(z + cs), 0.0)
        aq = jnp.exp(zq + csq + c0) * has_prev
        acc_s[:, sl] = (jnp.dot(a.astype(BF16), vd_ref[0, :, sl], preferred_element_type=F32)
                        + jnp.dot(aq.astype(BF16), vq_ref[0, :, sl], preferred_element_type=F32))
        c1 = c0 + jnp.sum(lnq, axis=-1, keepdims=True) * has_prev
        c_s[hd] = c1
        c_max = jnp.max(c1)
        worst = c_max if worst is None else jnp.maximum(worst, c_max)
    worst_s[0] = worst

    for e in range(STICK_RECENT_EXTRA):
        k_ref, v_ref = older[2 * e], older[2 * e + 1]

        @pl.when(jnp.logical_and(worst_s[0] > STICK_DEAD_LOG, i >= e + 2))
        def _(k_ref=k_ref, v_ref=v_ref):
            worst_e = None
            for hd, sl in enumerate(heads):
                zj = lax.dot_general(q_ref[0, :, sl], k_ref[0, :, sl], NT_DIMS, preferred_element_type=F32)
                ln = _neg_softplus(zj)
                csj = _suffix_sums(ln, upper)
                c = c_s[hd]
                aj = jnp.exp(zj + csj + c)
                acc_s[:, sl] += jnp.dot(aj.astype(BF16), v_ref[0, :, sl], preferred_element_type=F32)
                c_new = c + jnp.sum(ln, axis=-1, keepdims=True)
                c_s[hd] = c_new
                c_max = jnp.max(c_new)
                worst_e = c_max if worst_e is None else jnp.maximum(worst_e, c_max)
            worst_s[0] = worst_e

    o_ref[0] = acc_s[...].astype(o_ref.dtype)
    left = jnp.where(i >= STICK_RECENT_EXTRA + 2, worst_s[0], 2.0 * STICK_DEAD_LOG)
    rest_ref[...] = jnp.full(rest_ref.shape, left, F32)


def _stick_recent(q, k, v, *, tq):
    b, t, _ = q.shape
    nt = t // tq
    tile = pl.BlockSpec((1, tq, W_BRANCH), lambda bi, i: (bi, i, 0))

    def back(n):
        return pl.BlockSpec((1, tq, W_BRANCH), lambda bi, i: (bi, jnp.maximum(i - n, 0), 0))

    n_older = STICK_RECENT_EXTRA
    return pl.pallas_call(
        functools.partial(_stick_recent_kernel, tq=tq),
        grid=(b, nt),
        in_specs=[tile, tile, tile, back(1), back(1)] + [back(2 + e // 2) for e in range(2 * n_older)],
        out_specs=[tile, pl.BlockSpec((1, 1, 8, LANES), lambda bi, i: (bi, i, 0, 0))],
        out_shape=[jax.ShapeDtypeStruct((b, t, W_BRANCH), BF16),
                   jax.ShapeDtypeStruct((b, nt, 8, LANES), F32)],
        scratch_shapes=[
            pltpu.VMEM((N_HEADS, tq, 1), F32),
            pltpu.VMEM((tq, W_BRANCH), F32),
            pltpu.SMEM((1,), F32),
        ],
        compiler_params=_cparams(("arbitrary", "arbitrary")),
        name="stick_recent",
    )(q, k, v, k, v, *([k, v] * n_older))


def _stick_sample_kernel(q_ref, kn_ref, vn_ref, kp_ref, vp_ref, o_ref, rest_ref,
                         q_s, up_s, c_s, acc_s, live_s, *, tq, sub_rows):
    kt = pl.program_id(1)
    rows = N_HEADS * tq

    def fold(k, v, new):
        n = k.shape[0]
        z = _mm(q_s[...], k.astype(q_s.dtype), NT_DIMS)
        row = lax.broadcasted_iota(jnp.int32, (rows, n), 0)
        col = lax.broadcasted_iota(jnp.int32, (rows, n), 1)
        keep = (row >> _log2(tq)) == (col & (N_HEADS - 1))
        if new:
            keep = jnp.logical_and(keep, (col >> 2) < (row & (tq - 1)))
        ln = jnp.where(keep, _neg_softplus(z), 0.0)
        cs = _suffix_sums(ln, _upper(n, up_s.dtype) if new else up_s[...])
        c = c_s[...]
        a = jnp.where(keep, jnp.exp(z + cs + c), 0.0)
        acc_s[...] += _mm(a.astype(q_s.dtype), v.astype(q_s.dtype))
        c_new = c + jnp.sum(ln, axis=-1, keepdims=True)
        c_s[...] = c_new
        live_s[0] = (jnp.max(c_new) > STICK_DEAD_LOG).astype(jnp.int32)

    @pl.when(kt == 0)
    def _():
        q = q_ref[0]
        q_s[...] = jnp.concatenate([q[:, hd * HEAD_W:(hd + 1) * HEAD_W] for hd in range(N_HEADS)], axis=0)
        up_s[...] = _upper(sub_rows, up_s.dtype)
        c_s[...] = jnp.zeros((rows, 1), F32)
        acc_s[...] = jnp.zeros((rows, HEAD_W), F32)
        fold(kn_ref[0, 0], vn_ref[0, 0], True)

    n_sub = kp_ref.shape[2] // sub_rows
    for sb in reversed(range(n_sub)):
        @pl.when(live_s[0] > 0)
        def _(sb=sb):
            sl = slice(sb * sub_rows, (sb + 1) * sub_rows)
            fold(kp_ref[0, 0, sl, :], vp_ref[0, 0, sl, :], False)

    @pl.when(kt == pl.num_programs(1) - 1)
    def _():
        acc = acc_s[...]
        o_ref[0] = jnp.concatenate([acc[hd * tq:(hd + 1) * tq] for hd in range(N_HEADS)],
                                   axis=1).astype(o_ref.dtype)
        rest_ref[...] = jnp.full(rest_ref.shape, jnp.max(c_s[...]), F32)


def _stick_sample(q, k_new, v_new, k_past, v_past, layer, *, key_rows, sub_rows, newest_only):
    b, tq, _ = q.shape
    n_blocks = k_past.shape[2] // key_rows
    grid = (b, 1 if newest_only else n_blocks)
    new_spec = pl.BlockSpec((1, 1, tq * N_HEADS, HEAD_W), lambda bi, kt: (layer, bi, 0, 0))
    past_spec = pl.BlockSpec((1, 1, key_rows, HEAD_W), lambda bi, kt: (layer, bi, n_blocks - 1 - kt, 0))
    rows = N_HEADS * tq
    return pl.pallas_call(
        functools.partial(_stick_sample_kernel, tq=tq, sub_rows=sub_rows),
        grid=grid,
        in_specs=[pl.BlockSpec((1, tq, W_BRANCH), lambda bi, kt: (bi, 0, 0)),
                  new_spec, new_spec, past_spec, past_spec],
        out_specs=[pl.BlockSpec((1, tq, W_BRANCH), lambda bi, kt: (bi, 0, 0)),
                   pl.BlockSpec((1, 8, LANES), lambda bi, kt: (bi, 0, 0))],
        out_shape=[jax.ShapeDtypeStruct((b, tq, W_BRANCH), q.dtype),
                   jax.ShapeDtypeStruct((b, 8, LANES), F32)],
        scratch_shapes=[
            pltpu.VMEM((rows, HEAD_W), q.dtype),
            pltpu.VMEM((sub_rows, sub_rows), BF16),
            pltpu.VMEM((rows, 1), F32),
            pltpu.VMEM((rows, HEAD_W), F32),
            pltpu.SMEM((1,), jnp.int32),
        ],
        compiler_params=_cparams(("arbitrary", "arbitrary")),
        name="stick_sample",
    )(q, k_new, v_new, k_past, v_past)


def _router_gates_t(logits_t, bias_col):
    n = EXPERTS_PER_GROUP
    aff = jax.nn.sigmoid(logits_t)
    sel = aff + bias_col
    pos = [sel[n * k:n * (k + 1)] for k in range(n)]
    in_top2 = []
    for j in range(n):
        rank = jnp.zeros(pos[j].shape, jnp.int32)
        for i in range(n):
            if i != j:
                ahead = (pos[i] >= pos[j]) if i < j else (pos[i] > pos[j])
                rank = rank + jnp.where(ahead, 1, 0)
        in_top2.append(rank < 2)
    score = sum(jnp.where(in_top2[j], pos[j], 0.0) for j in range(n))
    grp = lax.broadcasted_iota(jnp.int32, score.shape, 0)
    beaten = jnp.zeros(score.shape, jnp.int32)
    for g2 in range(N_GROUPS):
        other = score[g2:g2 + 1]
        ahead = jnp.logical_or(other > score, jnp.logical_and(other == score, g2 < grp))
        beaten = beaten + jnp.where(ahead, 1, 0)
    chosen = beaten == 0
    w = [jnp.where(jnp.logical_and(in_top2[j], chosen), aff[n * j:n * (j + 1)], 0.0) for j in range(n)]
    total = jnp.sum(sum(w), axis=0, keepdims=True)
    return jnp.concatenate([wj / total for wj in w], axis=0)


def _mixout_kernel(x_ref, oa_ref, ob_ref, ga_ref, gb_ref, g1_ref, sc_ref, sh_ref, gain_ref,
                   wa_ref, wb_ref, wo_ref, wr_ref, br_ref, x1_ref, h2_ref, gates_ref):
    bb, tt, d = x_ref.shape
    rows = bb * tt
    ya = _mm(oa_ref[...].reshape(rows, W_BRANCH), wa_ref[...])
    yb = _mm(ob_ref[...].reshape(rows, W_BRANCH), wb_ref[...])
    y = (ga_ref[...].reshape(rows, d).astype(F32) * ya + gb_ref[...].reshape(rows, d).astype(F32) * yb)
    mix = _mm(y.astype(wo_ref.dtype), wo_ref[...])
    x1 = x_ref[...] + g1_ref[...] * mix.reshape(bb, tt, d)
    x1_ref[...] = x1
    r = lax.rsqrt(jnp.mean(x1 * x1, axis=-1, keepdims=True) + EPS)
    h2 = ((x1 * r) * gain_ref[...] * (1.0 + sc_ref[...]) + sh_ref[...]).reshape(rows, d)
    h2_ref[...] = h2.reshape(bb, tt, d).astype(h2_ref.dtype)
    logits_t = _mm(wr_ref[...], h2.astype(wr_ref.dtype), NT_DIMS)
    gates_t = _router_gates_t(logits_t, br_ref[...])
    pad = jnp.zeros((LANES - N_EXPERTS, rows), F32)
    gates_ref[...] = jnp.concatenate([gates_t, pad], axis=0).T.reshape(bb, tt, LANES)


def _mixout(x, oa, ob, ga, gb, g1, sc2, sh2, gain, wa, wb, wo, wr, br, bb, tt, layer):
    b, t, d = x.shape
    grid = (b // bb, t // tt)
    xmap = lambda i, j: (i, j, 0)
    modspec = pl.BlockSpec((bb, 1, d), lambda i, j: (i, 0, 0))

    def full(a):
        return pl.BlockSpec(a.shape, lambda i, j: (0,) * a.ndim)

    def of_layer(a):
        return pl.BlockSpec((None,) + a.shape[1:], lambda i, j: (layer,) + (0,) * (a.ndim - 1))

    return pl.pallas_call(
        _mixout_kernel,
        grid=grid,
        in_specs=[
            pl.BlockSpec((bb, tt, d), xmap),
            pl.BlockSpec((bb, tt, W_BRANCH), xmap), pl.BlockSpec((bb, tt, W_BRANCH), xmap),
            pl.BlockSpec((bb, tt, d), xmap), pl.BlockSpec((bb, tt, d), xmap),
            modspec, modspec, modspec,
            pl.BlockSpec((1, 1, d), lambda i, j: (0, 0, 0)),
            of_layer(wa), of_layer(wb), of_layer(wo), full(wr), full(br),
        ],
        out_specs=[pl.BlockSpec((bb, tt, d), xmap), pl.BlockSpec((bb, tt, d), xmap),
                   pl.BlockSpec((bb, tt, LANES), xmap)],
        out_shape=[jax.ShapeDtypeStruct((b, t, d), F32), jax.ShapeDtypeStruct((b, t, d), BF16),
                   jax.ShapeDtypeStruct((b, t, LANES), F32)],
        compiler_params=_cparams(("arbitrary", "arbitrary")),
        name="mixout",
    )(x, oa, ob, ga, gb, g1, sc2, sh2, gain.reshape(1, 1, d), wa, wb, wo, wr, br)


def _moe_kernel(h_ref, gates_ref, x1_ref, g2_ref, gain_ref, wg_ref, wu_ref, wd_ref, o_ref, acc_s,
                *, final_norm):
    bb, tt, d = h_ref.shape
    rows = bb * tt
    grp = pl.program_id(2)
    h = h_ref[...].reshape(rows, d)
    gates = gates_ref[...].reshape(rows, LANES)
    lane = lax.broadcasted_iota(jnp.int32, (rows, LANES), 1)

    @pl.when(grp == 0)
    def _():
        acc_s[...] = jnp.zeros(acc_s.shape, F32)

    hidden = []
    for k in range(EXPERTS_PER_GROUP):
        g = jnp.dot(h, wg_ref[k], preferred_element_type=F32)
        u = jnp.dot(h, wu_ref[k], preferred_element_type=F32)
        gate = jnp.sum(jnp.where(lane == EXPERTS_PER_GROUP * k + grp, gates, 0.0), axis=-1, keepdims=True)
        hidden.append(((g * jax.nn.sigmoid(g)) * u * gate).astype(BF16))
    acc_s[...] += jnp.dot(jnp.concatenate(hidden, axis=1), wd_ref[...], preferred_element_type=F32)

    @pl.when(grp == N_GROUPS - 1)
    def _():
        x2 = x1_ref[...] + g2_ref[...] * acc_s[...].reshape(bb, tt, d)
        if final_norm:
            r = lax.rsqrt(jnp.mean(x2 * x2, axis=-1, keepdims=True) + EPS)
            x2 = (x2 * r) * gain_ref[...]
        o_ref[...] = x2


def _moe(h2, gates, x1, g2, gain, wg, wu, wd, bb, tt, layer, final_norm):
    b, t, d = x1.shape
    grid = (b // bb, t // tt, N_GROUPS)
    xmap = lambda i, j, e: (i, j, 0)
    n = EXPERTS_PER_GROUP
    return pl.pallas_call(
        functools.partial(_moe_kernel, final_norm=final_norm),
        grid=grid,
        in_specs=[
            pl.BlockSpec((bb, tt, d), xmap),
            pl.BlockSpec((bb, tt, LANES), xmap),
            pl.BlockSpec((bb, tt, d), xmap),
            pl.BlockSpec((bb, 1, d), lambda i, j, e: (i, 0, 0)),
            pl.BlockSpec((1, 1, d), lambda i, j, e: (0, 0, 0)),
            pl.BlockSpec((None, n, d, D_EXPERT), lambda i, j, e: (layer, e, 0, 0)),
            pl.BlockSpec((None, n, d, D_EXPERT), lambda i, j, e: (layer, e, 0, 0)),
            pl.BlockSpec((None, None, n * D_EXPERT, d), lambda i, j, e: (layer, e, 0, 0)),
        ],
        out_specs=pl.BlockSpec((bb, tt, d), xmap),
        out_shape=jax.ShapeDtypeStruct((b, t, d), F32),
        scratch_shapes=[pltpu.VMEM((bb * tt, d), F32)],
        compiler_params=_cparams(("arbitrary", "arbitrary", "arbitrary")),
        name="moe",
    )(h2, gates, x1, g2, gain.reshape(1, 1, d), wg, wu, wd)


def _rope_tables(pos, freq_major):
    half = HD_A // 2
    inv = ROPE_THETA ** (-jnp.arange(half, dtype=F32) / half)
    if freq_major:
        ang = inv[:, None] * pos.astype(F32)[None, :]
    else:
        ang = pos.astype(F32)[:, None] * inv[None, :]
    return jnp.cos(ang), jnp.sin(ang)


def _trunk(x, mod, pos, past, p, *, row_block, moe_rows, attn_tile):
    b, t, d = x.shape
    bb, tt = row_block
    depth = p["norm_mix"].shape[0]
    prompt = past is None
    cos, sin = _rope_tables(pos, freq_major=past is None)
    caches = None
    for l in range(depth):
        wl = p["f32"] if (not prompt and l == 0) else p["bf16"]
        sh1, sc1, g1, sh2, sc2, g2 = [mod[l, :, i][:, None, :] for i in range(6)]
        outs = _inproj(x, p["norm_mix"][l], sc1, sh1, cos, sin, wl["w_in"], bb, tt, l, depth,
                       caches, feature_major=prompt)
        caches = outs[:4]
        qa, qb, kab, vab, kbb, vbb, ga, gb = outs[4:]
        lam_init = 0.8 - 0.6 * math.exp(-0.3 * l)
        lw, sub = p["a_lambda"][l], p["a_subln"][l]
        if prompt:
            oa = _diff_prompt(lw, sub[:, None], qa, kab, vab, tq=attn_tile, lam_init=lam_init)
            stick_tile = min(t, 256)
            ob, rest = _stick_recent(qb, kbb, vbb, tq=stick_tile)
            ob = lax.cond(jnp.max(rest) > STICK_DEAD_LOG,
                          lambda: _stick_prompt(qb, kbb, vbb, tq=stick_tile), lambda: ob)
        else:
            oa = _diff_sample(lw, sub[None, :], qa, caches[0], caches[1], past[0], past[1], l,
                              key_rows=min(past[0].shape[2], 4 * attn_tile), lam_init=lam_init)
            stick = functools.partial(_stick_sample, qb, caches[2], caches[3], past[2], past[3], l,
                                      key_rows=attn_tile, sub_rows=min(attn_tile, 512))
            ob, rest = stick(newest_only=True)
            if past[2].shape[2] > attn_tile:
                ob = lax.cond(jnp.max(rest) > STICK_DEAD_LOG,
                              lambda: stick(newest_only=False)[0], lambda: ob)
        x1, h2, gates = _mixout(x, oa, ob, ga, gb, g1, sc2, sh2, p["norm_ffn"][l],
                                wl["w_proj_a"], wl["w_proj_b"], wl["w_out"],
                                wl["w_router"], p["b_router"], bb, tt, l)
        x = _moe(h2, gates, x1, g2, p["norm_final"], p["w_e_gate"], p["w_e_up"],
                 p["w_e_down"], bb, moe_rows, l, final_norm=(l == depth - 1))
    return (x,) + tuple(c.reshape(depth, b, t, N_HEADS, HEAD_W) for c in caches)


def kernel(x_prompt, x_sample, cache_a_k, cache_a_v, cache_b_k, cache_b_v, c_prompt, c_sample,
           w_in, w_proj_a, w_proj_b, w_out, a_lambda, a_subln, w_ada, b_ada,
           norm_mix, norm_ffn, norm_final, w_router, b_router, w_e_gate, w_e_up, w_e_down):
    d = x_prompt.shape[-1]
    bp, tp = x_prompt.shape[:2]
    bs, ts = x_sample.shape[:2]
    depth = w_in.shape[0]
    past_len = cache_a_k.shape[2]
    assert d == D_MODEL and past_len % CHUNK == 0 and ts <= CHUNK

    wr = w_router.T.reshape(N_GROUPS, EXPERTS_PER_GROUP, d).transpose(1, 0, 2).reshape(N_EXPERTS, d)
    br = b_router.reshape(N_GROUPS, EXPERTS_PER_GROUP).T.reshape(N_EXPERTS, 1)
    mixer_f32 = dict(w_in=w_in, w_proj_a=w_proj_a, w_proj_b=w_proj_b, w_out=w_out, w_router=wr)
    p = dict(
        f32=mixer_f32, bf16={k: v.astype(BF16) for k, v in mixer_f32.items()},
        a_lambda=a_lambda, a_subln=a_subln,
        norm_mix=norm_mix, norm_ffn=norm_ffn, norm_final=norm_final, b_router=br,
        w_e_gate=w_e_gate.astype(BF16), w_e_up=w_e_up.astype(BF16),
        w_e_down=w_e_down.astype(BF16).reshape(depth, N_GROUPS, EXPERTS_PER_GROUP * D_EXPERT, d),
    )

    n_c = bp + bs
    rows = -(-n_c // 8) * 8
    c_all = jnp.zeros((rows, d), F32).at[:bp].set(c_prompt).at[bp:n_c].set(c_sample)
    mod = _ada_mod(c_all, w_ada, b_ada).reshape(depth, rows, 6, d)

    pos_p = jnp.arange(tp, dtype=jnp.int32)
    pos_s = past_len + jnp.arange(ts, dtype=jnp.int32)
    tile_p = min(tp, 512)
    out_p = _trunk(x_prompt, mod[:, :bp], pos_p, None, p,
                   row_block=(1, tile_p), moe_rows=min(tp, 1024), attn_tile=tile_p)
    past = tuple(c.reshape(depth, bs, past_len * N_HEADS, HEAD_W)
                 for c in (cache_a_k, cache_a_v, cache_b_k, cache_b_v))
    out_s = _trunk(x_sample, mod[:, bp:n_c], pos_s, past, p,
                   row_block=(bs, ts), moe_rows=ts, attn_tile=min(past_len * N_HEADS, 2048))
    return (out_p[0], out_s[0]) + out_p[1:] + out_s[1:]
```

```python
import functools
import math

import jax
import jax.numpy as jnp
from jax import lax
from jax.experimental import pallas as pl
from jax.experimental.pallas import tpu as pltpu

D_MODEL = 1024
CHUNK = 64
N_HEADS = 4
HEAD_W = 128
HD_A = 64
HD_B = 128
W_BRANCH = N_HEADS * HEAD_W
N_EXPERTS = 16
EXPERTS_PER_GROUP = 4
N_GROUPS = N_EXPERTS // EXPERTS_PER_GROUP
D_EXPERT = 256
ROPE_THETA = 10000.0
EPS = 1e-6
NEG_INF = -1e30
STICK_DEAD_LOG = -104.0
LOG2_E = 1.4426950408889634
LANES = 128
VMEM_LIMIT = 56 * 1024 * 1024

F32 = jnp.float32
BF16 = jnp.bfloat16
NT_DIMS = (((1,), (1,)), ((), ()))


def _log2(n):
    assert n > 0 and n & (n - 1) == 0, n
    return n.bit_length() - 1


def _split_bf16(x):
    hi = x.astype(BF16)
    return hi, (x - hi.astype(F32)).astype(BF16)


def _mm(a, b, dims=None):
    assert a.dtype == b.dtype, (a.dtype, b.dtype)

    def one(x, y):
        if dims is None:
            return jnp.dot(x, y, preferred_element_type=F32)
        return lax.dot_general(x, y, dims, preferred_element_type=F32)

    if a.dtype != F32:
        return one(a, b)
    a_hi, a_lo = _split_bf16(a)
    b_hi, b_lo = _split_bf16(b)
    return one(a_hi, b_hi) + (one(a_hi, b_lo) + one(a_lo, b_hi))


def _cparams(sem):
    return pltpu.CompilerParams(dimension_semantics=sem, vmem_limit_bytes=VMEM_LIMIT)


def _ada_kernel(c_ref, w_ref, b_ref, o_ref):
    c = c_ref[...]
    a = c * jax.nn.sigmoid(c)

    @pl.when(pl.program_id(0) == 0)
    def _():
        o_ref[0] = _mm(a, w_ref[0]) + b_ref[0]

    @pl.when(pl.program_id(0) > 0)
    def _():
        o_ref[0] = _mm(a.astype(BF16), w_ref[0].astype(BF16)) + b_ref[0]


def _ada_mod(c_all, w_ada, b_ada, tn=1536):
    depth, d, n6 = w_ada.shape
    rows = c_all.shape[0]
    return pl.pallas_call(
        _ada_kernel,
        grid=(depth, n6 // tn),
        in_specs=[
            pl.BlockSpec((rows, d), lambda l, j: (0, 0)),
            pl.BlockSpec((1, d, tn), lambda l, j: (l, 0, j)),
            pl.BlockSpec((1, 1, tn), lambda l, j: (l, 0, j)),
        ],
        out_specs=pl.BlockSpec((1, rows, tn), lambda l, j: (l, 0, j)),
        out_shape=jax.ShapeDtypeStruct((depth, rows, n6), F32),
        compiler_params=_cparams(("arbitrary", "arbitrary")),
        name="ada_mod",
    )(c_all, w_ada, b_ada.reshape(depth, 1, n6))


def _rope128(x, cos, sin_signed, lane):
    fwd = pltpu.roll(x, LANES - HD_A // 2, 1)
    bwd = pltpu.roll(x, HD_A // 2, 1)
    partner = jnp.where((lane & (HD_A - 1)) < HD_A // 2, fwd, bwd)
    return x * cos + partner * sin_signed


def _inproj_kernel(*refs, n_alias, feature_major):
    (x_ref, g_ref, sc_ref, sh_ref, cos_ref, sin_ref, w_ref) = refs[:7]
    outs = refs[7 + n_alias:]
    (ka_c, va_c, kb_c, vb_c, qa_ref, qb_ref, kab_ref, vab_ref, kbb_ref, vbb_ref, ga_ref, gb_ref) = outs
    bb, tt, d = x_ref.shape
    rows = bb * tt
    x = x_ref[...]
    r = lax.rsqrt(jnp.mean(x * x, axis=-1, keepdims=True) + EPS)
    h = (x * r) * g_ref[...] * (1.0 + sc_ref[...]) + sh_ref[...]
    hb = h.reshape(rows, d).astype(w_ref.dtype)
    c32, s32 = cos_ref[...], sin_ref[...]
    if feature_major:
        c32, s32 = c32.T, s32.T
    cos = jnp.concatenate([c32] * (LANES // c32.shape[1]), axis=1)
    sin = jnp.concatenate([-s32, s32] * (LANES // (2 * s32.shape[1])), axis=1)
    if bb > 1:
        cos = jnp.concatenate([cos] * bb, axis=0)
        sin = jnp.concatenate([sin] * bb, axis=0)
    lane = lax.broadcasted_iota(jnp.int32, (rows, LANES), 1)

    def proj(c0, width):
        return _mm(hb, w_ref[:, c0:c0 + width])

    def put(ref, val):
        ref[...] = val.reshape(ref.shape).astype(ref.dtype)

    def put_cache(ref, heads):
        for b in range(bb):
            for hd in range(N_HEADS):
                ref[0, b, pl.ds(hd, tt, stride=N_HEADS), :] = heads[hd][b * tt:(b + 1) * tt]

    def put_feature_major(ref, heads):
        for hd in range(N_HEADS):
            ref[0, hd, 0] = heads[hd].T.astype(ref.dtype)

    def split(val):
        return [val[:, hd * HEAD_W:(hd + 1) * HEAD_W] for hd in range(N_HEADS)]

    w = W_BRANCH
    qa = [_rope128(v, cos, sin, lane) * (HD_A ** -0.5 * LOG2_E) for v in split(proj(0, w))]
    ka = [_rope128(v, cos, sin, lane) for v in split(proj(w, w))]
    va = split(proj(2 * w, w))
    put_cache(ka_c, ka)
    put_cache(va_c, va)
    put(kab_ref, jnp.concatenate(ka, axis=1))
    if feature_major:
        put_feature_major(qa_ref, qa)
        put_feature_major(vab_ref, va)
    else:
        put(qa_ref, jnp.concatenate(qa, axis=1))
        put(vab_ref, jnp.concatenate(va, axis=1))
    put(qb_ref, proj(3 * w, w) * (HD_B ** -0.5))
    kb = proj(4 * w, w)
    put_cache(kb_c, split(kb))
    put(kbb_ref, kb)
    vb = proj(5 * w, w)
    put_cache(vb_c, split(vb))
    put(vbb_ref, vb)
    put(ga_ref, jax.nn.sigmoid(proj(6 * w, d)))
    put(gb_ref, jax.nn.sigmoid(proj(6 * w + d, d)))


def _inproj(x, gain, sc, sh, cos, sin, w, bb, tt, layer, depth, caches, feature_major):
    b, t, d = x.shape
    grid = (b // bb, t // tt)
    xmap = lambda i, j: (i, j, 0)
    modspec = pl.BlockSpec((bb, 1, d), lambda i, j: (i, 0, 0))
    if feature_major:
        tabspec = pl.BlockSpec((cos.shape[0], tt), lambda i, j: (0, j))
    else:
        tabspec = pl.BlockSpec((tt, cos.shape[1]), lambda i, j: (j, 0))
    n_alias = 0 if caches is None else 4
    act = w.dtype

    def out(width, dtype):
        return (pl.BlockSpec((bb, tt, width), xmap), jax.ShapeDtypeStruct((b, t, width), dtype))

    def out_t():
        return (pl.BlockSpec((1, N_HEADS, 1, HEAD_W, tt), lambda i, j: (i, 0, j, 0, 0)),
                jax.ShapeDtypeStruct((b, N_HEADS, t // tt, HEAD_W, tt), BF16))

    cache = (pl.BlockSpec((1, bb, tt * N_HEADS, HEAD_W), lambda i, j: (layer, i, j, 0)),
             jax.ShapeDtypeStruct((depth, b, t * N_HEADS, HEAD_W), F32))
    outs = [cache, cache, cache, cache,
            out_t() if feature_major else out(W_BRANCH, act),
            out(W_BRANCH, act),
            out(W_BRANCH, BF16),
            out_t() if feature_major else out(W_BRANCH, BF16),
            out(W_BRANCH, BF16), out(W_BRANCH, BF16),
            out(d, act), out(d, act)]
    in_specs = [
        pl.BlockSpec((bb, tt, d), xmap),
        pl.BlockSpec((1, 1, d), lambda i, j: (0, 0, 0)),
        modspec, modspec, tabspec, tabspec,
        pl.BlockSpec((None,) + w.shape[1:], lambda i, j: (layer, 0, 0)),
    ] + [pl.BlockSpec(memory_space=pl.ANY)] * n_alias
    args = (x, gain.reshape(1, 1, d), sc, sh, cos, sin, w) + (() if caches is None else tuple(caches))
    return pl.pallas_call(
        functools.partial(_inproj_kernel, n_alias=n_alias, feature_major=feature_major),
        grid=grid,
        in_specs=in_specs,
        out_specs=[o[0] for o in outs],
        out_shape=[o[1] for o in outs],
        input_output_aliases={7 + k: k for k in range(n_alias)},
        compiler_params=_cparams(("arbitrary", "arbitrary")),
        name="inproj",
    )(*args)


def _diff_lambda(lw_ref, lam_init):
    lw = lw_ref[...]
    return (jnp.exp(jnp.sum(lw[0:1] * lw[1:2], axis=-1, keepdims=True))
            - jnp.exp(jnp.sum(lw[2:3] * lw[3:4], axis=-1, keepdims=True)) + lam_init)


def _diff_prompt_kernel(lw_ref, sub_ref, hide_ref, qt_ref, k_ref, vt_ref, o_ref,
                        qz_s, s0_s, s1_s, m_s, l_s, acc_s, *, tq, lam_init):
    i = pl.program_id(2)
    cols = 2 * tq
    qt = qt_ref[0, 0, 0]
    feat = lax.broadcasted_iota(jnp.int32, (HEAD_W, tq), 0)
    zero = jnp.zeros_like(qt)
    qz_s[...] = jnp.concatenate([jnp.where(feat < HD_A, qt, zero), jnp.where(feat >= HD_A, qt, zero)], axis=1)
    m_s[...] = jnp.full((1, cols), NEG_INF, F32)
    l_s[...] = jnp.zeros((1, cols), F32)
    acc_s[...] = jnp.zeros((HEAD_W, cols), F32)

    def scores(j, s_ref):
        start = pl.multiple_of(j * tq, tq)
        s_ref[...] = jnp.dot(k_ref[0, pl.ds(start, tq), :], qz_s[...], preferred_element_type=F32)

    def absorb(j, s_ref, masked):
        s = s_ref[...]
        if masked:
            s = s + hide_ref[...]
        m_prev = m_s[...]
        m_new = jnp.maximum(m_prev, jnp.max(s, axis=0, keepdims=True))
        alpha = jnp.exp2(m_prev - m_new)
        p = jnp.exp2(s - m_new)
        l_s[...] = alpha * l_s[...] + jnp.sum(p, axis=0, keepdims=True)
        acc_s[...] = alpha * acc_s[...] + jnp.dot(vt_ref[0, 0, j], p.astype(BF16), preferred_element_type=F32)
        m_s[...] = m_new

    scores(0, s0_s)

    def pair(pi, carry):
        j = 2 * pi
        scores(j + 1, s1_s)
        absorb(j, s0_s, False)
        scores(j + 2, s0_s)
        absorb(j + 1, s1_s, False)
        return carry

    lax.fori_loop(0, i // 2, pair, 0)

    @pl.when((i & 1) == 0)
    def _():
        absorb(i, s0_s, True)

    @pl.when((i & 1) == 1)
    def _():
        scores(i, s1_s)
        absorb(i - 1, s0_s, False)
        absorb(i, s1_s, True)

    lam = _diff_lambda(lw_ref, lam_init)
    o = acc_s[...] / l_s[...]
    o = o[:, :tq] - lam * o[:, tq:]
    r = lax.rsqrt(jnp.mean(o * o, axis=0, keepdims=True) + EPS)
    o = (o * r) * sub_ref[...] * (1.0 - lam_init)
    o_ref[0] = o.T.astype(o_ref.dtype)


def _diff_prompt(lw, sub_col, qt, k, vt, *, tq, lam_init):
    b, t, _ = k.shape
    nt = t // tq
    grid = (b, N_HEADS, nt)
    key = lax.broadcasted_iota(jnp.int32, (tq, 2 * tq), 0)
    qry = lax.broadcasted_iota(jnp.int32, (tq, 2 * tq), 1) % tq
    hide = jnp.where(key // CHUNK <= qry // CHUNK, 0.0, NEG_INF).astype(F32)
    return pl.pallas_call(
        functools.partial(_diff_prompt_kernel, tq=tq, lam_init=lam_init),
        grid=grid,
        in_specs=[
            pl.BlockSpec(lw.shape, lambda bi, h, i: (0, 0)),
            pl.BlockSpec((HEAD_W, 1), lambda bi, h, i: (0, 0)),
            pl.BlockSpec(hide.shape, lambda bi, h, i: (0, 0)),
            pl.BlockSpec((1, 1, 1, HEAD_W, tq), lambda bi, h, i: (bi, h, i, 0, 0)),
            pl.BlockSpec((1, t, HEAD_W), lambda bi, h, i: (bi, 0, h)),
            pl.BlockSpec((1, 1, nt, HEAD_W, tq), lambda bi, h, i: (bi, h, 0, 0, 0)),
        ],
        out_specs=pl.BlockSpec((1, tq, HEAD_W), lambda bi, h, i: (bi, i, h)),
        out_shape=jax.ShapeDtypeStruct((b, t, W_BRANCH), BF16),
        scratch_shapes=[
            pltpu.VMEM((HEAD_W, 2 * tq), BF16),
            pltpu.VMEM((tq, 2 * tq), F32),
            pltpu.VMEM((tq, 2 * tq), F32),
            pltpu.VMEM((1, 2 * tq), F32),
            pltpu.VMEM((1, 2 * tq), F32),
            pltpu.VMEM((HEAD_W, 2 * tq), F32),
        ],
        compiler_params=_cparams(("arbitrary", "arbitrary", "arbitrary")),
        name="diff_prompt",
    )(lw, sub_col, hide, qt, k, vt)


def _diff_sample_kernel(lw_ref, sub_ref, q_ref, kn_ref, vn_ref, kp_ref, vp_ref, o_ref,
                        qz_s, m_s, l_s, acc_s, *, tq, lam_init):
    kt = pl.program_id(1)
    rows = N_HEADS * 2 * tq
    cdt = qz_s.dtype

    def fold(k_ref, v_ref, new):
        n = k_ref.shape[2] // N_HEADS
        heads = range(N_HEADS)
        mine = [pl.ds(hd * 2 * tq, 2 * tq) for hd in heads]
        scores = []
        for hd in heads:
            k = k_ref[0, 0, pl.ds(hd, n, stride=N_HEADS), :].astype(cdt)
            s = _mm(qz_s[mine[hd], :], k, NT_DIMS)
            if new:
                row = lax.broadcasted_iota(jnp.int32, s.shape, 0)
                col = lax.broadcasted_iota(jnp.int32, s.shape, 1)
                s = jnp.where((col >> 6) <= ((row & (tq - 1)) >> 6), s, NEG_INF)
            scores.append(s)
        m_prev = [m_s[mine[hd], :] for hd in heads]
        m_new = [jnp.maximum(m_prev[hd], jnp.max(scores[hd], axis=-1, keepdims=True)) for hd in heads]
        alpha = [jnp.exp2(m_prev[hd] - m_new[hd]) for hd in heads]
        probs = [jnp.exp2(scores[hd] - m_new[hd][:, :1]) for hd in heads]
        for hd in heads:
            v = v_ref[0, 0, pl.ds(hd, n, stride=N_HEADS), :].astype(cdt)
            l_s[mine[hd], :] = alpha[hd] * l_s[mine[hd], :] + jnp.sum(probs[hd], axis=-1, keepdims=True)
            acc_s[mine[hd], :] = alpha[hd] * acc_s[mine[hd], :] + _mm(probs[hd].astype(cdt), v)
            m_s[mine[hd], :] = m_new[hd]

    @pl.when(kt == 0)
    def _():
        q = q_ref[0]
        lane = lax.broadcasted_iota(jnp.int32, (tq, HEAD_W), 1)
        parts = []
        for hd in range(N_HEADS):
            qh = q[:, hd * HEAD_W:(hd + 1) * HEAD_W]
            zero = jnp.zeros_like(qh)
            parts += [jnp.where(lane < HD_A, qh, zero), jnp.where(lane >= HD_A, qh, zero)]
        qz_s[...] = jnp.concatenate(parts, axis=0)
        m_s[...] = jnp.full((rows, LANES), NEG_INF, F32)
        l_s[...] = jnp.zeros((rows, LANES), F32)
        acc_s[...] = jnp.zeros((rows, HEAD_W), F32)
        fold(kn_ref, vn_ref, True)

    fold(kp_ref, vp_ref, False)

    @pl.when(kt == pl.num_programs(1) - 1)
    def _():
        lam = _diff_lambda(lw_ref, lam_init)
        o = acc_s[...] / l_s[...]
        outs = []
        for hd in range(N_HEADS):
            base = hd * 2 * tq
            oh = o[base:base + tq] - lam * o[base + tq:base + 2 * tq]
            r = lax.rsqrt(jnp.mean(oh * oh, axis=-1, keepdims=True) + EPS)
            outs.append((oh * r) * sub_ref[...] * (1.0 - lam_init))
        o_ref[0] = jnp.concatenate(outs, axis=1).astype(o_ref.dtype)


def _diff_sample(lw, sub_row, q, k_new, v_new, k_past, v_past, layer, *, key_rows, lam_init):
    b, tq, _ = q.shape
    past_rows = k_past.shape[2]
    grid = (b, past_rows // key_rows)
    new_spec = pl.BlockSpec((1, 1, tq * N_HEADS, HEAD_W), lambda bi, kt: (layer, bi, 0, 0))
    past_spec = pl.BlockSpec((1, 1, key_rows, HEAD_W), lambda bi, kt: (layer, bi, kt, 0))
    rows = N_HEADS * 2 * tq
    return pl.pallas_call(
        functools.partial(_diff_sample_kernel, tq=tq, lam_init=lam_init),
        grid=grid,
        in_specs=[
            pl.BlockSpec(lw.shape, lambda bi, kt: (0, 0)),
            pl.BlockSpec((1, HEAD_W), lambda bi, kt: (0, 0)),
            pl.BlockSpec((1, tq, W_BRANCH), lambda bi, kt: (bi, 0, 0)),
            new_spec, new_spec, past_spec, past_spec,
        ],
        out_specs=pl.BlockSpec((1, tq, W_BRANCH), lambda bi, kt: (bi, 0, 0)),
        out_shape=jax.ShapeDtypeStruct((b, tq, W_BRANCH), q.dtype),
        scratch_shapes=[
            pltpu.VMEM((rows, HEAD_W), q.dtype),
            pltpu.VMEM((rows, LANES), F32),
            pltpu.VMEM((rows, LANES), F32),
            pltpu.VMEM((rows, HEAD_W), F32),
        ],
        compiler_params=_cparams(("arbitrary", "arbitrary")),
        name="diff_sample",
    )(lw, sub_row, q, k_new, v_new, k_past, v_past)


def _neg_softplus(z):
    return -(jnp.maximum(z, 0.0) + jnp.log(1.0 + jnp.exp(-jnp.abs(z))))


def _suffix_sums(x, upper):
    hi, lo = _split_bf16(x)
    return (jnp.dot(hi, upper, preferred_element_type=F32)
            + jnp.dot(lo, upper, preferred_element_type=F32))


def _upper(n, dtype=BF16):
    j = lax.broadcasted_iota(jnp.int32, (n, n), 0)
    s = lax.broadcasted_iota(jnp.int32, (n, n), 1)
    return jnp.where(j >= s, 1.0, 0.0).astype(dtype)


def _stick_prompt_kernel(q_ref, kd_ref, vd_ref, kq_ref, vq_ref, kp_ref, vp_ref, o_ref, up_s, c_s, acc_s,
                         *, tq):
    i = pl.program_id(2)
    has_prev = jnp.where(i > 0, 1.0, 0.0)
    row = lax.broadcasted_iota(jnp.int32, (tq, tq), 0)
    col = lax.broadcasted_iota(jnp.int32, (tq, tq), 1)
    earlier = col < row
    q = q_ref[0]
    up_s[...] = _upper(tq)
    z = lax.dot_general(q, kd_ref[0], NT_DIMS, preferred_element_type=F32)
    zq = lax.dot_general(q, kq_ref[0], NT_DIMS, preferred_element_type=F32)
    lneg = jnp.where(earlier, _neg_softplus(z), 0.0)
    lnq = _neg_softplus(zq)
    cs = _suffix_sums(lneg, up_s[...])
    csq = _suffix_sums(lnq, up_s[...])
    c0 = jnp.sum(lneg, axis=-1, keepdims=True)
    a = jnp.where(earlier, jnp.exp(z + cs), 0.0)
    aq = jnp.exp(zq + csq + c0) * has_prev
    acc_s[...] = (jnp.dot(a.astype(BF16), vd_ref[0], preferred_element_type=F32)
                  + jnp.dot(aq.astype(BF16), vq_ref[0], preferred_element_type=F32))
    c1 = c0 + jnp.sum(lnq, axis=-1, keepdims=True) * has_prev
    c_s[...] = c1

    def cond(state):
        j, c_max = state
        return jnp.logical_and(j >= 0, c_max > STICK_DEAD_LOG)

    def body(state):
        j, _ = state
        start = pl.multiple_of(j * tq, tq)
        k = kp_ref[0, pl.ds(start, tq), :]
        v = vp_ref[0, pl.ds(start, tq), :]
        zj = lax.dot_general(q_ref[0], k, NT_DIMS, preferred_element_type=F32)
        ln = _neg_softplus(zj)
        csj = _suffix_sums(ln, up_s[...])
        c = c_s[...]
        aj = jnp.exp(zj + csj + c)
        acc_s[...] += jnp.dot(aj.astype(BF16), v, preferred_element_type=F32)
        c_new = c + jnp.sum(ln, axis=-1, keepdims=True)
        c_s[...] = c_new
        return j - 1, jnp.max(c_new)

    lax.while_loop(cond, body, (i - 2, jnp.max(c1)))
    o_ref[0] = acc_s[...].astype(o_ref.dtype)


def _stick_prompt(q, k, v, *, tq):
    b, t, _ = q.shape
    grid = (b, N_HEADS, t // tq)
    tile = pl.BlockSpec((1, tq, HEAD_W), lambda bi, h, i: (bi, i, h))
    before = pl.BlockSpec((1, tq, HEAD_W), lambda bi, h, i: (bi, jnp.maximum(i - 1, 0), h))
    whole = pl.BlockSpec((1, t, HEAD_W), lambda bi, h, i: (bi, 0, h))
    return pl.pallas_call(
        functools.partial(_stick_prompt_kernel, tq=tq),
        grid=grid,
        in_specs=[tile, tile, tile, before, before, whole, whole],
        out_specs=tile,
        out_shape=jax.ShapeDtypeStruct((b, t, W_BRANCH), BF16),
        scratch_shapes=[
            pltpu.VMEM((tq, tq), BF16),
            pltpu.VMEM((tq, 1), F32),
            pltpu.VMEM((tq, HEAD_W), F32),
        ],
        compiler_params=_cparams(("arbitrary", "arbitrary", "arbitrary")),
        name="stick_prompt",
    )(q, k, v, k, v, k, v)


STICK_RECENT_EXTRA = 2


def _stick_recent_kernel(q_ref, kd_ref, vd_ref, kq_ref, vq_ref, *rest, tq):
    older = rest[:2 * STICK_RECENT_EXTRA]
    o_ref, rest_ref, c_s, acc_s, worst_s = rest[2 * STICK_RECENT_EXTRA:]
    i = pl.program_id(1)
    has_prev = jnp.where(i > 0, 1.0, 0.0)
    row = lax.broadcasted_iota(jnp.int32, (tq, tq), 0)
    col = lax.broadcasted_iota(jnp.int32, (tq, tq), 1)
    earlier = col < row
    upper = _upper(tq)
    heads = [slice(hd * HEAD_W, (hd + 1) * HEAD_W) for hd in range(N_HEADS)]
    worst = None
    for hd, sl in enumerate(heads):
        q = q_ref[0, :, sl]
        z = lax.dot_general(q, kd_ref[0, :, sl], NT_DIMS, preferred_element_type=F32)
        zq = lax.dot_general(q, kq_ref[0, :, sl], NT_DIMS, preferred_element_type=F32)
        lneg = jnp.where(earlier, _neg_softplus(z), 0.0)
        lnq = _neg_softplus(zq)
        cs = _suffix_sums(lneg, upper)
        csq = _suffix_sums(lnq, upper)
        c0 = jnp.sum(lneg, axis=-1, keepdims=True)
        a = jnp.where(earlier, jnp.exp(z + cs), 0.0)
        aq = jnp.exp(zq + csq + c0) * has_prev
        acc_s[:, sl] = (jnp.dot(a.astype(BF16), vd_ref[0, :, sl], preferred_element_type=F32)
                        + jnp.dot(aq.astype(BF16), vq_ref[0, :, sl], preferred_element_type=F32))
        c1 = c0 + jnp.sum(lnq, axis=-1, keepdims=True) * has_prev
        c_s[hd] = c1
        c_max = jnp.max(c1)
        worst = c_max if worst is None else jnp.maximum(worst, c_max)
    worst_s[0] = worst

    for e in range(STICK_RECENT_EXTRA):
        k_ref, v_ref = older[2 * e], older[2 * e + 1]

        @pl.when(jnp.logical_and(worst_s[0] > STICK_DEAD_LOG, i >= e + 2))
        def _(k_ref=k_ref, v_ref=v_ref):
            worst_e = None
            for hd, sl in enumerate(heads):
                zj = lax.dot_general(q_ref[0, :, sl], k_ref[0, :, sl], NT_DIMS, preferred_element_type=F32)
                ln = _neg_softplus(zj)
                csj = _suffix_sums(ln, upper)
                c = c_s[hd]
                aj = jnp.exp(zj + csj + c)
                acc_s[:, sl] += jnp.dot(aj.astype(BF16), v_ref[0, :, sl], preferred_element_type=F32)
                c_new = c + jnp.sum(ln, axis=-1, keepdims=True)
                c_s[hd] = c_new
                c_max = jnp.max(c_new)
                worst_e = c_max if worst_e is None else jnp.maximum(worst_e, c_max)
            worst_s[0] = worst_e

    o_ref[0] = acc_s[...].astype(o_ref.dtype)
    left = jnp.where(i >= STICK_RECENT_EXTRA + 2, worst_s[0], 2.0 * STICK_DEAD_LOG)
    rest_ref[...] = jnp.full(rest_ref.shape, left, F32)


def _stick_recent(q, k, v, *, tq):
    b, t, _ = q.shape
    nt = t // tq
    tile = pl.BlockSpec((1, tq, W_BRANCH), lambda bi, i: (bi, i, 0))

    def back(n):
        return pl.BlockSpec((1, tq, W_BRANCH), lambda bi, i: (bi, jnp.maximum(i - n, 0), 0))

    n_older = STICK_RECENT_EXTRA
    return pl.pallas_call(
        functools.partial(_stick_recent_kernel, tq=tq),
        grid=(b, nt),
        in_specs=[tile, tile, tile, back(1), back(1)] + [back(2 + e // 2) for e in range(2 * n_older)],
        out_specs=[tile, pl.BlockSpec((1, 1, 8, LANES), lambda bi, i: (bi, i, 0, 0))],
        out_shape=[jax.ShapeDtypeStruct((b, t, W_BRANCH), BF16),
                   jax.ShapeDtypeStruct((b, nt, 8, LANES), F32)],
        scratch_shapes=[
            pltpu.VMEM((N_HEADS, tq, 1), F32),
            pltpu.VMEM((tq, W_BRANCH), F32),
            pltpu.SMEM((1,), F32),
        ],
        compiler_params=_cparams(("arbitrary", "arbitrary")),
        name="stick_recent",
    )(q, k, v, k, v, *([k, v] * n_older))


def _stick_sample_kernel(q_ref, kn_ref, vn_ref, kp_ref, vp_ref, o_ref, rest_ref,
                         q_s, up_s, c_s, acc_s, live_s, *, tq, sub_rows):
    kt = pl.program_id(1)
    rows = N_HEADS * tq

    def fold(k, v, new):
        n = k.shape[0]
        z = _mm(q_s[...], k.astype(q_s.dtype), NT_DIMS)
        row = lax.broadcasted_iota(jnp.int32, (rows, n), 0)
        col = lax.broadcasted_iota(jnp.int32, (rows, n), 1)
        keep = (row >> _log2(tq)) == (col & (N_HEADS - 1))
        if new:
            keep = jnp.logical_and(keep, (col >> 2) < (row & (tq - 1)))
        ln = jnp.where(keep, _neg_softplus(z), 0.0)
        cs = _suffix_sums(ln, _upper(n, up_s.dtype) if new else up_s[...])
        c = c_s[...]
        a = jnp.where(keep, jnp.exp(z + cs + c), 0.0)
        acc_s[...] += _mm(a.astype(q_s.dtype), v.astype(q_s.dtype))
        c_new = c + jnp.sum(ln, axis=-1, keepdims=True)
        c_s[...] = c_new
        live_s[0] = (jnp.max(c_new) > STICK_DEAD_LOG).astype(jnp.int32)

    @pl.when(kt == 0)
    def _():
        q = q_ref[0]
        q_s[...] = jnp.concatenate([q[:, hd * HEAD_W:(hd + 1) * HEAD_W] for hd in range(N_HEADS)], axis=0)
        up_s[...] = _upper(sub_rows, up_s.dtype)
        c_s[...] = jnp.zeros((rows, 1), F32)
        acc_s[...] = jnp.zeros((rows, HEAD_W), F32)
        fold(kn_ref[0, 0], vn_ref[0, 0], True)

    n_sub = kp_ref.shape[2] // sub_rows
    for sb in reversed(range(n_sub)):
        @pl.when(live_s[0] > 0)
        def _(sb=sb):
            sl = slice(sb * sub_rows, (sb + 1) * sub_rows)
            fold(kp_ref[0, 0, sl, :], vp_ref[0, 0, sl, :], False)

    @pl.when(kt == pl.num_programs(1) - 1)
    def _():
        acc = acc_s[...]
        o_ref[0] = jnp.concatenate([acc[hd * tq:(hd + 1) * tq] for hd in range(N_HEADS)],
                                   axis=1).astype(o_ref.dtype)
        rest_ref[...] = jnp.full(rest_ref.shape, jnp.max(c_s[...]), F32)


def _stick_sample(q, k_new, v_new, k_past, v_past, layer, *, key_rows, sub_rows, newest_only):
    b, tq, _ = q.shape
    n_blocks = k_past.shape[2] // key_rows
    grid = (b, 1 if newest_only else n_blocks)
    new_spec = pl.BlockSpec((1, 1, tq * N_HEADS, HEAD_W), lambda bi, kt: (layer, bi, 0, 0))
    past_spec = pl.BlockSpec((1, 1, key_rows, HEAD_W), lambda bi, kt: (layer, bi, n_blocks - 1 - kt, 0))
    rows = N_HEADS * tq
    return pl.pallas_call(
        functools.partial(_stick_sample_kernel, tq=tq, sub_rows=sub_rows),
        grid=grid,
        in_specs=[pl.BlockSpec((1, tq, W_BRANCH), lambda bi, kt: (bi, 0, 0)),
                  new_spec, new_spec, past_spec, past_spec],
        out_specs=[pl.BlockSpec((1, tq, W_BRANCH), lambda bi, kt: (bi, 0, 0)),
                   pl.BlockSpec((1, 8, LANES), lambda bi, kt: (bi, 0, 0))],
        out_shape=[jax.ShapeDtypeStruct((b, tq, W_BRANCH), q.dtype),
                   jax.ShapeDtypeStruct((b, 8, LANES), F32)],
        scratch_shapes=[
            pltpu.VMEM((rows, HEAD_W), q.dtype),
            pltpu.VMEM((sub_rows, sub_rows), BF16),
            pltpu.VMEM((rows, 1), F32),
            pltpu.VMEM((rows, HEAD_W), F32),
            pltpu.SMEM((1,), jnp.int32),
        ],
        compiler_params=_cparams(("arbitrary", "arbitrary")),
        name="stick_sample",
    )(q, k_new, v_new, k_past, v_past)


def _router_gates_t(logits_t, bias_col):
    n = EXPERTS_PER_GROUP
    aff = jax.nn.sigmoid(logits_t)
    sel = aff + bias_col
    pos = [sel[n * k:n * (k + 1)] for k in range(n)]
    in_top2 = []
    for j in range(n):
        rank = jnp.zeros(pos[j].shape, jnp.int32)
        for i in range(n):
            if i != j:
                ahead = (pos[i] >= pos[j]) if i < j else (pos[i] > pos[j])
                rank = rank + jnp.where(ahead, 1, 0)
        in_top2.append(rank < 2)
    score = sum(jnp.where(in_top2[j], pos[j], 0.0) for j in range(n))
    grp = lax.broadcasted_iota(jnp.int32, score.shape, 0)
    beaten = jnp.zeros(score.shape, jnp.int32)
    for g2 in range(N_GROUPS):
        other = score[g2:g2 + 1]
        ahead = jnp.logical_or(other > score, jnp.logical_and(other == score, g2 < grp))
        beaten = beaten + jnp.where(ahead, 1, 0)
    chosen = beaten == 0
    w = [jnp.where(jnp.logical_and(in_top2[j], chosen), aff[n * j:n * (j + 1)], 0.0) for j in range(n)]
    total = jnp.sum(sum(w), axis=0, keepdims=True)
    return jnp.concatenate([wj / total for wj in w], axis=0)


def _mixout_kernel(x_ref, oa_ref, ob_ref, ga_ref, gb_ref, g1_ref, sc_ref, sh_ref, gain_ref,
                   wa_ref, wb_ref, wo_ref, wr_ref, br_ref, x1_ref, h2_ref, gates_ref):
    bb, tt, d = x_ref.shape
    rows = bb * tt
    ya = _mm(oa_ref[...].reshape(rows, W_BRANCH), wa_ref[...])
    yb = _mm(ob_ref[...].reshape(rows, W_BRANCH), wb_ref[...])
    y = (ga_ref[...].reshape(rows, d).astype(F32) * ya + gb_ref[...].reshape(rows, d).astype(F32) * yb)
    mix = _mm(y.astype(wo_ref.dtype), wo_ref[...])
    x1 = x_ref[...] + g1_ref[...] * mix.reshape(bb, tt, d)
    x1_ref[...] = x1
    r = lax.rsqrt(jnp.mean(x1 * x1, axis=-1, keepdims=True) + EPS)
    h2 = ((x1 * r) * gain_ref[...] * (1.0 + sc_ref[...]) + sh_ref[...]).reshape(rows, d)
    h2_ref[...] = h2.reshape(bb, tt, d).astype(h2_ref.dtype)
    logits_t = _mm(wr_ref[...], h2.astype(wr_ref.dtype), NT_DIMS)
    gates_t = _router_gates_t(logits_t, br_ref[...])
    pad = jnp.zeros((LANES - N_EXPERTS, rows), F32)
    gates_ref[...] = jnp.concatenate([gates_t, pad], axis=0).T.reshape(bb, tt, LANES)


def _mixout(x, oa, ob, ga, gb, g1, sc2, sh2, gain, wa, wb, wo, wr, br, bb, tt, layer):
    b, t, d = x.shape
    grid = (b // bb, t // tt)
    xmap = lambda i, j: (i, j, 0)
    modspec = pl.BlockSpec((bb, 1, d), lambda i, j: (i, 0, 0))

    def full(a):
        return pl.BlockSpec(a.shape, lambda i, j: (0,) * a.ndim)

    def of_layer(a):
        return pl.BlockSpec((None,) + a.shape[1:], lambda i, j: (layer,) + (0,) * (a.ndim - 1))

    return pl.pallas_call(
        _mixout_kernel,
        grid=grid,
        in_specs=[
            pl.BlockSpec((bb, tt, d), xmap),
            pl.BlockSpec((bb, tt, W_BRANCH), xmap), pl.BlockSpec((bb, tt, W_BRANCH), xmap),
            pl.BlockSpec((bb, tt, d), xmap), pl.BlockSpec((bb, tt, d), xmap),
            modspec, modspec, modspec,
            pl.BlockSpec((1, 1, d), lambda i, j: (0, 0, 0)),
            of_layer(wa), of_layer(wb), of_layer(wo), full(wr), full(br),
        ],
        out_specs=[pl.BlockSpec((bb, tt, d), xmap), pl.BlockSpec((bb, tt, d), xmap),
                   pl.BlockSpec((bb, tt, LANES), xmap)],
        out_shape=[jax.ShapeDtypeStruct((b, t, d), F32), jax.ShapeDtypeStruct((b, t, d), BF16),
                   jax.ShapeDtypeStruct((b, t, LANES), F32)],
        compiler_params=_cparams(("arbitrary", "arbitrary")),
        name="mixout",
    )(x, oa, ob, ga, gb, g1, sc2, sh2, gain.reshape(1, 1, d), wa, wb, wo, wr, br)


def _moe_kernel(h_ref, gates_ref, x1_ref, g2_ref, gain_ref, wg_ref, wu_ref, wd_ref, o_ref, acc_s,
                *, final_norm):
    bb, tt, d = h_ref.shape
    rows = bb * tt
    grp = pl.program_id(2)
    h = h_ref[...].reshape(rows, d)
    gates = gates_ref[...].reshape(rows, LANES)
    lane = lax.broadcasted_iota(jnp.int32, (rows, LANES), 1)

    @pl.when(grp == 0)
    def _():
        acc_s[...] = jnp.zeros(acc_s.shape, F32)

    hidden = []
    for k in range(EXPERTS_PER_GROUP):
        g = jnp.dot(h, wg_ref[k], preferred_element_type=F32)
        u = jnp.dot(h, wu_ref[k], preferred_element_type=F32)
        gate = jnp.sum(jnp.where(lane == EXPERTS_PER_GROUP * k + grp, gates, 0.0), axis=-1, keepdims=True)
        hidden.append(((g * jax.nn.sigmoid(g)) * u * gate).astype(BF16))
    acc_s[...] += jnp.dot(jnp.concatenate(hidden, axis=1), wd_ref[...], preferred_element_type=F32)

    @pl.when(grp == N_GROUPS - 1)
    def _():
        x2 = x1_ref[...] + g2_ref[...] * acc_s[...].reshape(bb, tt, d)
        if final_norm:
            r = lax.rsqrt(jnp.mean(x2 * x2, axis=-1, keepdims=True) + EPS)
            x2 = (x2 * r) * gain_ref[...]
        o_ref[...] = x2


def _moe(h2, gates, x1, g2, gain, wg, wu, wd, bb, tt, layer, final_norm):
    b, t, d = x1.shape
    grid = (b // bb, t // tt, N_GROUPS)
    xmap = lambda i, j, e: (i, j, 0)
    n = EXPERTS_PER_GROUP
    return pl.pallas_call(
        functools.partial(_moe_kernel, final_norm=final_norm),
        grid=grid,
        in_specs=[
            pl.BlockSpec((bb, tt, d), xmap),
            pl.BlockSpec((bb, tt, LANES), xmap),
            pl.BlockSpec((bb, tt, d), xmap),
            pl.BlockSpec((bb, 1, d), lambda i, j, e: (i, 0, 0)),
            pl.BlockSpec((1, 1, d), lambda i, j, e: (0, 0, 0)),
            pl.BlockSpec((None, n, d, D_EXPERT), lambda i, j, e: (layer, e, 0, 0)),
            pl.BlockSpec((None, n, d, D_EXPERT), lambda i, j, e: (layer, e, 0, 0)),
            pl.BlockSpec((None, None, n * D_EXPERT, d), lambda i, j, e: (layer, e, 0, 0)),
        ],
        out_specs=pl.BlockSpec((bb, tt, d), xmap),
        out_shape=jax.ShapeDtypeStruct((b, t, d), F32),
        scratch_shapes=[pltpu.VMEM((bb * tt, d), F32)],
        compiler_params=_cparams(("arbitrary", "arbitrary", "arbitrary")),
        name="moe",
    )(h2, gates, x1, g2, gain.reshape(1, 1, d), wg, wu, wd)


def _rope_tables(pos, freq_major):
    half = HD_A // 2
    inv = ROPE_THETA ** (-jnp.arange(half, dtype=F32) / half)
    if freq_major:
        ang = inv[:, None] * pos.astype(F32)[None, :]
    else:
        ang = pos.astype(F32)[:, None] * inv[None, :]
    return jnp.cos(ang), jnp.sin(ang)


def _trunk(x, mod, pos, past, p, *, row_block, moe_rows, attn_tile):
    b, t, d = x.shape
    bb, tt = row_block
    depth = p["norm_mix"].shape[0]
    prompt = past is None
    cos, sin = _rope_tables(pos, freq_major=past is None)
    caches = None
    for l in range(depth):
        wl = p["f32"] if (not prompt and l == 0) else p["bf16"]
        sh1, sc1, g1, sh2, sc2, g2 = [mod[l, :, i][:, None, :] for i in range(6)]
        outs = _inproj(x, p["norm_mix"][l], sc1, sh1, cos, sin, wl["w_in"], bb, tt, l, depth,
                       caches, feature_major=prompt)
        caches = outs[:4]
        qa, qb, kab, vab, kbb, vbb, ga, gb = outs[4:]
        lam_init = 0.8 - 0.6 * math.exp(-0.3 * l)
        lw, sub = p["a_lambda"][l], p["a_subln"][l]
        if prompt:
            oa = _diff_prompt(lw, sub[:, None], qa, kab, vab, tq=attn_tile, lam_init=lam_init)
            stick_tile = min(t, 256)
            ob, rest = _stick_recent(qb, kbb, vbb, tq=stick_tile)
            ob = lax.cond(jnp.max(rest) > STICK_DEAD_LOG,
                          lambda: _stick_prompt(qb, kbb, vbb, tq=stick_tile), lambda: ob)
        else:
            oa = _diff_sample(lw, sub[None, :], qa, caches[0], caches[1], past[0], past[1], l,
                              key_rows=min(past[0].shape[2], 4 * attn_tile), lam_init=lam_init)
            stick = functools.partial(_stick_sample, qb, caches[2], caches[3], past[2], past[3], l,
                                      key_rows=attn_tile, sub_rows=min(attn_tile, 512))
            ob, rest = stick(newest_only=True)
            if past[2].shape[2] > attn_tile:
                ob = lax.cond(jnp.max(rest) > STICK_DEAD_LOG,
                              lambda: stick(newest_only=False)[0], lambda: ob)
        x1, h2, gates = _mixout(x, oa, ob, ga, gb, g1, sc2, sh2, p["norm_ffn"][l],
                                wl["w_proj_a"], wl["w_proj_b"], wl["w_out"],
                                wl["w_router"], p["b_router"], bb, tt, l)
        x = _moe(h2, gates, x1, g2, p["norm_final"], p["w_e_gate"], p["w_e_up"],
                 p["w_e_down"], bb, moe_rows, l, final_norm=(l == depth - 1))
    return (x,) + tuple(c.reshape(depth, b, t, N_HEADS, HEAD_W) for c in caches)


def kernel(x_prompt, x_sample, cache_a_k, cache_a_v, cache_b_k, cache_b_v, c_prompt, c_sample,
           w_in, w_proj_a, w_proj_b, w_out, a_lambda, a_subln, w_ada, b_ada,
           norm_mix, norm_ffn, norm_final, w_router, b_router, w_e_gate, w_e_up, w_e_down):
    d = x_prompt.shape[-1]
    bp, tp = x_prompt.shape[:2]
    bs, ts = x_sample.shape[:2]
    depth = w_in.shape[0]
    past_len = cache_a_k.shape[2]
    assert d == D_MODEL and past_len % CHUNK == 0 and ts <= CHUNK

    wr = w_router.T.reshape(N_GROUPS, EXPERTS_PER_GROUP, d).transpose(1, 0, 2).reshape(N_EXPERTS, d)
    br = b_router.reshape(N_GROUPS, EXPERTS_PER_GROUP).T.reshape(N_EXPERTS, 1)
    mixer_f32 = dict(w_in=w_in, w_proj_a=w_proj_a, w_proj_b=w_proj_b, w_out=w_out, w_router=wr)
    p = dict(
        f32=mixer_f32, bf16={k: v.astype(BF16) for k, v in mixer_f32.items()},
        a_lambda=a_lambda, a_subln=a_subln,
        norm_mix=norm_mix, norm_ffn=norm_ffn, norm_final=norm_final, b_router=br,
        w_e_gate=w_e_gate.astype(BF16), w_e_up=w_e_up.astype(BF16),
        w_e_down=w_e_down.astype(BF16).reshape(depth, N_GROUPS, EXPERTS_PER_GROUP * D_EXPERT, d),
    )

    n_c = bp + bs
    rows = -(-n_c // 8) * 8
    c_all = jnp.zeros((rows, d), F32).at[:bp].set(c_prompt).at[bp:n_c].set(c_sample)
    mod = _ada_mod(c_all, w_ada, b_ada).reshape(depth, rows, 6, d)

    pos_p = jnp.arange(tp, dtype=jnp.int32)
    pos_s = past_len + jnp.arange(ts, dtype=jnp.int32)
    tile_p = min(tp, 512)
    out_p = _trunk(x_prompt, mod[:, :bp], pos_p, None, p,
                   row_block=(1, tile_p), moe_rows=min(tp, 1024), attn_tile=tile_p)
    past = tuple(c.reshape(depth, bs, past_len * N_HEADS, HEAD_W)
                 for c in (cache_a_k, cache_a_v, cache_b_k, cache_b_v))
    out_s = _trunk(x_sample, mod[:, bp:n_c], pos_s, past, p,
                   row_block=(bs, ts), moe_rows=ts, attn_tile=min(past_len * N_HEADS, 2048))
    return (out_p[0], out_s[0]) + out_p[1:] + out_s[1:]
```
